```python
import math
import jax, jax.numpy as jnp
from jax import lax
import numpy as np

D_MODEL = 2048
BATCH = 8
SEQ = 2048
DEPTH = 2

CTX_LEN = 256
GRID_W = 64
HEAD_DIM = 128
ROPE_THETA = 10000.0
Q_BLOCK = 128
EPS = 1e-6
DIFF_HEADS = 4
DIFF_V_DIM = 2 * HEAD_DIM
LRU_WIDTH = 1024
LRU_BLOCKS = 8
LRU_BLOCK_W = LRU_WIDTH // LRU_BLOCKS
CONV_W = 4
LRU_C = 8.0
GQA_HEADS = 8
GQA_KV_HEADS = 2
N_BRANCH = 3
BRANCH_W = 1024
N_GROUPS = 4
EXPERTS_PER_GROUP = 8
N_EXPERTS = N_GROUPS * EXPERTS_PER_GROUP
TOP_K = 2
D_EXPERT = 512
N_MOD = 6

A_Q = 2 * DIFF_HEADS * HEAD_DIM
A_K = 2 * DIFF_HEADS * HEAD_DIM
A_V = DIFF_HEADS * DIFF_V_DIM
C_Q = GQA_HEADS * HEAD_DIM
C_KV = GQA_KV_HEADS * HEAD_DIM
SPLITS = (A_Q, A_K, A_V, LRU_WIDTH, LRU_WIDTH, C_Q, C_KV, C_KV, N_BRANCH * D_MODEL)
IN_COLS = A_Q + A_K + A_V + 2 * LRU_WIDTH + C_Q + 2 * C_KV + N_BRANCH * D_MODEL

kernel_name = "hybrid_diff_rglru_gqa_hmoe_dit"

F32 = jnp.float32


def rms_norm(x, g):
    xf = x.astype(F32)
    y = xf * lax.rsqrt(jnp.mean(xf * xf, axis=-1, keepdims=True) + EPS)
    return (y * g.astype(F32)).astype(x.dtype)


def modulate(x, shift, scale):
    return x * (1 + scale) + shift


def axial_rope_tables(n_tokens):
    n_rows = n_tokens // GRID_W
    row = jnp.repeat(jnp.arange(n_rows), GRID_W).astype(F32)
    col = jnp.tile(jnp.arange(GRID_W), n_rows).astype(F32)
    half = HEAD_DIM // 2
    inv = 1.0 / (ROPE_THETA ** (jnp.arange(0, half, 2, dtype=F32) / half))
    ang_r = row[:, None] * inv
    ang_c = col[:, None] * inv
    ang = jnp.concatenate([ang_r, ang_r, ang_c, ang_c], axis=-1)
    return jnp.cos(ang), jnp.sin(ang)


def _rot_half(u):
    u1, u2 = jnp.split(u, 2, axis=-1)
    return jnp.concatenate([-u2, u1], axis=-1)


def apply_axial_rope(x, cos, sin):
    xf = x.astype(F32)
    xr, xc = jnp.split(xf, 2, axis=-1)
    rot = jnp.concatenate([_rot_half(xr), _rot_half(xc)], axis=-1)
    return (xf * cos[:, None, :] + rot * sin[:, None, :]).astype(x.dtype)


def attn_probs(q, k):
    s = jnp.einsum('bqhgd,bkhd->bhgqk', q, k, preferred_element_type=F32) * (HEAD_DIM ** -0.5)
    return jax.nn.softmax(s, axis=-1)


def diff_attention(q, k, v, lam, lambda_init, sub_g):
    b, tq = q.shape[:2]
    p = attn_probs(q.reshape(b, tq, 2 * DIFF_HEADS, 1, HEAD_DIM), k)[:, :, 0]
    p = p.reshape(b, DIFF_HEADS, 2, tq, p.shape[-1])
    w = p[:, :, 0] - lam * p[:, :, 1]
    o = jnp.einsum('bhqk,bkhe->bqhe', w.astype(v.dtype), v)
    o = rms_norm(o, sub_g) * (1.0 - lambda_init)
    return o.reshape(b, tq, DIFF_HEADS * DIFF_V_DIM)


def gqa_attention(q, k, v):
    b, tq = q.shape[:2]
    qg = q.reshape(b, tq, GQA_KV_HEADS, GQA_HEADS // GQA_KV_HEADS, HEAD_DIM)
    p = attn_probs(qg, k)
    o = jnp.einsum('bhgqk,bkhd->bqhgd', p.astype(v.dtype), v)
    return o.reshape(b, tq, GQA_HEADS * HEAD_DIM)


def sweep_query_blocks(q, fn):
    b, t = q.shape[:2]
    nb = t // Q_BLOCK
    qb = q.reshape(b, nb, Q_BLOCK, *q.shape[2:]).swapaxes(0, 1)
    out = lax.map(fn, qb)
    return out.swapaxes(0, 1).reshape(b, t, *out.shape[3:])


def centred_depthwise_conv(x, w, bias):
    t = x.shape[1]
    left = CONV_W // 2
    right = CONV_W - 1 - left
    xp = jnp.pad(x, ((0, 0), (left, right), (0, 0)))
    out = xp[:, 0:t] * w[0]
    for k in range(1, CONV_W):
        out = out + xp[:, k:k + t] * w[k]
    return out + bias


def block_diag_linear(x, w, b):
    xb = x.reshape(*x.shape[:-1], LRU_BLOCKS, LRU_BLOCK_W)
    y = jnp.einsum('btnc,ncd->btnd', xb, w)
    return y.reshape(x.shape) + b


def linear_scan(a, b, h0, reverse):
    def combine(e1, e2):
        a1, b1 = e1
        a2, b2 = e2
        return a1 * a2, a2 * b1 + b2
    a_cum, h = lax.associative_scan(combine, (a, b), axis=1, reverse=reverse)
    return h + a_cum * h0[:, None, :]


def rglru_scan(xc, h0, w_a, b_a, w_x, b_x, lam, reverse):
    r = jax.nn.sigmoid(block_diag_linear(xc, w_a, b_a).astype(F32))
    i = jax.nn.sigmoid(block_diag_linear(xc, w_x, b_x).astype(F32))
    log_a = -LRU_C * r * jax.nn.softplus(-lam.astype(F32))
    a = jnp.exp(log_a)
    mult = jnp.sqrt(-jnp.expm1(2.0 * log_a))
    return linear_scan(a, mult * i * xc.astype(F32), h0, reverse)


def merge_branches(ya, yb, yc, gate_logits, b_gate, w_branch_a, w_branch_b, w_branch_c, w_out):
    g = jax.nn.sigmoid((gate_logits.reshape(*gate_logits.shape[:-1], N_BRANCH, D_MODEL) + b_gate).astype(F32)).astype(ya.dtype)
    m = g[..., 0, :] * (ya @ w_branch_a) + g[..., 1, :] * (yb @ w_branch_b) + g[..., 2, :] * (yc @ w_branch_c)
    return m @ w_out


def stream_projections(u, rope, w_in, q_norm_a, k_norm_a, q_norm_c, k_norm_c):
    b, t, _ = u.shape
    idx = np.cumsum(np.array(SPLITS))[:-1].tolist()
    aq, ak, av, bx, by, cq, ck, cv, gate_logits = jnp.split(u @ w_in, idx, axis=-1)
    aq = rms_norm(aq.reshape(b, t, 2 * DIFF_HEADS, HEAD_DIM), q_norm_a)
    ak = rms_norm(ak.reshape(b, t, 2 * DIFF_HEADS, HEAD_DIM), k_norm_a)
    cq = rms_norm(cq.reshape(b, t, GQA_HEADS, HEAD_DIM), q_norm_c)
    ck = rms_norm(ck.reshape(b, t, GQA_KV_HEADS, HEAD_DIM), k_norm_c)
    if rope is not None:
        cos, sin = rope
        aq = apply_axial_rope(aq, cos, sin)
        ak = apply_axial_rope(ak, cos, sin)
        cq = apply_axial_rope(cq, cos, sin)
        ck = apply_axial_rope(ck, cos, sin)
    av = av.reshape(b, t, DIFF_HEADS, DIFF_V_DIM)
    cv = cv.reshape(b, t, GQA_KV_HEADS, HEAD_DIM)
    return aq, ak, av, bx, by, cq, ck, cv, gate_logits


def mixer_sublayer(u_lat, u_ctx, rope, lambda_init, need_ctx_out, w_in, b_gate, q_norm_a, k_norm_a,
                   diff_lambda, sub_norm_a, q_norm_c, k_norm_c, conv_w, conv_b, lru_w_a, lru_b_a,
                   lru_w_x, lru_b_x, lru_lambda, w_branch_a, w_branch_b, w_branch_c, w_out):
    lam = (jnp.exp(jnp.sum(diff_lambda[0] * diff_lambda[1]).astype(F32))
           - jnp.exp(jnp.sum(diff_lambda[2] * diff_lambda[3]).astype(F32)) + lambda_init)
    aq_x, ak_x, av_x, bx_x, by_x, cq_x, ck_x, cv_x, gl_x = stream_projections(
        u_ctx, None, w_in, q_norm_a, k_norm_a, q_norm_c, k_norm_c)
    aq_l, ak_l, av_l, bx_l, by_l, cq_l, ck_l, cv_l, gl_l = stream_projections(
        u_lat, rope, w_in, q_norm_a, k_norm_a, q_norm_c, k_norm_c)

    xc_x = centred_depthwise_conv(bx_x, conv_w, conv_b)
    xc_l = centred_depthwise_conv(bx_l, conv_w, conv_b)
    zeros = jnp.zeros((u_ctx.shape[0], LRU_WIDTH), F32)
    hf_x = rglru_scan(xc_x, zeros, lru_w_a[0], lru_b_a[0], lru_w_x[0], lru_b_x[0], lru_lambda[0], False)
    hb_x = rglru_scan(xc_x, zeros, lru_w_a[1], lru_b_a[1], lru_w_x[1], lru_b_x[1], lru_lambda[1], True)
    hf_l = rglru_scan(xc_l, hf_x[:, -1], lru_w_a[0], lru_b_a[0], lru_w_x[0], lru_b_x[0], lru_lambda[0], False)
    hb_l = rglru_scan(xc_l, hb_x[:, 0], lru_w_a[1], lru_b_a[1], lru_w_x[1], lru_b_x[1], lru_lambda[1], True)
    yb_l = (hf_l + hb_l).astype(u_lat.dtype) * jax.nn.gelu(by_l)

    ka = jnp.concatenate([ak_x, ak_l], axis=1)
    va = jnp.concatenate([av_x, av_l], axis=1)
    ya_l = sweep_query_blocks(aq_l, lambda qb: diff_attention(qb, ka, va, lam, lambda_init, sub_norm_a))
    kc = jnp.concatenate([ck_x, ck_l], axis=1)
    vc = jnp.concatenate([cv_x, cv_l], axis=1)
    yc_l = sweep_query_blocks(cq_l, lambda qb: gqa_attention(qb, kc, vc))
    y_lat = merge_branches(ya_l, yb_l, yc_l, gl_l, b_gate, w_branch_a, w_branch_b, w_branch_c, w_out)
    if not need_ctx_out:
        return y_lat, None
    ya_x = diff_attention(aq_x, ak_x, av_x, lam, lambda_init, sub_norm_a)
    yc_x = gqa_attention(cq_x, ck_x, cv_x)
    yb_x = (hf_x + hb_x).astype(u_ctx.dtype) * jax.nn.gelu(by_x)
    y_ctx = merge_branches(ya_x, yb_x, yc_x, gl_x, b_gate, w_branch_a, w_branch_b, w_branch_c, w_out)
    return y_lat, y_ctx


def hierarchical_moe(t, w_group, b_group, w_route, b_route, w1, w3, w2):
    n = t.shape[0]
    gl = (t @ w_group).astype(F32) + b_group
    gp = jax.nn.softmax(gl, axis=-1)
    g_idx = jnp.argmax(gl, axis=-1)
    g_w = jnp.take_along_axis(gp, g_idx[:, None], axis=-1)
    el = ((t @ w_route).astype(F32) + b_route).reshape(n, N_GROUPS, EXPERTS_PER_GROUP)
    el_sel = jnp.take_along_axis(el, g_idx[:, None, None], axis=1)[:, 0]
    top_v, top_i = lax.top_k(el_sel, TOP_K)
    top_p = jax.nn.softmax(top_v, axis=-1) * g_w
    expert_id = g_idx[:, None] * EXPERTS_PER_GROUP + top_i
    combine = jnp.sum(jax.nn.one_hot(expert_id, N_EXPERTS, dtype=F32) * top_p[..., None], axis=1).astype(t.dtype)
    out = jnp.zeros_like(t)
    for e in range(N_EXPERTS):
        h = jax.nn.silu(t @ w1[e]) * (t @ w3[e])
        out = out + combine[:, e:e + 1] * (h @ w2[e])
    return out


def setup_inputs(seed: int = 0) -> dict:
    key = jax.random.key(seed)
    ks = list(jax.random.split(key, 40))

    def nrm(i, shape, scale):
        return jax.random.normal(ks[i], shape, F32) * scale

    L, D = DEPTH, D_MODEL
    u = jax.random.uniform(ks[39], (L, 2, LRU_WIDTH), F32, minval=0.9, maxval=0.999)
    a_base = u ** (1.0 / LRU_C)
    lru_lambda = jnp.log(a_base) - jnp.log1p(-a_base)
    return {
        "x": nrm(0, (BATCH, SEQ, D), 1.0),
        "c": nrm(1, (BATCH, D), 1.0),
        "ctx": nrm(2, (BATCH, CTX_LEN, D), 1.0),
        "c_ctx": nrm(3, (D,), 1.0),
        "w_ada": nrm(4, (L, D, N_MOD * D), 0.5 * D ** -0.5),
        "b_ada": nrm(5, (L, N_MOD * D), 0.01),
        "norm1_g": 1.0 + nrm(6, (L, D), 0.02),
        "norm2_g": 1.0 + nrm(7, (L, D), 0.02),
        "w_in": nrm(8, (L, D, IN_COLS), D ** -0.5),
        "b_gate": nrm(9, (L, N_BRANCH, D), 0.01),
        "q_norm_a": 1.0 + nrm(10, (L, HEAD_DIM), 0.02),
        "k_norm_a": 1.0 + nrm(11, (L, HEAD_DIM), 0.02),
        "diff_lambda": nrm(12, (L, 4, HEAD_DIM), 0.1),
        "sub_norm_a": 1.0 + nrm(13, (L, DIFF_V_DIM), 0.02),
        "q_norm_c": 1.0 + nrm(14, (L, HEAD_DIM), 0.02),
        "k_norm_c": 1.0 + nrm(15, (L, HEAD_DIM), 0.02),
        "conv_w": nrm(16, (L, CONV_W, LRU_WIDTH), CONV_W ** -0.5),
        "conv_b": nrm(17, (L, LRU_WIDTH), 0.01),
        "lru_w_a": nrm(18, (L, 2, LRU_BLOCKS, LRU_BLOCK_W, LRU_BLOCK_W), LRU_BLOCK_W ** -0.5),
        "lru_b_a": nrm(19, (L, 2, LRU_WIDTH), 0.01),
        "lru_w_x": nrm(20, (L, 2, LRU_BLOCKS, LRU_BLOCK_W, LRU_BLOCK_W), LRU_BLOCK_W ** -0.5),
        "lru_b_x": nrm(21, (L, 2, LRU_WIDTH), 0.01),
        "lru_lambda": lru_lambda,
        "w_branch_a": nrm(22, (L, BRANCH_W, D), BRANCH_W ** -0.5),
        "w_branch_b": nrm(23, (L, BRANCH_W, D), BRANCH_W ** -0.5),
        "w_branch_c": nrm(24, (L, BRANCH_W, D), BRANCH_W ** -0.5),
        "w_out": nrm(25, (L, D, D), D ** -0.5),
        "w_group": nrm(26, (L, D, N_GROUPS), D ** -0.5),
        "b_group": nrm(27, (L, N_GROUPS), 0.01),
        "w_route": nrm(28, (L, D, N_EXPERTS), D ** -0.5),
        "b_route": nrm(29, (L, N_EXPERTS), 0.01),
        "w1": nrm(30, (L, N_EXPERTS, D, D_EXPERT), D ** -0.5),
        "w3": nrm(31, (L, N_EXPERTS, D, D_EXPERT), D ** -0.5),
        "w2": nrm(32, (L, N_EXPERTS, D_EXPERT, D), D_EXPERT ** -0.5),
    }


def reference(x, c, ctx, c_ctx, w_ada, b_ada, norm1_g, norm2_g, w_in, b_gate, q_norm_a, k_norm_a,
              diff_lambda, sub_norm_a, q_norm_c, k_norm_c, conv_w, conv_b, lru_w_a, lru_b_a,
              lru_w_x, lru_b_x, lru_lambda, w_branch_a, w_branch_b, w_branch_c, w_out,
              w_group, b_group, w_route, b_route, w1, w3, w2):
    b, t, d = x.shape
    rope = axial_rope_tables(t)
    silu_c = jax.nn.silu(c)
    silu_cc = jax.nn.silu(c_ctx)
    h_lat, h_ctx = x, ctx
    for l in range(DEPTH):
        last = l == DEPTH - 1
        lambda_init = 0.8 - 0.6 * math.exp(-0.3 * l)
        mod_l = (silu_c @ w_ada[l] + b_ada[l]).reshape(b, N_MOD, 1, d)
        mod_x = (silu_cc @ w_ada[l] + b_ada[l]).reshape(N_MOD, d)
        u_lat = modulate(rms_norm(h_lat, norm1_g[l]), mod_l[:, 0], mod_l[:, 1])
        u_ctx = modulate(rms_norm(h_ctx, norm1_g[l]), mod_x[0], mod_x[1])
        y_lat, y_ctx = mixer_sublayer(
            u_lat, u_ctx, rope, lambda_init, not last, w_in[l], b_gate[l], q_norm_a[l], k_norm_a[l],
            diff_lambda[l], sub_norm_a[l], q_norm_c[l], k_norm_c[l], conv_w[l], conv_b[l],
            lru_w_a[l], lru_b_a[l], lru_w_x[l], lru_b_x[l], lru_lambda[l],
            w_branch_a[l], w_branch_b[l], w_branch_c[l], w_out[l])
        h_lat = h_lat + mod_l[:, 2] * y_lat
        v_lat = modulate(rms_norm(h_lat, norm2_g[l]), mod_l[:, 3], mod_l[:, 4])
        if last:
            out = hierarchical_moe(v_lat.reshape(-1, d), w_group[l], b_group[l], w_route[l], b_route[l],
                                   w1[l], w3[l], w2[l])
            h_lat = h_lat + mod_l[:, 5] * out.reshape(b, t, d)
        else:
            h_ctx = h_ctx + mod_x[2] * y_ctx
            v_ctx = modulate(rms_norm(h_ctx, norm2_g[l]), mod_x[3], mod_x[4])
            n_lat = b * t
            tokens = jnp.concatenate([v_lat.reshape(-1, d), v_ctx.reshape(-1, d)], axis=0)
            out = hierarchical_moe(tokens, w_group[l], b_group[l], w_route[l], b_route[l],
                                   w1[l], w3[l], w2[l])
            h_lat = h_lat + mod_l[:, 5] * out[:n_lat].reshape(b, t, d)
            h_ctx = h_ctx + mod_x[5] * out[n_lat:].reshape(h_ctx.shape)
    return h_lat
```

```python
import functools
import math

import jax
import jax.numpy as jnp
from jax import lax
from jax.experimental import pallas as pl
from jax.experimental.pallas import tpu as pltpu

F32 = jnp.float32
BF16 = jnp.bfloat16
I32 = jnp.int32

EPS = 1e-6
HEAD_DIM = 128
GRID_W = 64
ROPE_THETA = 10000.0
DIFF_HEADS = 4
DIFF_V_DIM = 2 * HEAD_DIM
LRU_WIDTH = 1024
LRU_BLOCK_W = 128
CONV_W = 4
LRU_C = 8.0
GQA_HEADS = 8
GQA_KV_HEADS = 2
N_BRANCH = 3
BRANCH_W = 1024
N_GROUPS = 4
EXPERTS_PER_GROUP = 8
N_EXPERTS = N_GROUPS * EXPERTS_PER_GROUP
D_EXPERT = 512
N_MOD = 6
MOD_ROWS = 16

COL_AQ, COL_AK, COL_CQ, COL_AV, COL_BX, COL_BY, COL_CK, COL_CV = 0, 1024, 2048, 3072, 4096, 5120, 6144, 6400
PROJ_COLS = 6656
PROJ_TN = 256
NORM_TILES_HEAD = COL_AV // PROJ_TN
NORM_TILE_CK = COL_CK // PROJ_TN
LANES = 128
ROUTE_LANES = 128

VMEM_LIMIT = 56 * 1024 * 1024


def _cparams(*sem):
    return pltpu.CompilerParams(dimension_semantics=sem, vmem_limit_bytes=VMEM_LIMIT)


def _dot(a, b):
    return jnp.dot(a, b, preferred_element_type=F32)


def _dot_nt(a, b):
    return lax.dot_general(a, b, (((1,), (1,)), ((), ())), preferred_element_type=F32)


def _ada_kernel(c_ref, w_ref, b_ref, o_ref):
    c = c_ref[...]
    s = (c * jax.nn.sigmoid(c)).astype(BF16)
    o_ref[0] = _dot(s, w_ref[0].astype(BF16)) + b_ref[0]


def _ada_modulation(cc, w_ada, b_ada):
    depth, d, n = w_ada.shape
    tn = 1024
    return pl.pallas_call(
        _ada_kernel,
        grid=(depth, n // tn),
        in_specs=[
            pl.BlockSpec((MOD_ROWS, d), lambda l, j: (0, 0)),
            pl.BlockSpec((1, d, tn), lambda l, j: (l, 0, j)),
            pl.BlockSpec((1, 1, tn), lambda l, j: (l, 0, j)),
        ],
        out_specs=pl.BlockSpec((1, MOD_ROWS, tn), lambda l, j: (l, 0, j)),
        out_shape=jax.ShapeDtypeStruct((depth, MOD_ROWS, n), F32),
        compiler_params=_cparams("parallel", "parallel"),
        name="ada_modulation",
    )(cc, w_ada, b_ada.reshape(depth, 1, n))


def _norm_modulate(x, g, mod_ref, shift_row, scale_row):
    ms = jnp.mean(x * x, axis=-1, keepdims=True)
    y = x * lax.rsqrt(ms + EPS) * g
    return y * (1.0 + mod_ref[0, scale_row:scale_row + 1, :]) + mod_ref[0, shift_row:shift_row + 1, :]


def _inproj_kernel(h_ref, mod_ref, g_ref, w_ref, gc_ref, cos_ref, sa_ref, sb_ref, qkv_ref, u_ref):
    j = pl.program_id(1)

    @pl.when(j == 0)
    def _():
        u_ref[...] = _norm_modulate(h_ref[...], g_ref[...], mod_ref, 0, 1).astype(BF16)

    acc = _dot(u_ref[...], w_ref[...])
    is_norm = jnp.logical_or(j < NORM_TILES_HEAD, j == NORM_TILE_CK)

    @pl.when(is_norm)
    def _():
        cos = cos_ref[...]
        sa = sa_ref[...]
        sb = sb_ref[...]
        for c in range(PROJ_TN // HEAD_DIM):
            sl = slice(c * HEAD_DIM, (c + 1) * HEAD_DIM)
            x = acc[:, sl]
            ms = jnp.mean(x * x, axis=-1, keepdims=True)
            y = x * lax.rsqrt(ms + EPS) * gc_ref[:, sl]
            y = y * cos + pltpu.roll(y, 96, 1) * sa + pltpu.roll(y, 32, 1) * sb
            qkv_ref[:, sl] = y.astype(BF16)

    @pl.when(jnp.logical_not(is_norm))
    def _():
        qkv_ref[...] = acc.astype(BF16)


def _in_projection(h_all, mod, g1, w_proj, gcol, rope, *, tm, n_lat_tiles, tiles_per_seq, batch):
    n_all, d = h_all.shape
    cos, sa, sb = rope

    def mod_idx(i, j):
        return (jnp.where(i < n_lat_tiles, i // tiles_per_seq, batch), 0, 0)

    def rope_idx(i, j):
        return (jnp.where(i < n_lat_tiles, i % tiles_per_seq, tiles_per_seq), 0)

    return pl.pallas_call(
        _inproj_kernel,
        grid=(n_all // tm, PROJ_COLS // PROJ_TN),
        in_specs=[
            pl.BlockSpec((tm, d), lambda i, j: (i, 0)),
            pl.BlockSpec((1, N_MOD, d), mod_idx),
            pl.BlockSpec((1, d), lambda i, j: (0, 0)),
            pl.BlockSpec((d, PROJ_TN), lambda i, j: (0, j)),
            pl.BlockSpec((1, PROJ_TN), lambda i, j: (0, j)),
            pl.BlockSpec((tm, HEAD_DIM), rope_idx),
            pl.BlockSpec((tm, HEAD_DIM), rope_idx),
            pl.BlockSpec((tm, HEAD_DIM), rope_idx),
        ],
        out_specs=[
            pl.BlockSpec((tm, PROJ_TN), lambda i, j: (i, j)),
            pl.BlockSpec((tm, d), lambda i, j: (i, 0)),
        ],
        out_shape=[
            jax.ShapeDtypeStruct((n_all, PROJ_COLS), BF16),
            jax.ShapeDtypeStruct((n_all, d), BF16),
        ],
        compiler_params=_cparams("parallel", "arbitrary"),
        name="in_projection",
    )(h_all, mod, g1, w_proj, gcol, cos, sa, sb)


def _softmax_pieces(q, k_refs, col):
    s = [_dot_nt(q, k_ref[:, col:col + HEAD_DIM]) for k_ref in k_refs]
    m = functools.reduce(jnp.maximum, [jnp.max(x, axis=-1, keepdims=True) for x in s])
    e = [jnp.exp(x - m) for x in s]
    l = functools.reduce(jnp.add, [jnp.sum(x, axis=-1, keepdims=True) for x in e])
    return e, 1.0 / l


def _diff_attn_kernel(dl_ref, sg_ref, q_ref, *refs, lambda_init, n_pieces):
    k_refs = refs[0:2 * n_pieces:2]
    v_refs = refs[1:2 * n_pieces:2]
    o_ref = refs[2 * n_pieces]
    dl = dl_ref[...]
    lam = (jnp.exp(jnp.sum(dl[0:1] * dl[1:2], axis=-1, keepdims=True))
           - jnp.exp(jnp.sum(dl[2:3] * dl[3:4], axis=-1, keepdims=True)) + lambda_init)
    for h in range(DIFF_HEADS):
        c1 = 2 * h * HEAD_DIM
        c2 = c1 + HEAD_DIM
        e1, r1 = _softmax_pieces(q_ref[:, c1:c1 + HEAD_DIM], k_refs, c1)
        e2, r2 = _softmax_pieces(q_ref[:, c2:c2 + HEAD_DIM], k_refs, c2)
        r2 = r2 * lam
        vs = slice(h * DIFF_V_DIM, (h + 1) * DIFF_V_DIM)
        o = None
        for x1, x2, v_ref in zip(e1, e2, v_refs):
            w = (x1 * r1 - x2 * r2).astype(BF16)
            part = _dot(w, v_ref[:, vs])
            o = part if o is None else o + part
        ms = jnp.mean(o * o, axis=-1, keepdims=True)
        o = o * lax.rsqrt(ms + EPS) * sg_ref[...] * (1.0 - lambda_init)
        o_ref[:, vs] = o.astype(BF16)


def _gqa_kernel(q_ref, *refs, n_pieces):
    k_refs = refs[0:2 * n_pieces:2]
    v_refs = refs[1:2 * n_pieces:2]
    o_ref = refs[2 * n_pieces]
    group = GQA_HEADS // GQA_KV_HEADS
    for head in range(GQA_HEADS):
        kc = (head // group) * HEAD_DIM
        qs = slice(head * HEAD_DIM, (head + 1) * HEAD_DIM)
        e, r = _softmax_pieces(q_ref[:, qs], k_refs, kc)
        o = None
        for x, v_ref in zip(e, v_refs):
            part = _dot(x.astype(BF16), v_ref[:, kc:kc + HEAD_DIM])
            o = part if o is None else o + part
        o_ref[:, qs] = (o * r).astype(BF16)


def _attention(kind, qkv, extra, *, batch, seq, ctx, tq, latent, lambda_init=None):
    n_lat = batch * seq
    ctx_blk0 = n_lat // ctx
    if kind == "diff":
        qcol, kcol, vcol, kvw = COL_AQ // 1024, COL_AK // 1024, COL_AV // 1024, 1024
    else:
        qcol, kcol, vcol, kvw = COL_CQ // 1024, COL_CK // 256, COL_CV // 256, 256
    if latent:
        q_tiles = seq // tq
        q_spec = pl.BlockSpec((tq, 1024), lambda b, i: (b * q_tiles + i, qcol))
        o_spec = pl.BlockSpec((tq, 1024), lambda b, i: (b * q_tiles + i, 0))
        n_q = n_lat
    else:
        q_tiles = 1
        tq = ctx
        q_spec = pl.BlockSpec((ctx, 1024), lambda b, i: (ctx_blk0 + b, qcol))
        o_spec = pl.BlockSpec((ctx, 1024), lambda b, i: (b, 0))
        n_q = batch * ctx
    kv_specs = [
        pl.BlockSpec((ctx, kvw), lambda b, i: (ctx_blk0 + b, kcol)),
        pl.BlockSpec((ctx, kvw), lambda b, i: (ctx_blk0 + b, vcol)),
    ]
    kv_args = [qkv, qkv]
    if latent:
        kv_specs += [
            pl.BlockSpec((seq, kvw), lambda b, i: (b, kcol)),
            pl.BlockSpec((seq, kvw), lambda b, i: (b, vcol)),
        ]
        kv_args += [qkv, qkv]
    n_pieces = len(kv_args) // 2
    if kind == "diff":
        diff_lambda, sub_g = extra
        body = functools.partial(_diff_attn_kernel, lambda_init=lambda_init, n_pieces=n_pieces)
        pre_specs = [
            pl.BlockSpec((4, HEAD_DIM), lambda b, i: (0, 0)),
            pl.BlockSpec((1, DIFF_V_DIM), lambda b, i: (0, 0)),
        ]
        pre_args = [diff_lambda, sub_g]
    else:
        body = functools.partial(_gqa_kernel, n_pieces=n_pieces)
        pre_specs, pre_args = [], []
    return pl.pallas_call(
        body,
        grid=(batch, q_tiles),
        in_specs=pre_specs + [q_spec] + kv_specs,
        out_specs=o_spec,
        out_shape=jax.ShapeDtypeStruct((n_q, 1024), BF16),
        compiler_params=_cparams("parallel", "arbitrary"),
        name=f"{kind}_attn_{'lat' if latent else 'ctx'}",
    )(*pre_args, qkv, *kv_args)


LRU_CW = 512
LRU_PAD = 8


def _lru_kernel(bxc_ref, byc_ref, bxl_ref, byl_ref, cw_ref, cb_ref, wa_ref, ba_ref, wx_ref, bx_ref, lam_ref,
                *refs, ctx, seq, need_ctx):
    if need_ctx:
        yc_ref, yl_ref = refs[0], refs[1]
        scratch = refs[2:]
    else:
        yc_ref, yl_ref = None, refs[0]
        scratch = refs[1:]
    xpad, a_f, b_f, a_b, b_b = scratch
    a_scr = (a_f, a_b)
    b_scr = (b_f, b_b)
    lam = lam_ref[...]
    neg_sp = -LRU_C * jax.nn.softplus(-lam)

    def gates(x_ref, n):
        zeros = jnp.zeros((LRU_PAD, LRU_CW), F32)
        xpad[0:LRU_PAD, :] = zeros
        xpad[LRU_PAD:LRU_PAD + n, :] = x_ref[...].astype(F32)
        xpad[LRU_PAD + n:2 * LRU_PAD + n, :] = zeros
        xc = cb_ref[...] + cw_ref[0:1, :] * xpad[LRU_PAD - 2:LRU_PAD - 2 + n, :]
        for k in range(1, CONV_W):
            xc = xc + cw_ref[k:k + 1, :] * xpad[LRU_PAD - 2 + k:LRU_PAD - 2 + k + n, :]
        xcb = xc.astype(BF16)
        for d in range(2):
            for blk in range(LRU_CW // LRU_BLOCK_W):
                sl = slice(blk * LRU_BLOCK_W, (blk + 1) * LRU_BLOCK_W)
                xs = xcb[:, sl]
                r = jax.nn.sigmoid(_dot(xs, wa_ref[d, blk].astype(BF16)) + ba_ref[d:d + 1, sl])
                gi = jax.nn.sigmoid(_dot(xs, wx_ref[d, blk].astype(BF16)) + bx_ref[d:d + 1, sl])
                log_a = r * neg_sp[d:d + 1, sl]
                a = jnp.exp(log_a)
                mult = jnp.sqrt(1.0 - a * a)
                a_scr[d][0:n, sl] = a
                b_scr[d][0:n, sl] = mult * gi * xc[:, sl]

    def scan(n, hf0, hb0):
        def body(t, carry):
            hf, hb = carry
            tb = n - 1 - t
            hf = a_f[pl.ds(t, 1), :] * hf + b_f[pl.ds(t, 1), :]
            b_f[pl.ds(t, 1), :] = hf
            hb = a_b[pl.ds(tb, 1), :] * hb + b_b[pl.ds(tb, 1), :]
            b_b[pl.ds(tb, 1), :] = hb
            return hf, hb
        return lax.fori_loop(0, n, body, (hf0, hb0), unroll=8)

    zero = jnp.zeros((1, LRU_CW), F32)
    gates(bxc_ref, ctx)
    hf, hb = scan(ctx, zero, zero)
    if need_ctx:
        yc_ref[...] = ((b_f[0:ctx, :] + b_b[0:ctx, :])
                       * jax.nn.gelu(byc_ref[...].astype(F32), approximate=True)).astype(BF16)
    gates(bxl_ref, seq)
    scan(seq, hf, hb)
    yl_ref[...] = ((b_f[0:seq, :] + b_b[0:seq, :])
                   * jax.nn.gelu(byl_ref[...].astype(F32), approximate=True)).astype(BF16)


def _rglru(qkv, conv_w, conv_b, w_a, b_a, w_x, b_x, lam, *, batch, seq, ctx, need_ctx):
    n_lat = batch * seq
    ctx_blk0 = n_lat // ctx
    cbx, cby = COL_BX // LRU_CW, COL_BY // LRU_CW
    nblk = LRU_CW // LRU_BLOCK_W
    tmax = max(seq, ctx)
    out_specs = [pl.BlockSpec((seq, LRU_CW), lambda b, c: (b, c))]
    out_shape = [jax.ShapeDtypeStruct((n_lat, LRU_WIDTH), BF16)]
    if need_ctx:
        out_specs = [pl.BlockSpec((ctx, LRU_CW), lambda b, c: (b, c))] + out_specs
        out_shape = [jax.ShapeDtypeStruct((batch * ctx, LRU_WIDTH), BF16)] + out_shape
    outs = pl.pallas_call(
        functools.partial(_lru_kernel, ctx=ctx, seq=seq, need_ctx=need_ctx),
        grid=(batch, LRU_WIDTH // LRU_CW),
        in_specs=[
            pl.BlockSpec((ctx, LRU_CW), lambda b, c: (ctx_blk0 + b, cbx + c)),
            pl.BlockSpec((ctx, LRU_CW), lambda b, c: (ctx_blk0 + b, cby + c)),
            pl.BlockSpec((seq, LRU_CW), lambda b, c: (b, cbx + c)),
            pl.BlockSpec((seq, LRU_CW), lambda b, c: (b, cby + c)),
            pl.BlockSpec((CONV_W, LRU_CW), lambda b, c: (0, c)),
            pl.BlockSpec((1, LRU_CW), lambda b, c: (0, c)),
            pl.BlockSpec((2, nblk, LRU_BLOCK_W, LRU_BLOCK_W), lambda b, c: (0, c, 0, 0)),
            pl.BlockSpec((2, LRU_CW), lambda b, c: (0, c)),
            pl.BlockSpec((2, nblk, LRU_BLOCK_W, LRU_BLOCK_W), lambda b, c: (0, c, 0, 0)),
            pl.BlockSpec((2, LRU_CW), lambda b, c: (0, c)),
            pl.BlockSpec((2, LRU_CW), lambda b, c: (0, c)),
        ],
        out_specs=out_specs,
        out_shape=out_shape,
        scratch_shapes=[
            pltpu.VMEM((tmax + 2 * LRU_PAD, LRU_CW), F32),
            pltpu.VMEM((tmax, LRU_CW), F32),
            pltpu.VMEM((tmax, LRU_CW), F32),
            pltpu.VMEM((tmax, LRU_CW), F32),
            pltpu.VMEM((tmax, LRU_CW), F32),
        ],
        compiler_params=_cparams("parallel", "arbitrary"),
        name="rglru",
    )(qkv, qkv, qkv, qkv, conv_w, conv_b, w_a, b_a, w_x, b_x, lam)
    if need_ctx:
        return outs[1], outs[0]
    return outs[0], None


MERGE_TN = 256


def _merge_kernel(u_ref, ya_ref, yb_ref, yc_ref, wg0_ref, wg1_ref, wg2_ref, bg_ref,
                  wa_ref, wb_ref, wc_ref, m_ref):
    u = u_ref[...]
    m = None
    for k, (y_ref, wg_ref, wbr_ref) in enumerate(
            ((ya_ref, wg0_ref, wa_ref), (yb_ref, wg1_ref, wb_ref), (yc_ref, wg2_ref, wc_ref))):
        g = jax.nn.sigmoid(_dot(u, wg_ref[...]) + bg_ref[k:k + 1, :])
        part = g * _dot(y_ref[...], wbr_ref[...])
        m = part if m is None else m + part
    m_ref[...] = m.astype(BF16)


def _merge(u, ya, yb, yc, w_gate, b_gate, wba, wbb, wbc, *, n_tok, tm):
    d = u.shape[1]
    nj = d // MERGE_TN
    y_spec = pl.BlockSpec((tm, BRANCH_W), lambda i, j: (i, 0))
    wbr_spec = pl.BlockSpec((BRANCH_W, MERGE_TN), lambda i, j: (0, j))
    return pl.pallas_call(
        _merge_kernel,
        grid=(n_tok // tm, nj),
        in_specs=[
            pl.BlockSpec((tm, d), lambda i, j: (i, 0)),
            y_spec, y_spec, y_spec,
            pl.BlockSpec((d, MERGE_TN), lambda i, j: (0, j)),
            pl.BlockSpec((d, MERGE_TN), lambda i, j: (0, nj + j)),
            pl.BlockSpec((d, MERGE_TN), lambda i, j: (0, 2 * nj + j)),
            pl.BlockSpec((N_BRANCH, MERGE_TN), lambda i, j: (0, j)),
            wbr_spec, wbr_spec, wbr_spec,
        ],
        out_specs=pl.BlockSpec((tm, MERGE_TN), lambda i, j: (i, j)),
        out_shape=jax.ShapeDtypeStruct((n_tok, d), BF16),
        compiler_params=_cparams("parallel", "arbitrary"),
        name="branch_merge",
    )(u, ya, yb, yc, w_gate, w_gate, w_gate, b_gate, wba, wbb, wbc)


def _outproj_kernel(m_ref, h_ref, mod_ref, g_ref, wo_ref, wr_ref, br_ref, h1_ref, v_ref, route_ref):
    y = _dot(m_ref[...], wo_ref[...])
    h1 = h_ref[...] + mod_ref[0, 2:3, :] * y
    h1_ref[...] = h1
    v = _norm_modulate(h1, g_ref[...], mod_ref, 3, 4)
    v_ref[...] = v
    logits = _dot(v.astype(BF16), wr_ref[...]) + br_ref[...]
    lane = lax.broadcasted_iota(I32, logits.shape, 1).astype(F32)
    neg = jnp.float32(-jnp.inf)
    big = jnp.float32(1e9)

    def masked_argmax(mask):
        val = jnp.max(jnp.where(mask, logits, neg), axis=-1, keepdims=True)
        idx = jnp.min(jnp.where(jnp.logical_and(mask, logits == val), lane, big), axis=-1, keepdims=True)
        return val, idx

    gmask = lane < N_GROUPS
    gmax, gidx = masked_argmax(gmask)
    gsum = jnp.sum(jnp.where(gmask, jnp.exp(logits - gmax), 0.0), axis=-1, keepdims=True)
    g_w = 1.0 / gsum
    lo = N_GROUPS + EXPERTS_PER_GROUP * gidx
    emask = jnp.logical_and(lane >= lo, lane < lo + EXPERTS_PER_GROUP)
    v1, i1 = masked_argmax(emask)
    v2, i2 = masked_argmax(jnp.logical_and(emask, lane != i1))
    e21 = jnp.exp(v2 - v1)
    p1 = g_w / (1.0 + e21)
    p2 = g_w * e21 / (1.0 + e21)
    route = jnp.where(lane == 0, i1 - N_GROUPS,
                      jnp.where(lane == 1, i2 - N_GROUPS,
                                jnp.where(lane == 2, p1, jnp.where(lane == 3, p2, 0.0))))
    route_ref[...] = route


def _out_projection(m, h_all, mod, g2, w_out, w_router, b_router, *, n_tok, tm, n_lat, seq, batch):
    d = m.shape[1]

    def mod_idx(i):
        return (jnp.where(i * tm < n_lat, (i * tm) // seq, batch), 0, 0)

    return pl.pallas_call(
        _outproj_kernel,
        grid=(n_tok // tm,),
        in_specs=[
            pl.BlockSpec((tm, d), lambda i: (i, 0)),
            pl.BlockSpec((tm, d), lambda i: (i, 0)),
            pl.BlockSpec((1, N_MOD, d), mod_idx),
            pl.BlockSpec((1, d), lambda i: (0, 0)),
            pl.BlockSpec((d, d), lambda i: (0, 0)),
            pl.BlockSpec((d, ROUTE_LANES), lambda i: (0, 0)),
            pl.BlockSpec((1, ROUTE_LANES), lambda i: (0, 0)),
        ],
        out_specs=[
            pl.BlockSpec((tm, d), lambda i: (i, 0)),
            pl.BlockSpec((tm, d), lambda i: (i, 0)),
            pl.BlockSpec((tm, ROUTE_LANES), lambda i: (i, 0)),
        ],
        out_shape=[
            jax.ShapeDtypeStruct((n_tok, d), F32),
            jax.ShapeDtypeStruct((n_tok, d), F32),
            jax.ShapeDtypeStruct((n_tok, ROUTE_LANES), F32),
        ],
        compiler_params=_cparams("parallel"),
        name="out_projection_router",
    )(m, h_all, mod, g2, w_out, w_router, b_router)


MOE_TM = 256


def _moe_kernel(te_ref, tf_ref, tv_ref, rt_ref, rw_ref, v_hbm, w1_ref, w3_ref, w2_ref, y_ref,
                xbuf, w1b, w3b, w2b, sem):
    i = pl.program_id(0)
    valid = tv_ref[i] == 1

    @pl.when(valid)
    def _():
        def issue(r, carry):
            tok = rt_ref[0, 0, r]
            pltpu.make_async_copy(v_hbm.at[pl.ds(tok, 1), :], xbuf.at[pl.ds(r, 1), :], sem).start()
            return carry
        lax.fori_loop(0, MOE_TM, issue, 0, unroll=8)

        @pl.when(tf_ref[i] == 1)
        def _():
            w1b[...] = w1_ref[0, 0].astype(BF16)
            w3b[...] = w3_ref[0, 0].astype(BF16)
            w2b[...] = w2_ref[0, 0].astype(BF16)

        def drain(r, carry):
            pltpu.make_async_copy(v_hbm.at[pl.ds(0, 1), :], xbuf.at[pl.ds(r, 1), :], sem).wait()
            return carry
        lax.fori_loop(0, MOE_TM, drain, 0, unroll=8)
        x = xbuf[...].astype(BF16)
        h1 = _dot(x, w1b[...])
        h3 = _dot(x, w3b[...])
        hh = (h1 * jax.nn.sigmoid(h1) * h3).astype(BF16)
        y_ref[...] = _dot(hh, w2b[...]) * rw_ref[:, 0:1]

    @pl.when(jnp.logical_not(valid))
    def _():
        y_ref[...] = jnp.zeros_like(y_ref)


def _moe_experts(layer, v, tile_expert, tile_first, tile_valid, row_token, row_w, w1, w3, w2):
    n_tiles = tile_expert.shape[0]
    d = v.shape[1]
    grid_spec = pltpu.PrefetchScalarGridSpec(
        num_scalar_prefetch=3,
        grid=(n_tiles,),
        in_specs=[
            pl.BlockSpec((1, 1, MOE_TM), lambda i, te, tf, tv: (i, 0, 0), memory_space=pltpu.SMEM),
            pl.BlockSpec((MOE_TM, LANES), lambda i, te, tf, tv: (i, 0)),
            pl.BlockSpec(memory_space=pl.ANY),
            pl.BlockSpec((1, 1, d, D_EXPERT), lambda i, te, tf, tv: (layer, te[i], 0, 0)),
            pl.BlockSpec((1, 1, d, D_EXPERT), lambda i, te, tf, tv: (layer, te[i], 0, 0)),
            pl.BlockSpec((1, 1, D_EXPERT, d), lambda i, te, tf, tv: (layer, te[i], 0, 0)),
        ],
        out_specs=pl.BlockSpec((MOE_TM, d), lambda i, te, tf, tv: (i, 0)),
        scratch_shapes=[
            pltpu.VMEM((MOE_TM, d), F32),
            pltpu.VMEM((d, D_EXPERT), BF16),
            pltpu.VMEM((d, D_EXPERT), BF16),
            pltpu.VMEM((D_EXPERT, d), BF16),
            pltpu.SemaphoreType.DMA(()),
        ],
    )
    return pl.pallas_call(
        _moe_kernel,
        grid_spec=grid_spec,
        out_shape=jax.ShapeDtypeStruct((n_tiles * MOE_TM, d), F32),
        compiler_params=_cparams("arbitrary"),
        name="moe_experts",
    )(tile_expert, tile_first, tile_valid, row_token.reshape(n_tiles, 1, MOE_TM), row_w, v, w1, w3, w2)


def _dispatch(route, n_tok):
    e_flat = route[:, 0:2].astype(I32).reshape(-1)
    p_flat = route[:, 2:4].reshape(-1)
    n_asg = 2 * n_tok
    n_tiles = n_asg // MOE_TM + N_EXPERTS
    onehot = (e_flat[:, None] == jnp.arange(N_EXPERTS, dtype=I32)[None, :]).astype(I32)
    csum = jnp.cumsum(onehot, axis=0)
    rank = jnp.sum(csum * onehot, axis=1) - 1
    counts = csum[-1]
    padded = ((counts + MOE_TM - 1) // MOE_TM) * MOE_TM
    seg_end = jnp.cumsum(padded)
    seg_start = seg_end - padded
    dest = seg_start[e_flat] + rank
    n_rows = n_tiles * MOE_TM
    row_token = jnp.zeros((n_rows,), I32).at[dest].set(jnp.arange(n_asg, dtype=I32) // 2)
    row_w = jnp.zeros((n_rows,), F32).at[dest].set(p_flat)
    row_w = jnp.broadcast_to(row_w[:, None], (n_rows, LANES))
    tile_start = jnp.arange(n_tiles, dtype=I32) * MOE_TM
    tile_valid = (tile_start < seg_end[-1]).astype(I32)
    tile_expert = jnp.searchsorted(seg_end, tile_start, side="right").astype(I32)
    last_expert = jnp.max(jnp.where(tile_valid == 1, tile_expert, 0))
    tile_expert = jnp.where(tile_valid == 1, tile_expert, last_expert)
    prev = jnp.concatenate([jnp.full((1,), -1, I32), tile_expert[:-1]])
    tile_first = (tile_expert != prev).astype(I32)
    return tile_expert, tile_first, tile_valid, row_token, row_w, dest.reshape(n_tok, 2)


COMB_TM = 256


def _combine_kernel(p0_ref, p1_ref, h_ref, mod_ref, y_hbm, o_ref, ybuf, sem):
    def issue(r, carry):
        pltpu.make_async_copy(y_hbm.at[pl.ds(p0_ref[0, 0, r], 1), :], ybuf.at[0, pl.ds(r, 1), :], sem).start()
        pltpu.make_async_copy(y_hbm.at[pl.ds(p1_ref[0, 0, r], 1), :], ybuf.at[1, pl.ds(r, 1), :], sem).start()
        return carry
    lax.fori_loop(0, COMB_TM, issue, 0, unroll=8)

    def drain(r, carry):
        pltpu.make_async_copy(y_hbm.at[pl.ds(0, 1), :], ybuf.at[0, pl.ds(r, 1), :], sem).wait()
        pltpu.make_async_copy(y_hbm.at[pl.ds(0, 1), :], ybuf.at[1, pl.ds(r, 1), :], sem).wait()
        return carry
    lax.fori_loop(0, COMB_TM, drain, 0, unroll=8)
    o_ref[...] = h_ref[...] + mod_ref[0, 5:6, :] * (ybuf[0] + ybuf[1])


def _combine(h1, mod, y_sorted, pos, *, n_tok, n_lat, seq, batch):
    d = h1.shape[1]
    tm = COMB_TM
    n_tiles = n_tok // tm
    pos0 = pos[:, 0].reshape(n_tiles, 1, tm)
    pos1 = pos[:, 1].reshape(n_tiles, 1, tm)

    def mod_idx(i):
        return (jnp.where(i * tm < n_lat, (i * tm) // seq, batch), 0, 0)

    return pl.pallas_call(
        _combine_kernel,
        grid=(n_tiles,),
        in_specs=[
            pl.BlockSpec((1, 1, tm), lambda i: (i, 0, 0), memory_space=pltpu.SMEM),
            pl.BlockSpec((1, 1, tm), lambda i: (i, 0, 0), memory_space=pltpu.SMEM),
            pl.BlockSpec((tm, d), lambda i: (i, 0)),
            pl.BlockSpec((1, N_MOD, d), mod_idx),
            pl.BlockSpec(memory_space=pl.ANY),
        ],
        out_specs=pl.BlockSpec((tm, d), lambda i: (i, 0)),
        out_shape=jax.ShapeDtypeStruct((n_tok, d), F32),
        scratch_shapes=[pltpu.VMEM((2, tm, d), F32), pltpu.SemaphoreType.DMA(())],
        compiler_params=_cparams("arbitrary"),
        name="moe_combine",
    )(pos0, pos1, h1, mod, y_sorted)


def _rope_tables(seq, tm):
    n_rows = seq // GRID_W
    row = jnp.repeat(jnp.arange(n_rows), GRID_W).astype(F32)
    col = jnp.tile(jnp.arange(GRID_W), n_rows).astype(F32)
    half = HEAD_DIM // 2
    inv = 1.0 / (ROPE_THETA ** (jnp.arange(0, half, 2, dtype=F32) / half))
    ang_r = row[:, None] * inv
    ang_c = col[:, None] * inv
    ang = jnp.concatenate([ang_r, ang_r, ang_c, ang_c], axis=-1)
    cos, sin = jnp.cos(ang), jnp.sin(ang)
    lane = jnp.arange(HEAD_DIM)
    first = (lane % half) < (half // 2)
    sin_a = jnp.where(first, -sin, 0.0)
    sin_b = jnp.where(first, 0.0, sin)
    ident = jnp.zeros((tm, HEAD_DIM), F32)
    return (jnp.concatenate([cos, ident + 1.0]), jnp.concatenate([sin_a, ident]),
            jnp.concatenate([sin_b, ident]))


def kernel(x, c, ctx, c_ctx, w_ada, b_ada, norm1_g, norm2_g, w_in, b_gate, q_norm_a, k_norm_a, diff_lambda,
           sub_norm_a, q_norm_c, k_norm_c, conv_w, conv_b, lru_w_a, lru_b_a, lru_w_x, lru_b_x, lru_lambda,
           w_branch_a, w_branch_b, w_branch_c, w_out, w_group, b_group, w_route, b_route, w1, w3, w2):
    batch, seq, d = x.shape
    ctx_len = ctx.shape[1]
    depth = w_ada.shape[0]
    n_lat = batch * seq
    n_ctx = batch * ctx_len
    n_all = n_lat + n_ctx
    tm = min(1024, seq, n_ctx)
    assert seq % tm == 0 and n_ctx % tm == 0 and n_lat % ctx_len == 0 and batch < MOD_ROWS
    assert seq % GRID_W == 0 and seq % COMB_TM == 0 and n_ctx % COMB_TM == 0
    tq = min(256, seq)
    tm_out = min(512, tm)

    cc = jnp.zeros((MOD_ROWS, d), F32).at[:batch].set(c).at[batch].set(c_ctx)
    mod_all = _ada_modulation(cc, w_ada, b_ada).reshape(depth, MOD_ROWS, N_MOD, d)
    rope = _rope_tables(seq, tm)
    h_all = jnp.concatenate([x.reshape(n_lat, d), ctx.reshape(n_ctx, d)], axis=0)
    scale = HEAD_DIM ** -0.5
    ones = jnp.ones((PROJ_COLS - COL_CK - 256,), F32)

    for l in range(depth):
        last = l == depth - 1
        lambda_init = 0.8 - 0.6 * math.exp(-0.3 * l)
        mod = mod_all[l]
        wl = w_in[l]
        w_proj = jnp.concatenate([wl[:, 0:2048], wl[:, 5120:6144], wl[:, 2048:5120], wl[:, 6144:6656]],
                                 axis=1).astype(BF16)
        w_gate = wl[:, 6656:].astype(BF16)
        gcol = jnp.concatenate([
            jnp.tile(q_norm_a[l] * scale, 2 * DIFF_HEADS), jnp.tile(k_norm_a[l], 2 * DIFF_HEADS),
            jnp.tile(q_norm_c[l] * scale, GQA_HEADS), jnp.ones((COL_CK - COL_AV,), F32),
            jnp.tile(k_norm_c[l], GQA_KV_HEADS), ones]).reshape(1, PROJ_COLS)

        qkv, u = _in_projection(h_all, mod, norm1_g[l].reshape(1, d), w_proj, gcol, rope, tm=tm,
                                n_lat_tiles=n_lat // tm, tiles_per_seq=seq // tm, batch=batch)
        dims = dict(batch=batch, seq=seq, ctx=ctx_len)
        extra_a = (diff_lambda[l], sub_norm_a[l].reshape(1, DIFF_V_DIM))
        ya = _attention("diff", qkv, extra_a, tq=tq, latent=True, lambda_init=lambda_init, **dims)
        yc = _attention("gqa", qkv, None, tq=tq, latent=True, **dims)
        yb, yb_c = _rglru(qkv, conv_w[l], conv_b[l].reshape(1, LRU_WIDTH), lru_w_a[l], lru_b_a[l],
                          lru_w_x[l], lru_b_x[l], lru_lambda[l], need_ctx=not last, **dims)
        if last:
            n_tok = n_lat
        else:
            n_tok = n_all
            ya_c = _attention("diff", qkv, extra_a, tq=tq, latent=False, lambda_init=lambda_init, **dims)
            yc_c = _attention("gqa", qkv, None, tq=tq, latent=False, **dims)
            ya = jnp.concatenate([ya, ya_c], axis=0)
            yb = jnp.concatenate([yb, yb_c], axis=0)
            yc = jnp.concatenate([yc, yc_c], axis=0)
        m = _merge(u, ya, yb, yc, w_gate, b_gate[l], w_branch_a[l].astype(BF16), w_branch_b[l].astype(BF16),
                   w_branch_c[l].astype(BF16), n_tok=n_tok, tm=tm)
        w_router = jnp.zeros((d, ROUTE_LANES), F32).at[:, :N_GROUPS].set(w_group[l])
        w_router = w_router.at[:, N_GROUPS:N_GROUPS + N_EXPERTS].set(w_route[l]).astype(BF16)
        b_router = jnp.zeros((1, ROUTE_LANES), F32).at[0, :N_GROUPS].set(b_group[l])
        b_router = b_router.at[0, N_GROUPS:N_GROUPS + N_EXPERTS].set(b_route[l])
        h1, v, route = _out_projection(m, h_all, mod, norm2_g[l].reshape(1, d), w_out[l].astype(BF16),
                                       w_router, b_router, n_tok=n_tok, tm=tm_out, n_lat=n_lat, seq=seq,
                                       batch=batch)
        tile_expert, tile_first, tile_valid, row_token, row_w, pos = _dispatch(route, n_tok)
        y_sorted = _moe_experts(l, v, tile_expert, tile_first, tile_valid, row_token, row_w, w1, w3, w2)
        h_all = _combine(h1, mod, y_sorted, pos, n_tok=n_tok, n_lat=n_lat, seq=seq, batch=batch)
    return h_all.reshape(batch, seq, d)
```

```python
import functools
import math

import jax
import jax.numpy as jnp
from jax import lax
from jax.experimental import pallas as pl
from jax.experimental.pallas import tpu as pltpu

F32 = jnp.float32
BF16 = jnp.bfloat16
I32 = jnp.int32
U32 = jnp.uint32

EPS = 1e-6
HEAD_DIM = 128
GRID_W = 64
ROPE_THETA = 10000.0
DIFF_HEADS = 4
DIFF_V_DIM = 2 * HEAD_DIM
LRU_WIDTH = 1024
LRU_BLOCK_W = 128
CONV_W = 4
LRU_C = 8.0
GQA_HEADS = 8
GQA_KV_HEADS = 2
N_BRANCH = 3
BRANCH_W = 1024
N_GROUPS = 4
EXPERTS_PER_GROUP = 8
N_EXPERTS = N_GROUPS * EXPERTS_PER_GROUP
D_EXPERT = 512
N_MOD = 6
MOD_ROWS = 16
LANES = 128

QK_AQ, QK_AK, QK_CQ, QK_CK, QK_CV = 0, 1024, 2048, 3072, 3328
VB_AV, VB_BX, VB_BY = 0, 1024, 2048
QK_COLS = 3584
VB_COLS = 3072
QK_TN = QK_COLS // 2
VB_TN = VB_COLS // 2
PROJ_CW = 256
PROJ_RC = 512
HEAD_PERM = tuple(range(0, 32)) + tuple(range(64, 96)) + tuple(range(32, 64)) + tuple(range(96, 128))

VMEM_LIMIT = 56 * 1024 * 1024


def _cparams(*sem):
    return pltpu.CompilerParams(dimension_semantics=sem, vmem_limit_bytes=VMEM_LIMIT)


def _dot(a, b):
    return jnp.dot(a, b, preferred_element_type=F32)


def _dot_nt(a, b):
    return lax.dot_general(a, b, (((1,), (1,)), ((), ())), preferred_element_type=F32)


def _lane_mean(x2):
    return _dot(x2.astype(BF16), jnp.full((LANES, LANES), 1.0 / LANES, BF16))


def _ada_kernel(c_ref, w_ref, b_ref, o_ref):
    c = c_ref[...]
    s = (c * jax.nn.sigmoid(c)).astype(BF16)
    o_ref[0] = _dot(s, w_ref[0].astype(BF16)) + b_ref[0]


def _ada_modulation(cc, w_ada, b_ada):
    depth, d, n = w_ada.shape
    tn = 1024
    return pl.pallas_call(
        _ada_kernel,
        grid=(depth, n // tn),
        in_specs=[
            pl.BlockSpec((MOD_ROWS, d), lambda l, j: (0, 0)),
            pl.BlockSpec((1, d, tn), lambda l, j: (l, 0, j)),
            pl.BlockSpec((1, 1, tn), lambda l, j: (l, 0, j)),
        ],
        out_specs=pl.BlockSpec((1, MOD_ROWS, tn), lambda l, j: (l, 0, j)),
        out_shape=jax.ShapeDtypeStruct((depth, MOD_ROWS, n), F32),
        compiler_params=_cparams("parallel", "parallel"),
        name="ada_modulation",
    )(cc, w_ada, b_ada.reshape(depth, 1, n))


NORM_ROWS = 32


def _norm_modulate_rows(x_ref, g_ref, mod_ref, shift_row, scale_row, out_refs):
    g = g_ref[...]
    sc = 1.0 + mod_ref[0, scale_row:scale_row + 1, :]
    sh = mod_ref[0, shift_row:shift_row + 1, :]

    def body(i, carry):
        rows = pl.ds(pl.multiple_of(i * NORM_ROWS, NORM_ROWS), NORM_ROWS)
        x = x_ref[rows, :]
        ms = jnp.mean(x * x, axis=-1, keepdims=True)
        y = (x * lax.rsqrt(ms + EPS) * g) * sc + sh
        for o_ref in out_refs:
            o_ref[rows, :] = y.astype(o_ref.dtype)
        return carry
    lax.fori_loop(0, x_ref.shape[0] // NORM_ROWS, body, 0)


def _norm_kernel(h_ref, mod_ref, g_ref, u_ref):
    _norm_modulate_rows(h_ref, g_ref, mod_ref, 0, 1, (u_ref,))


def _project_tile(u_ref, w_ref, epilogue):
    tm, tn = u_ref.shape[0], w_ref.shape[1]
    rc = min(PROJ_RC, tm)
    pending = []
    for c in range(tn // PROJ_CW):
        cols = slice(c * PROJ_CW, (c + 1) * PROJ_CW)
        accs = []
        for r in range(tm // rc):
            rows = slice(r * rc, (r + 1) * rc)
            accs.append((rows, cols, _dot(u_ref[rows, :], w_ref[:, cols])))
        for item in pending:
            epilogue(*item)
        pending = accs
    for item in pending:
        epilogue(*item)


def _inproj_qk_kernel(u_ref, w_ref, gc_ref, cos_ref, sin_ref, qk_ref):
    last_tile = pl.program_id(1) == pl.num_programs(1) - 1
    tn = w_ref.shape[1]

    def epilogue(rows, cols, acc):
        plain = jnp.logical_and(last_tile, cols.stop == tn)
        for c in range(PROJ_CW // HEAD_DIM):
            sl = slice(cols.start + c * HEAD_DIM, cols.start + (c + 1) * HEAD_DIM)
            x = acc[:, c * HEAD_DIM:(c + 1) * HEAD_DIM]
            y = x * lax.rsqrt(_lane_mean(x * x) + EPS) * gc_ref[:, sl]
            y = y * cos_ref[rows, :] + pltpu.roll(y, HEAD_DIM // 2, 1) * sin_ref[rows, :]
            qk_ref[rows, sl] = jnp.where(plain, x, y).astype(BF16)

    _project_tile(u_ref, w_ref, epilogue)


def _inproj_vb_kernel(u_ref, w_ref, vb_ref):
    def epilogue(rows, cols, acc):
        vb_ref[rows, cols] = acc.astype(BF16)

    _project_tile(u_ref, w_ref, epilogue)


def _in_projection(h_all, mod, g1, w_qk, w_vb, gcol, rope, *, tm, n_lat_tiles, tiles_per_seq, batch):
    n_all, d = h_all.shape
    cos, sin = rope

    def mod_idx(i):
        return (jnp.where(i < n_lat_tiles, i // tiles_per_seq, batch), 0, 0)

    def rope_idx(i, j):
        return (jnp.where(i < n_lat_tiles, i % tiles_per_seq, tiles_per_seq), 0)

    u = pl.pallas_call(
        _norm_kernel,
        grid=(n_all // tm,),
        in_specs=[
            pl.BlockSpec((tm, d), lambda i: (i, 0)),
            pl.BlockSpec((1, N_MOD, d), mod_idx),
            pl.BlockSpec((1, d), lambda i: (0, 0)),
        ],
        out_specs=pl.BlockSpec((tm, d), lambda i: (i, 0)),
        out_shape=jax.ShapeDtypeStruct((n_all, d), BF16),
        compiler_params=_cparams("parallel"),
        name="norm_modulate",
    )(h_all, mod, g1)
    qk = pl.pallas_call(
        _inproj_qk_kernel,
        grid=(n_all // tm, QK_COLS // QK_TN),
        in_specs=[
            pl.BlockSpec((tm, d), lambda i, j: (i, 0)),
            pl.BlockSpec((d, QK_TN), lambda i, j: (0, j)),
            pl.BlockSpec((1, QK_TN), lambda i, j: (0, j)),
            pl.BlockSpec((tm, HEAD_DIM), rope_idx),
            pl.BlockSpec((tm, HEAD_DIM), rope_idx),
        ],
        out_specs=pl.BlockSpec((tm, QK_TN), lambda i, j: (i, j)),
        out_shape=jax.ShapeDtypeStruct((n_all, QK_COLS), BF16),
        compiler_params=_cparams("parallel", "arbitrary"),
        name="in_projection_qk",
    )(u, w_qk, gcol, cos, sin)
    vb = pl.pallas_call(
        _inproj_vb_kernel,
        grid=(n_all // tm, VB_COLS // VB_TN),
        in_specs=[
            pl.BlockSpec((tm, d), lambda i, j: (i, 0)),
            pl.BlockSpec((d, VB_TN), lambda i, j: (0, j)),
        ],
        out_specs=pl.BlockSpec((tm, VB_TN), lambda i, j: (i, j)),
        out_shape=jax.ShapeDtypeStruct((n_all, VB_COLS), BF16),
        compiler_params=_cparams("parallel", "arbitrary"),
        name="in_projection_vb",
    )(u, w_vb)
    return qk, vb, u


ATT_KC = 256
SAFE_SHIFT = 40.0


def _key_chunks(k_refs, v_refs):
    chunks = []
    for k_ref, v_ref in zip(k_refs, v_refs):
        n = k_ref.shape[0]
        for s in range(0, n, ATT_KC):
            chunks.append((k_ref, v_ref, s, min(ATT_KC, n - s)))
    return chunks


def _fill_shifts(bound_ref, m_scr, q_ref, chunks, heads):
    bound = bound_ref[0]

    @pl.when(bound <= SAFE_SHIFT)
    def _():
        m_scr[...] = jnp.full(m_scr.shape, bound, F32)

    @pl.when(bound > SAFE_SHIFT)
    def _():
        for idx, (qcol, kcol) in enumerate(heads):
            q = q_ref[:, qcol:qcol + HEAD_DIM]
            m = None
            for k_ref, _, s, n in chunks:
                part = jnp.max(_dot_nt(q, k_ref[s:s + n, kcol:kcol + HEAD_DIM]), axis=-1, keepdims=True)
                m = part if m is None else jnp.maximum(m, part)
            m_scr[idx] = jnp.broadcast_to(m, m_scr.shape[1:])


def _stream_softmax_pv(q, shift, chunks, kcol, vcols):
    o = None
    l = None
    for k_ref, v_ref, s, n in chunks:
        sc = _dot_nt(q, k_ref[s:s + n, kcol:kcol + HEAD_DIM])
        tiles = [jnp.exp(sc[:, t * LANES:(t + 1) * LANES] - shift) for t in range(n // LANES)]
        for e_t in tiles:
            l = e_t if l is None else l + e_t
        e = tiles[0] if len(tiles) == 1 else jnp.concatenate(tiles, axis=1)
        pv = _dot(e.astype(BF16), v_ref[s:s + n, vcols])
        o = pv if o is None else o + pv
    return o, jnp.sum(l, axis=-1, keepdims=True)


def _diff_attn_kernel(bound_ref, dl_ref, sg_ref, q_ref, *refs, lambda_init, n_pieces):
    k_refs = refs[0:2 * n_pieces:2]
    v_refs = refs[1:2 * n_pieces:2]
    o_ref = refs[2 * n_pieces]
    m_scr = refs[2 * n_pieces + 1]
    chunks = _key_chunks(k_refs, v_refs)
    heads = [(s * HEAD_DIM, s * HEAD_DIM) for s in range(2 * DIFF_HEADS)]
    _fill_shifts(bound_ref, m_scr, q_ref, chunks, heads)
    dl = dl_ref[...]
    lam = (jnp.exp(jnp.sum(dl[0:1] * dl[1:2], axis=-1, keepdims=True))
           - jnp.exp(jnp.sum(dl[2:3] * dl[3:4], axis=-1, keepdims=True)) + lambda_init)
    for h in range(DIFF_HEADS):
        c1 = 2 * h * HEAD_DIM
        c2 = c1 + HEAD_DIM
        vs = slice(h * DIFF_V_DIM, (h + 1) * DIFF_V_DIM)
        o1, l1 = _stream_softmax_pv(q_ref[:, c1:c1 + HEAD_DIM], m_scr[2 * h], chunks, c1, vs)
        o2, l2 = _stream_softmax_pv(q_ref[:, c2:c2 + HEAD_DIM], m_scr[2 * h + 1], chunks, c2, vs)
        o = o1 * (1.0 / l1) - o2 * (lam / l2)
        ms = jnp.mean(o * o, axis=-1, keepdims=True)
        o = o * lax.rsqrt(ms + EPS) * sg_ref[...] * (1.0 - lambda_init)
        o_ref[:, vs] = o.astype(BF16)


def _gqa_kernel(bound_ref, q_ref, *refs, n_pieces):
    k_refs = refs[0:2 * n_pieces:2]
    v_refs = refs[1:2 * n_pieces:2]
    o_ref = refs[2 * n_pieces]
    m_scr = refs[2 * n_pieces + 1]
    chunks = _key_chunks(k_refs, v_refs)
    group = GQA_HEADS // GQA_KV_HEADS
    heads = [(h * HEAD_DIM, (h // group) * HEAD_DIM) for h in range(GQA_HEADS)]
    _fill_shifts(bound_ref, m_scr, q_ref, chunks, heads)
    for h, (qcol, kcol) in enumerate(heads):
        o, l = _stream_softmax_pv(q_ref[:, qcol:qcol + HEAD_DIM], m_scr[h], chunks, kcol,
                                  slice(kcol, kcol + HEAD_DIM))
        o_ref[:, qcol:qcol + HEAD_DIM] = (o * (1.0 / l)).astype(BF16)


def _attention(kind, qk, vb, bound, extra, *, batch, seq, ctx, tq, latent, lambda_init=None):
    n_lat = batch * seq
    ctx_blk0 = n_lat // ctx
    if kind == "diff":
        qcol, kcol, vcol, kvw, v_arr = QK_AQ // 1024, QK_AK // 1024, VB_AV // 1024, 1024, vb
    else:
        qcol, kcol, vcol, kvw, v_arr = QK_CQ // 1024, QK_CK // 256, QK_CV // 256, 256, qk
    if latent:
        q_tiles = seq // tq
        q_spec = pl.BlockSpec((tq, 1024), lambda b, i: (b * q_tiles + i, qcol))
        o_spec = pl.BlockSpec((tq, 1024), lambda b, i: (b * q_tiles + i, 0))
        n_q = n_lat
    else:
        q_tiles = 1
        tq = ctx
        q_spec = pl.BlockSpec((ctx, 1024), lambda b, i: (ctx_blk0 + b, qcol))
        o_spec = pl.BlockSpec((ctx, 1024), lambda b, i: (b, 0))
        n_q = batch * ctx
    kv_specs = [
        pl.BlockSpec((ctx, kvw), lambda b, i: (ctx_blk0 + b, kcol)),
        pl.BlockSpec((ctx, kvw), lambda b, i: (ctx_blk0 + b, vcol)),
    ]
    kv_args = [qk, v_arr]
    if latent:
        kv_specs += [
            pl.BlockSpec((seq, kvw), lambda b, i: (b, kcol)),
            pl.BlockSpec((seq, kvw), lambda b, i: (b, vcol)),
        ]
        kv_args += [qk, v_arr]
    n_pieces = len(kv_args) // 2
    pre_specs = [pl.BlockSpec(memory_space=pltpu.SMEM)]
    pre_args = [bound]
    if kind == "diff":
        diff_lambda, sub_g = extra
        body = functools.partial(_diff_attn_kernel, lambda_init=lambda_init, n_pieces=n_pieces)
        pre_specs += [
            pl.BlockSpec((4, HEAD_DIM), lambda b, i: (0, 0)),
            pl.BlockSpec((1, DIFF_V_DIM), lambda b, i: (0, 0)),
        ]
        pre_args += [diff_lambda, sub_g]
        n_softmax = 2 * DIFF_HEADS
    else:
        body = functools.partial(_gqa_kernel, n_pieces=n_pieces)
        n_softmax = GQA_HEADS
    return pl.pallas_call(
        body,
        grid=(batch, q_tiles),
        in_specs=pre_specs + [q_spec] + kv_specs,
        out_specs=o_spec,
        out_shape=jax.ShapeDtypeStruct((n_q, 1024), BF16),
        scratch_shapes=[pltpu.VMEM((n_softmax, tq, LANES), F32)],
        compiler_params=_cparams("parallel", "arbitrary"),
        name=f"{kind}_attn_{'lat' if latent else 'ctx'}",
    )(*pre_args, qk, *kv_args)


LRU_CW = 512
LRU_PAD = 8


def _lru_kernel(bxc_ref, byc_ref, bxl_ref, byl_ref, cw_ref, cb_ref, wa_ref, ba_ref, wx_ref, bx_ref, lam_ref,
                *refs, ctx, seq, need_ctx):
    if need_ctx:
        yc_ref, yl_ref = refs[0], refs[1]
        scratch = refs[2:]
    else:
        yc_ref, yl_ref = None, refs[0]
        scratch = refs[1:]
    xpad, a_f, b_f, a_b, b_b = scratch
    a_scr = (a_f, a_b)
    b_scr = (b_f, b_b)
    lam = lam_ref[...]
    neg_sp = -LRU_C * jax.nn.softplus(-lam)

    def gates(x_ref, n):
        zeros = jnp.zeros((LRU_PAD, LRU_CW), F32)
        xpad[0:LRU_PAD, :] = zeros
        xpad[LRU_PAD:LRU_PAD + n, :] = x_ref[...].astype(F32)
        xpad[LRU_PAD + n:2 * LRU_PAD + n, :] = zeros
        xc = cb_ref[...] + cw_ref[0:1, :] * xpad[LRU_PAD - 2:LRU_PAD - 2 + n, :]
        for k in range(1, CONV_W):
            xc = xc + cw_ref[k:k + 1, :] * xpad[LRU_PAD - 2 + k:LRU_PAD - 2 + k + n, :]
        xcb = xc.astype(BF16)
        for d in range(2):
            for blk in range(LRU_CW // LRU_BLOCK_W):
                sl = slice(blk * LRU_BLOCK_W, (blk + 1) * LRU_BLOCK_W)
                xs = xcb[:, sl]
                r = jax.nn.sigmoid(_dot(xs, wa_ref[d, blk].astype(BF16)) + ba_ref[d:d + 1, sl])
                gi = jax.nn.sigmoid(_dot(xs, wx_ref[d, blk].astype(BF16)) + bx_ref[d:d + 1, sl])
                log_a = r * neg_sp[d:d + 1, sl]
                a = jnp.exp(log_a)
                mult = jnp.sqrt(1.0 - a * a)
                a_scr[d][0:n, sl] = a
                b_scr[d][0:n, sl] = mult * gi * xc[:, sl]

    def scan(n, hf0, hb0):
        def body(t, carry):
            hf, hb = carry
            tb = n - 1 - t
            hf = a_f[pl.ds(t, 1), :] * hf + b_f[pl.ds(t, 1), :]
            b_f[pl.ds(t, 1), :] = hf
            hb = a_b[pl.ds(tb, 1), :] * hb + b_b[pl.ds(tb, 1), :]
            b_b[pl.ds(tb, 1), :] = hb
            return hf, hb
        return lax.fori_loop(0, n, body, (hf0, hb0), unroll=8)

    zero = jnp.zeros((1, LRU_CW), F32)
    gates(bxc_ref, ctx)
    hf, hb = scan(ctx, zero, zero)
    if need_ctx:
        yc_ref[...] = ((b_f[0:ctx, :] + b_b[0:ctx, :])
                       * jax.nn.gelu(byc_ref[...].astype(F32), approximate=True)).astype(BF16)
    gates(bxl_ref, seq)
    scan(seq, hf, hb)
    yl_ref[...] = ((b_f[0:seq, :] + b_b[0:seq, :])
                   * jax.nn.gelu(byl_ref[...].astype(F32), approximate=True)).astype(BF16)


def _rglru(vb, conv_w, conv_b, w_a, b_a, w_x, b_x, lam, *, batch, seq, ctx, need_ctx):
    n_lat = batch * seq
    ctx_blk0 = n_lat // ctx
    cbx, cby = VB_BX // LRU_CW, VB_BY // LRU_CW
    nblk = LRU_CW // LRU_BLOCK_W
    tmax = max(seq, ctx)
    out_specs = [pl.BlockSpec((seq, LRU_CW), lambda b, c: (b, c))]
    out_shape = [jax.ShapeDtypeStruct((n_lat, LRU_WIDTH), BF16)]
    if need_ctx:
        out_specs = [pl.BlockSpec((ctx, LRU_CW), lambda b, c: (b, c))] + out_specs
        out_shape = [jax.ShapeDtypeStruct((batch * ctx, LRU_WIDTH), BF16)] + out_shape
    outs = pl.pallas_call(
        functools.partial(_lru_kernel, ctx=ctx, seq=seq, need_ctx=need_ctx),
        grid=(batch, LRU_WIDTH // LRU_CW),
        in_specs=[
            pl.BlockSpec((ctx, LRU_CW), lambda b, c: (ctx_blk0 + b, cbx + c)),
            pl.BlockSpec((ctx, LRU_CW), lambda b, c: (ctx_blk0 + b, cby + c)),
            pl.BlockSpec((seq, LRU_CW), lambda b, c: (b, cbx + c)),
            pl.BlockSpec((seq, LRU_CW), lambda b, c: (b, cby + c)),
            pl.BlockSpec((CONV_W, LRU_CW), lambda b, c: (0, c)),
            pl.BlockSpec((1, LRU_CW), lambda b, c: (0, c)),
            pl.BlockSpec((2, nblk, LRU_BLOCK_W, LRU_BLOCK_W), lambda b, c: (0, c, 0, 0)),
            pl.BlockSpec((2, LRU_CW), lambda b, c: (0, c)),
            pl.BlockSpec((2, nblk, LRU_BLOCK_W, LRU_BLOCK_W), lambda b, c: (0, c, 0, 0)),
            pl.BlockSpec((2, LRU_CW), lambda b, c: (0, c)),
            pl.BlockSpec((2, LRU_CW), lambda b, c: (0, c)),
        ],
        out_specs=out_specs,
        out_shape=out_shape,
        scratch_shapes=[
            pltpu.VMEM((tmax + 2 * LRU_PAD, LRU_CW), F32),
            pltpu.VMEM((tmax, LRU_CW), F32),
            pltpu.VMEM((tmax, LRU_CW), F32),
            pltpu.VMEM((tmax, LRU_CW), F32),
            pltpu.VMEM((tmax, LRU_CW), F32),
        ],
        compiler_params=_cparams("parallel", "arbitrary"),
        name="rglru",
    )(vb, vb, vb, vb, conv_w, conv_b, w_a, b_a, w_x, b_x, lam)
    if need_ctx:
        return outs[1], outs[0]
    return outs[0], None


MERGE_TN = 256


def _merge_kernel(u_ref, ya_ref, yb_ref, yc_ref, wg0_ref, wg1_ref, wg2_ref, bg_ref,
                  wa_ref, wb_ref, wc_ref, m_ref):
    u = u_ref[...]
    m = None
    for k, (y_ref, wg_ref, wbr_ref) in enumerate(
            ((ya_ref, wg0_ref, wa_ref), (yb_ref, wg1_ref, wb_ref), (yc_ref, wg2_ref, wc_ref))):
        g = jax.nn.sigmoid(_dot(u, wg_ref[...]) + bg_ref[k:k + 1, :])
        part = g * _dot(y_ref[...], wbr_ref[...])
        m = part if m is None else m + part
    m_ref[...] = m.astype(BF16)


def _merge(u, ya, yb, yc, w_gate, b_gate, wba, wbb, wbc, *, n_tok, tm):
    d = u.shape[1]
    nj = d // MERGE_TN
    y_spec = pl.BlockSpec((tm, BRANCH_W), lambda i, j: (i, 0))
    wbr_spec = pl.BlockSpec((BRANCH_W, MERGE_TN), lambda i, j: (0, j))
    return pl.pallas_call(
        _merge_kernel,
        grid=(n_tok // tm, nj),
        in_specs=[
            pl.BlockSpec((tm, d), lambda i, j: (i, 0)),
            y_spec, y_spec, y_spec,
            pl.BlockSpec((d, MERGE_TN), lambda i, j: (0, j)),
            pl.BlockSpec((d, MERGE_TN), lambda i, j: (0, nj + j)),
            pl.BlockSpec((d, MERGE_TN), lambda i, j: (0, 2 * nj + j)),
            pl.BlockSpec((N_BRANCH, MERGE_TN), lambda i, j: (0, j)),
            wbr_spec, wbr_spec, wbr_spec,
        ],
        out_specs=pl.BlockSpec((tm, MERGE_TN), lambda i, j: (i, j)),
        out_shape=jax.ShapeDtypeStruct((n_tok, d), BF16),
        compiler_params=_cparams("parallel", "arbitrary"),
        name="branch_merge",
    )(u, ya, yb, yc, w_gate, w_gate, w_gate, b_gate, wba, wbb, wbc)


OUT_RC = 256
ROUTE_E, ROUTE_P, ROUTE_RANK = 0, 2, 4
COUNT_ROWS = 8


def _outproj_kernel(m_ref, h_ref, mod_ref, g_ref, wo_ref, wr_ref, br_ref,
                    h1_ref, vp_ref, route_ref, cnt_ref, v_scr):
    tm, d = h_ref.shape
    rc = min(OUT_RC, tm)
    for r in range(tm // rc):
        rows = slice(r * rc, (r + 1) * rc)
        h1 = h_ref[rows, :] + mod_ref[0, 2:3, :] * _dot(m_ref[rows, :], wo_ref[...])
        h1_ref[rows, :] = h1
        ms = jnp.mean(h1 * h1, axis=-1, keepdims=True)
        y = h1 * lax.rsqrt(ms + EPS) * g_ref[...]
        v_scr[rows, :] = (y * (1.0 + mod_ref[0, 4:5, :]) + mod_ref[0, 3:4, :]).astype(BF16)
    v = v_scr[...]
    half = d // 2
    lo = lax.bitcast_convert_type(v[:, :half].astype(F32), U32)
    hi = lax.bitcast_convert_type(v[:, half:].astype(F32), U32)
    vp_ref[...] = (lo >> 16) | (hi & jnp.uint32(0xFFFF0000))

    logits = _dot(v, wr_ref[...]) + br_ref[...]
    lane = lax.broadcasted_iota(I32, logits.shape, 1).astype(F32)
    neg = jnp.float32(-jnp.inf)
    big = jnp.float32(1e9)

    def masked_argmax(mask):
        val = jnp.max(jnp.where(mask, logits, neg), axis=-1, keepdims=True)
        idx = jnp.min(jnp.where(jnp.logical_and(mask, logits == val), lane, big), axis=-1, keepdims=True)
        return val, idx

    gmask = lane < N_GROUPS
    gmax, gidx = masked_argmax(gmask)
    gsum = jnp.sum(jnp.where(gmask, jnp.exp(logits - gmax), 0.0), axis=-1, keepdims=True)
    g_w = 1.0 / gsum
    lo_lane = N_GROUPS + EXPERTS_PER_GROUP * gidx
    emask = jnp.logical_and(lane >= lo_lane, lane < lo_lane + EXPERTS_PER_GROUP)
    v1, i1 = masked_argmax(emask)
    v2, i2 = masked_argmax(jnp.logical_and(emask, lane != i1))
    e21 = jnp.exp(v2 - v1)
    p1 = g_w / (1.0 + e21)
    p2 = g_w * e21 / (1.0 + e21)
    e1 = i1 - N_GROUPS
    e2 = i2 - N_GROUPS
    oh1 = jnp.where(lane == e1, 1.0, 0.0)
    oh2 = jnp.where(lane == e2, 1.0, 0.0)
    cnt = oh1 + oh2
    row = lax.broadcasted_iota(I32, (tm, tm), 0)
    col = lax.broadcasted_iota(I32, (tm, tm), 1)
    before = jnp.where(row > col, 1.0, 0.0).astype(BF16)
    prefix = _dot(before, cnt.astype(BF16))
    rank1 = jnp.sum(prefix * oh1, axis=-1, keepdims=True)
    rank2 = jnp.sum(prefix * oh2, axis=-1, keepdims=True)
    route = jnp.where(lane == 0, e1, jnp.where(lane == 1, e2, jnp.where(lane == 2, p1, jnp.where(
        lane == 3, p2, jnp.where(lane == 4, rank1, jnp.where(lane == 5, rank2, 0.0))))))
    route_ref[...] = route
    cnt_ref[0] = jnp.broadcast_to(jnp.sum(cnt, axis=0, keepdims=True), (COUNT_ROWS, LANES))


def _out_projection(m, h_all, mod, g2, w_out, w_router, b_router, *, n_tok, tm, n_lat, seq, batch):
    d = m.shape[1]
    n_tiles = n_tok // tm

    def mod_idx(i):
        return (jnp.where(i * tm < n_lat, (i * tm) // seq, batch), 0, 0)

    return pl.pallas_call(
        _outproj_kernel,
        grid=(n_tiles,),
        in_specs=[
            pl.BlockSpec((tm, d), lambda i: (i, 0)),
            pl.BlockSpec((tm, d), lambda i: (i, 0)),
            pl.BlockSpec((1, N_MOD, d), mod_idx),
            pl.BlockSpec((1, d), lambda i: (0, 0)),
            pl.BlockSpec((d, d), lambda i: (0, 0)),
            pl.BlockSpec((d, LANES), lambda i: (0, 0)),
            pl.BlockSpec((1, LANES), lambda i: (0, 0)),
        ],
        out_specs=[
            pl.BlockSpec((tm, d), lambda i: (i, 0)),
            pl.BlockSpec((tm, d // 2), lambda i: (i, 0)),
            pl.BlockSpec((tm, LANES), lambda i: (i, 0)),
            pl.BlockSpec((1, COUNT_ROWS, LANES), lambda i: (i, 0, 0)),
        ],
        out_shape=[
            jax.ShapeDtypeStruct((n_tok, d), F32),
            jax.ShapeDtypeStruct((n_tok, d // 2), U32),
            jax.ShapeDtypeStruct((n_tok, LANES), F32),
            jax.ShapeDtypeStruct((n_tiles, COUNT_ROWS, LANES), F32),
        ],
        scratch_shapes=[pltpu.VMEM((tm, d), BF16)],
        compiler_params=_cparams("parallel"),
        name="out_projection_router",
    )(m, h_all, mod, g2, w_out, w_router, b_router)


MOE_TM = 256
ROW_TM = 256


def _dispatch(route, counts, n_tok, tm_route):
    n_tiles = (2 * n_tok) // MOE_TM + N_EXPERTS
    counts = counts[:, 0, :N_EXPERTS].astype(I32)
    tile_base = jnp.cumsum(counts, axis=0) - counts
    total = jnp.sum(counts, axis=0)
    padded = ((total + MOE_TM - 1) // MOE_TM) * MOE_TM
    seg_end = jnp.cumsum(padded)
    base = (seg_end - padded)[None, :] + tile_base
    e = route[:, ROUTE_E:ROUTE_E + 2].astype(I32)
    rank = route[:, ROUTE_RANK:ROUTE_RANK + 2].astype(I32)
    tile_id = jnp.arange(n_tok, dtype=I32) // tm_route
    dest = jnp.take(base.reshape(-1), tile_id[:, None] * N_EXPERTS + e, axis=0) + rank
    tile_start = jnp.arange(n_tiles, dtype=I32) * MOE_TM
    n_used = seg_end[-1] // MOE_TM
    tile_valid = (tile_start < seg_end[-1]).astype(I32)
    tile_index = jnp.minimum(jnp.arange(n_tiles, dtype=I32), n_used - 1)
    tile_expert = jnp.sum((seg_end[None, :] <= (tile_index * MOE_TM)[:, None]).astype(I32), axis=1)
    prev = jnp.concatenate([jnp.full((1,), -1, I32), tile_expert[:-1]])
    tile_first = (tile_expert != prev).astype(I32)
    tails = jnp.maximum(seg_end - MOE_TM, 0).astype(I32)
    return dest, tile_expert, tile_first, tile_valid, tile_index, tails


def _row_scatter_kernel(tail_ref, d0_ref, d1_ref, v_ref, x_hbm, zbuf, sem_z, sem):
    @pl.when(pl.program_id(0) == 0)
    def _():
        zbuf[...] = jnp.zeros_like(zbuf)
        def zero_tile(start):
            start = pl.multiple_of(start, MOE_TM)
            return pltpu.make_async_copy(zbuf, x_hbm.at[pl.ds(start, MOE_TM), :], sem_z)
        for e in range(N_EXPERTS):
            zero_tile(tail_ref[e]).start()
        for e in range(N_EXPERTS):
            zero_tile(tail_ref[e]).wait()
        first_unused = tail_ref[N_EXPERTS - 1] // MOE_TM + 1
        n_tiles = x_hbm.shape[0] // MOE_TM

        def start_unused(t, carry):
            zero_tile(t * MOE_TM).start()
            return carry

        def wait_unused(t, carry):
            zero_tile(t * MOE_TM).wait()
            return carry
        lax.fori_loop(first_unused, n_tiles, start_unused, 0)
        lax.fori_loop(first_unused, n_tiles, wait_unused, 0)

    def issue(r, carry):
        src = v_ref.at[pl.ds(r, 1), :]
        pltpu.make_async_copy(src, x_hbm.at[pl.ds(d0_ref[0, 0, r], 1), :], sem).start()
        pltpu.make_async_copy(src, x_hbm.at[pl.ds(d1_ref[0, 0, r], 1), :], sem).start()
        return carry
    lax.fori_loop(0, ROW_TM, issue, 0, unroll=8)

    def drain(r, carry):
        src = v_ref.at[pl.ds(r, 1), :]
        pltpu.make_async_copy(src, x_hbm.at[pl.ds(0, 1), :], sem).wait()
        pltpu.make_async_copy(src, x_hbm.at[pl.ds(0, 1), :], sem).wait()
        return carry
    lax.fori_loop(0, ROW_TM, drain, 0, unroll=8)


def _row_scatter(vp, dest, tails, n_rows):
    n_tok, half = vp.shape
    n_steps = n_tok // ROW_TM
    d0 = dest[:, 0].reshape(n_steps, 1, ROW_TM)
    d1 = dest[:, 1].reshape(n_steps, 1, ROW_TM)
    grid_spec = pltpu.PrefetchScalarGridSpec(
        num_scalar_prefetch=1,
        grid=(n_steps,),
        in_specs=[
            pl.BlockSpec((1, 1, ROW_TM), lambda i, t: (i, 0, 0), memory_space=pltpu.SMEM),
            pl.BlockSpec((1, 1, ROW_TM), lambda i, t: (i, 0, 0), memory_space=pltpu.SMEM),
            pl.BlockSpec((ROW_TM, half), lambda i, t: (i, 0)),
        ],
        out_specs=pl.BlockSpec(memory_space=pl.ANY),
        scratch_shapes=[
            pltpu.VMEM((MOE_TM, half), U32),
            pltpu.SemaphoreType.DMA(()),
            pltpu.SemaphoreType.DMA(()),
        ],
    )
    return pl.pallas_call(
        _row_scatter_kernel,
        grid_spec=grid_spec,
        out_shape=jax.ShapeDtypeStruct((n_rows, half), U32),
        compiler_params=_cparams("arbitrary"),
        name="moe_row_scatter",
    )(tails, d0, d1, vp)


def _moe_kernel(te_ref, tf_ref, tv_ref, ti_ref, x_ref, w1_ref, w3_ref, w2_ref, y_ref, w1b, w3b, w2b):
    i = pl.program_id(0)
    valid = tv_ref[i] == 1

    @pl.when(valid)
    def _():
        @pl.when(tf_ref[i] == 1)
        def _():
            w1b[...] = w1_ref[0, 0].astype(BF16)
            w3b[...] = w3_ref[0, 0].astype(BF16)
            w2b[...] = w2_ref[0, 0].astype(BF16)

        half = x_ref.shape[1]
        xp = x_ref[...]
        lo = lax.bitcast_convert_type(xp << 16, F32).astype(BF16)
        hi = lax.bitcast_convert_type(xp & jnp.uint32(0xFFFF0000), F32).astype(BF16)
        h1 = _dot(lo, w1b[0:half, :]) + _dot(hi, w1b[half:, :])
        h3 = _dot(lo, w3b[0:half, :]) + _dot(hi, w3b[half:, :])
        hh = (h1 * jax.nn.sigmoid(h1) * h3).astype(BF16)
        y_ref[...] = _dot(hh, w2b[...])

    @pl.when(jnp.logical_not(valid))
    def _():
        y_ref[...] = jnp.zeros_like(y_ref)


def _moe_experts(layer, x_sorted, tile_expert, tile_first, tile_valid, tile_index, w1, w3, w2):
    n_tiles = tile_expert.shape[0]
    half = x_sorted.shape[1]
    d = 2 * half
    grid_spec = pltpu.PrefetchScalarGridSpec(
        num_scalar_prefetch=4,
        grid=(n_tiles,),
        in_specs=[
            pl.BlockSpec((MOE_TM, half), lambda i, te, tf, tv, ti: (ti[i], 0)),
            pl.BlockSpec((1, 1, d, D_EXPERT), lambda i, te, tf, tv, ti: (layer, te[i], 0, 0)),
            pl.BlockSpec((1, 1, d, D_EXPERT), lambda i, te, tf, tv, ti: (layer, te[i], 0, 0)),
            pl.BlockSpec((1, 1, D_EXPERT, d), lambda i, te, tf, tv, ti: (layer, te[i], 0, 0)),
        ],
        out_specs=pl.BlockSpec((MOE_TM, d), lambda i, te, tf, tv, ti: (i, 0)),
        scratch_shapes=[
            pltpu.VMEM((d, D_EXPERT), BF16),
            pltpu.VMEM((d, D_EXPERT), BF16),
            pltpu.VMEM((D_EXPERT, d), BF16),
        ],
    )
    return pl.pallas_call(
        _moe_kernel,
        grid_spec=grid_spec,
        out_shape=jax.ShapeDtypeStruct((n_tiles * MOE_TM, d), F32),
        compiler_params=_cparams("arbitrary"),
        name="moe_experts",
    )(tile_expert, tile_first, tile_valid, tile_index, x_sorted, w1, w3, w2)


def _combine_kernel(d0_ref, d1_ref, h_ref, route_ref, mod_ref, y_hbm, o_ref, ybuf, sem):
    def issue(r, carry):
        pltpu.make_async_copy(y_hbm.at[pl.ds(d0_ref[0, 0, r], 1), :], ybuf.at[0, pl.ds(r, 1), :], sem).start()
        pltpu.make_async_copy(y_hbm.at[pl.ds(d1_ref[0, 0, r], 1), :], ybuf.at[1, pl.ds(r, 1), :], sem).start()
        return carry
    lax.fori_loop(0, ROW_TM, issue, 0, unroll=8)

    def drain(r, carry):
        pltpu.make_async_copy(y_hbm.at[pl.ds(0, 1), :], ybuf.at[0, pl.ds(r, 1), :], sem).wait()
        pltpu.make_async_copy(y_hbm.at[pl.ds(0, 1), :], ybuf.at[1, pl.ds(r, 1), :], sem).wait()
        return carry
    lax.fori_loop(0, ROW_TM, drain, 0, unroll=8)
    p0 = route_ref[:, ROUTE_P:ROUTE_P + 1]
    p1 = route_ref[:, ROUTE_P + 1:ROUTE_P + 2]
    o_ref[...] = h_ref[...] + mod_ref[0, 5:6, :] * (p0 * ybuf[0] + p1 * ybuf[1])


def _combine(h1, route, mod, y_sorted, dest, *, n_tok, n_lat, seq, batch):
    d = h1.shape[1]
    tm = ROW_TM
    n_steps = n_tok // tm
    d0 = dest[:, 0].reshape(n_steps, 1, tm)
    d1 = dest[:, 1].reshape(n_steps, 1, tm)

    def mod_idx(i):
        return (jnp.where(i * tm < n_lat, (i * tm) // seq, batch), 0, 0)

    return pl.pallas_call(
        _combine_kernel,
        grid=(n_steps,),
        in_specs=[
            pl.BlockSpec((1, 1, tm), lambda i: (i, 0, 0), memory_space=pltpu.SMEM),
            pl.BlockSpec((1, 1, tm), lambda i: (i, 0, 0), memory_space=pltpu.SMEM),
            pl.BlockSpec((tm, d), lambda i: (i, 0)),
            pl.BlockSpec((tm, LANES), lambda i: (i, 0)),
            pl.BlockSpec((1, N_MOD, d), mod_idx),
            pl.BlockSpec(memory_space=pl.ANY),
        ],
        out_specs=pl.BlockSpec((tm, d), lambda i: (i, 0)),
        out_shape=jax.ShapeDtypeStruct((n_tok, d), F32),
        scratch_shapes=[pltpu.VMEM((2, tm, d), F32), pltpu.SemaphoreType.DMA(())],
        compiler_params=_cparams("arbitrary"),
        name="moe_combine",
    )(d0, d1, h1, route, mod, y_sorted)


def _rope_tables(seq, tm):
    n_rows = seq // GRID_W
    row = jnp.repeat(jnp.arange(n_rows), GRID_W).astype(F32)
    col = jnp.tile(jnp.arange(GRID_W), n_rows).astype(F32)
    half = HEAD_DIM // 2
    inv = 1.0 / (ROPE_THETA ** (jnp.arange(0, half, 2, dtype=F32) / half))
    ang_r = row[:, None] * inv
    ang_c = col[:, None] * inv
    ang = jnp.concatenate([ang_r, ang_c, ang_r, ang_c], axis=-1)
    cos, sin = jnp.cos(ang), jnp.sin(ang)
    sin_signed = jnp.where(jnp.arange(HEAD_DIM) < half, -sin, sin)
    ident = jnp.zeros((tm, HEAD_DIM), F32)
    return jnp.concatenate([cos, ident + 1.0]), jnp.concatenate([sin_signed, ident])


def _permute_heads(a):
    lead = a.shape[:-1]
    quarter = HEAD_DIM // 4
    assert HEAD_PERM[quarter] == 2 * quarter and HEAD_PERM[2 * quarter] == quarter
    q = a.reshape(*lead, a.shape[-1] // HEAD_DIM, 4, quarter)
    q = jnp.stack([q[..., 0, :], q[..., 2, :], q[..., 1, :], q[..., 3, :]], axis=-2)
    return q.reshape(*lead, a.shape[-1])


def _score_bound(gq, gk):
    return (1.02 * HEAD_DIM * jnp.max(jnp.abs(gq)) * jnp.max(jnp.abs(gk))).reshape(1).astype(F32)


def kernel(x, c, ctx, c_ctx, w_ada, b_ada, norm1_g, norm2_g, w_in, b_gate, q_norm_a, k_norm_a, diff_lambda,
           sub_norm_a, q_norm_c, k_norm_c, conv_w, conv_b, lru_w_a, lru_b_a, lru_w_x, lru_b_x, lru_lambda,
           w_branch_a, w_branch_b, w_branch_c, w_out, w_group, b_group, w_route, b_route, w1, w3, w2):
    batch, seq, d = x.shape
    ctx_len = ctx.shape[1]
    depth = w_ada.shape[0]
    n_lat = batch * seq
    n_ctx = batch * ctx_len
    n_all = n_lat + n_ctx
    tm = min(1024, seq, n_ctx)
    assert seq % tm == 0 and n_ctx % tm == 0 and n_lat % ctx_len == 0 and batch < MOD_ROWS
    assert seq % GRID_W == 0 and seq % ROW_TM == 0 and n_ctx % ROW_TM == 0
    tq = min(256, seq)
    tm_out = min(512, tm)

    cc = jnp.zeros((MOD_ROWS, d), F32).at[:batch].set(c).at[batch].set(c_ctx)
    mod_all = _ada_modulation(cc, w_ada, b_ada).reshape(depth, MOD_ROWS, N_MOD, d)
    rope = _rope_tables(seq, tm)
    h_all = jnp.concatenate([x.reshape(n_lat, d), ctx.reshape(n_ctx, d)], axis=0)
    scale = HEAD_DIM ** -0.5

    for l in range(depth):
        last = l == depth - 1
        lambda_init = 0.8 - 0.6 * math.exp(-0.3 * l)
        mod = mod_all[l]
        wl = w_in[l]
        w_qk = jnp.concatenate([_permute_heads(wl[:, 0:2048]), _permute_heads(wl[:, 5120:6400]),
                                wl[:, 6400:6656]], axis=1).astype(BF16)
        w_vb = wl[:, 2048:5120].astype(BF16)
        w_gate = wl[:, 6656:].astype(BF16)
        gq_a, gq_c = q_norm_a[l] * scale, q_norm_c[l] * scale
        gcol = jnp.concatenate([
            _permute_heads(jnp.concatenate([
                jnp.tile(gq_a, 2 * DIFF_HEADS), jnp.tile(k_norm_a[l], 2 * DIFF_HEADS),
                jnp.tile(gq_c, GQA_HEADS), jnp.tile(k_norm_c[l], GQA_KV_HEADS)])),
            jnp.ones((QK_COLS - QK_CV,), F32)]).reshape(1, QK_COLS)
        bound_a = _score_bound(gq_a, k_norm_a[l])
        bound_c = _score_bound(gq_c, k_norm_c[l])

        qk, vb, u = _in_projection(h_all, mod, norm1_g[l].reshape(1, d), w_qk, w_vb, gcol, rope, tm=tm,
                                   n_lat_tiles=n_lat // tm, tiles_per_seq=seq // tm, batch=batch)
        dims = dict(batch=batch, seq=seq, ctx=ctx_len)
        extra_a = (diff_lambda[l], sub_norm_a[l].reshape(1, DIFF_V_DIM))
        ya = _attention("diff", qk, vb, bound_a, extra_a, tq=tq, latent=True, lambda_init=lambda_init, **dims)
        yc = _attention("gqa", qk, vb, bound_c, None, tq=tq, latent=True, **dims)
        yb, yb_c = _rglru(vb, conv_w[l], conv_b[l].reshape(1, LRU_WIDTH), lru_w_a[l], lru_b_a[l],
                          lru_w_x[l], lru_b_x[l], lru_lambda[l], need_ctx=not last, **dims)
        if last:
            n_tok = n_lat
        else:
            n_tok = n_all
            ya_c = _attention("diff", qk, vb, bound_a, extra_a, tq=tq, latent=False, lambda_init=lambda_init,
                              **dims)
            yc_c = _attention("gqa", qk, vb, bound_c, None, tq=tq, latent=False, **dims)
            ya = jnp.concatenate([ya, ya_c], axis=0)
            yb = jnp.concatenate([yb, yb_c], axis=0)
            yc = jnp.concatenate([yc, yc_c], axis=0)
        m = _merge(u, ya, yb, yc, w_gate, b_gate[l], w_branch_a[l].astype(BF16), w_branch_b[l].astype(BF16),
                   w_branch_c[l].astype(BF16), n_tok=n_tok, tm=tm)
        w_router = jnp.zeros((d, LANES), F32).at[:, :N_GROUPS].set(w_group[l])
        w_router = w_router.at[:, N_GROUPS:N_GROUPS + N_EXPERTS].set(w_route[l]).astype(BF16)
        b_router = jnp.zeros((1, LANES), F32).at[0, :N_GROUPS].set(b_group[l])
        b_router = b_router.at[0, N_GROUPS:N_GROUPS + N_EXPERTS].set(b_route[l])
        h1, vp, route, counts = _out_projection(m, h_all, mod, norm2_g[l].reshape(1, d),
                                                w_out[l].astype(BF16), w_router, b_router, n_tok=n_tok,
                                                tm=tm_out, n_lat=n_lat, seq=seq, batch=batch)
        dest, tile_expert, tile_first, tile_valid, tile_index, tails = _dispatch(route, counts, n_tok, tm_out)
        x_sorted = _row_scatter(vp, dest, tails, tile_expert.shape[0] * MOE_TM)
        y_sorted = _moe_experts(l, x_sorted, tile_expert, tile_first, tile_valid, tile_index, w1, w3, w2)
        h_all = _combine(h1, route, mod, y_sorted, dest, n_tok=n_tok, n_lat=n_lat, seq=seq, batch=batch)
    return h_all.reshape(batch, seq, d)
```

```python
import functools
import math

import jax
import jax.numpy as jnp
from jax import lax
from jax.experimental import pallas as pl
from jax.experimental.pallas import tpu as pltpu

F32 = jnp.float32
BF16 = jnp.bfloat16
I32 = jnp.int32
U32 = jnp.uint32

EPS = 1e-6
HEAD_DIM = 128
GRID_W = 64
ROPE_THETA = 10000.0
DIFF_HEADS = 4
DIFF_V_DIM = 2 * HEAD_DIM
LRU_WIDTH = 1024
LRU_BLOCK_W = 128
CONV_W = 4
LRU_C = 8.0
GQA_HEADS = 8
GQA_KV_HEADS = 2
N_BRANCH = 3
BRANCH_W = 1024
N_GROUPS = 4
EXPERTS_PER_GROUP = 8
N_EXPERTS = N_GROUPS * EXPERTS_PER_GROUP
D_EXPERT = 512
N_MOD = 6
MOD_ROWS = 16
LANES = 128

QK_AQ, QK_AK, QK_CQ, QK_CK, QK_CV = 0, 1024, 2048, 3072, 3328
VB_AV, VB_BX, VB_BY = 0, 1024, 2048
QK_COLS = 3584
VB_COLS = 3072
QK_TN = QK_COLS // 2
VB_TN = VB_COLS // 2
PROJ_CW = 256
PROJ_RC = 512
HEAD_PERM = tuple(range(0, 32)) + tuple(range(64, 96)) + tuple(range(32, 64)) + tuple(range(96, 128))

VMEM_LIMIT = 56 * 1024 * 1024


def _cparams(*sem):
    return pltpu.CompilerParams(dimension_semantics=sem, vmem_limit_bytes=VMEM_LIMIT)


def _dot(a, b):
    return jnp.dot(a, b, preferred_element_type=F32)


def _dot_nt(a, b):
    return lax.dot_general(a, b, (((1,), (1,)), ((), ())), preferred_element_type=F32)


def _sigmoid(x):
    return 0.5 * jnp.tanh(0.5 * x) + 0.5


def _lane_mean(x2):
    return _dot(x2.astype(BF16), jnp.full((LANES, LANES), 1.0 / LANES, BF16))


def _ada_kernel(c_ref, w_ref, b_ref, o_ref):
    c = c_ref[...]
    s = (c * jax.nn.sigmoid(c)).astype(BF16)
    o_ref[0] = _dot(s, w_ref[0].astype(BF16)) + b_ref[0]


def _ada_modulation(cc, w_ada, b_ada):
    depth, d, n = w_ada.shape
    tn = 1024
    return pl.pallas_call(
        _ada_kernel,
        grid=(depth, n // tn),
        in_specs=[
            pl.BlockSpec((MOD_ROWS, d), lambda l, j: (0, 0)),
            pl.BlockSpec((1, d, tn), lambda l, j: (l, 0, j)),
            pl.BlockSpec((1, 1, tn), lambda l, j: (l, 0, j)),
        ],
        out_specs=pl.BlockSpec((1, MOD_ROWS, tn), lambda l, j: (l, 0, j)),
        out_shape=jax.ShapeDtypeStruct((depth, MOD_ROWS, n), F32),
        compiler_params=_cparams("parallel", "parallel"),
        name="ada_modulation",
    )(cc, w_ada, b_ada.reshape(depth, 1, n))


NORM_ROWS = 32


def _stream_specs(h_src, tm, n_lat, n_tiles):
    nl = n_lat // tm
    if isinstance(h_src, tuple) and n_tiles > nl:
        d = h_src[0].shape[1]
        return list(h_src), [pl.BlockSpec((tm, d), lambda i: (jnp.minimum(i, nl - 1), 0)),
                             pl.BlockSpec((tm, d), lambda i: (jnp.maximum(i - nl, 0), 0))]
    arr = h_src[0] if isinstance(h_src, tuple) else h_src
    return [arr], [pl.BlockSpec((tm, arr.shape[1]), lambda i: (i, 0))]


def _read_stream(h_refs, rows, n_lat_tiles):
    if len(h_refs) == 1:
        return h_refs[0][rows, :]
    return jnp.where(pl.program_id(0) < n_lat_tiles, h_refs[0][rows, :], h_refs[1][rows, :])


def _norm_modulate_rows(h_refs, n_lat_tiles, g_ref, mod_ref, shift_row, scale_row, out_ref):
    g = g_ref[...]
    sc = 1.0 + mod_ref[0, scale_row:scale_row + 1, :]
    sh = mod_ref[0, shift_row:shift_row + 1, :]

    def body(i, carry):
        rows = pl.ds(pl.multiple_of(i * NORM_ROWS, NORM_ROWS), NORM_ROWS)
        x = _read_stream(h_refs, rows, n_lat_tiles)
        ms = jnp.mean(x * x, axis=-1, keepdims=True)
        y = (x * lax.rsqrt(ms + EPS) * g) * sc + sh
        out_ref[rows, :] = y.astype(out_ref.dtype)
        return carry
    lax.fori_loop(0, out_ref.shape[0] // NORM_ROWS, body, 0)


def _norm_kernel(*refs, n_lat_tiles):
    mod_ref, g_ref, u_ref = refs[-3:]
    _norm_modulate_rows(refs[:-3], n_lat_tiles, g_ref, mod_ref, 0, 1, u_ref)


def _project_tile(u_ref, w_ref, epilogue):
    tm, tn = u_ref.shape[0], w_ref.shape[1]
    rc = min(PROJ_RC, tm)
    pending = []
    for c in range(tn // PROJ_CW):
        cols = slice(c * PROJ_CW, (c + 1) * PROJ_CW)
        accs = []
        for r in range(tm // rc):
            rows = slice(r * rc, (r + 1) * rc)
            accs.append((rows, cols, _dot(u_ref[rows, :], w_ref[:, cols])))
        for item in pending:
            epilogue(*item)
        pending = accs
    for item in pending:
        epilogue(*item)


def _inproj_qk_kernel(u_ref, w_ref, gc_ref, cos_ref, sin_ref, qk_ref):
    last_tile = pl.program_id(1) == pl.num_programs(1) - 1
    tn = w_ref.shape[1]

    def epilogue(rows, cols, acc):
        plain = jnp.logical_and(last_tile, cols.stop == tn)
        for c in range(PROJ_CW // HEAD_DIM):
            sl = slice(cols.start + c * HEAD_DIM, cols.start + (c + 1) * HEAD_DIM)
            x = acc[:, c * HEAD_DIM:(c + 1) * HEAD_DIM]
            y = x * lax.rsqrt(_lane_mean(x * x) + EPS) * gc_ref[:, sl]
            y = y * cos_ref[rows, :] + pltpu.roll(y, HEAD_DIM // 2, 1) * sin_ref[rows, :]
            qk_ref[rows, sl] = jnp.where(plain, x, y).astype(BF16)

    _project_tile(u_ref, w_ref, epilogue)


def _inproj_vb_kernel(u_ref, w_ref, vb_ref):
    def epilogue(rows, cols, acc):
        vb_ref[rows, cols] = acc.astype(BF16)

    _project_tile(u_ref, w_ref, epilogue)


def _in_projection(h_src, mod, g1, w_qk, w_vb, gcol, rope, *, n_all, tm, n_lat_tiles, tiles_per_seq, batch):
    d = g1.shape[1]
    cos, sin = rope

    def mod_idx(i):
        return (jnp.where(i < n_lat_tiles, i // tiles_per_seq, batch), 0, 0)

    def rope_idx(i, j):
        return (jnp.where(i < n_lat_tiles, i % tiles_per_seq, tiles_per_seq), 0)

    h_args, h_specs = _stream_specs(h_src, tm, n_lat_tiles * tm, n_all // tm)
    u = pl.pallas_call(
        functools.partial(_norm_kernel, n_lat_tiles=n_lat_tiles),
        grid=(n_all // tm,),
        in_specs=h_specs + [
            pl.BlockSpec((1, N_MOD, d), mod_idx),
            pl.BlockSpec((1, d), lambda i: (0, 0)),
        ],
        out_specs=pl.BlockSpec((tm, d), lambda i: (i, 0)),
        out_shape=jax.ShapeDtypeStruct((n_all, d), BF16),
        compiler_params=_cparams("parallel"),
        name="norm_modulate",
    )(*h_args, mod, g1)
    qk = pl.pallas_call(
        _inproj_qk_kernel,
        grid=(n_all // tm, QK_COLS // QK_TN),
        in_specs=[
            pl.BlockSpec((tm, d), lambda i, j: (i, 0)),
            pl.BlockSpec((d, QK_TN), lambda i, j: (0, j)),
            pl.BlockSpec((1, QK_TN), lambda i, j: (0, j)),
            pl.BlockSpec((tm, HEAD_DIM), rope_idx),
            pl.BlockSpec((tm, HEAD_DIM), rope_idx),
        ],
        out_specs=pl.BlockSpec((tm, QK_TN), lambda i, j: (i, j)),
        out_shape=jax.ShapeDtypeStruct((n_all, QK_COLS), BF16),
        compiler_params=_cparams("parallel", "arbitrary"),
        name="in_projection_qk",
    )(u, w_qk, gcol, cos, sin)
    vb = pl.pallas_call(
        _inproj_vb_kernel,
        grid=(n_all // tm, VB_COLS // VB_TN),
        in_specs=[
            pl.BlockSpec((tm, d), lambda i, j: (i, 0)),
            pl.BlockSpec((d, VB_TN), lambda i, j: (0, j)),
        ],
        out_specs=pl.BlockSpec((tm, VB_TN), lambda i, j: (i, j)),
        out_shape=jax.ShapeDtypeStruct((n_all, VB_COLS), BF16),
        compiler_params=_cparams("parallel", "arbitrary"),
        name="in_projection_vb",
    )(u, w_vb)
    return qk, vb, u


ATT_KC = 256
SAFE_SHIFT = 40.0


def _key_chunks(k_refs, v_refs):
    chunks = []
    for k_ref, v_ref in zip(k_refs, v_refs):
        n = k_ref.shape[0]
        for s in range(0, n, ATT_KC):
            chunks.append((k_ref, v_ref, s, min(ATT_KC, n - s)))
    return chunks


def _fill_shifts(bound_ref, m_scr, q_ref, chunks, heads):
    bound = bound_ref[0]

    @pl.when(bound <= SAFE_SHIFT)
    def _():
        m_scr[...] = jnp.full(m_scr.shape, bound, F32)

    @pl.when(bound > SAFE_SHIFT)
    def _():
        for idx, (qcol, kcol) in enumerate(heads):
            q = q_ref[:, qcol:qcol + HEAD_DIM]
            m = None
            for k_ref, _, s, n in chunks:
                part = jnp.max(_dot_nt(q, k_ref[s:s + n, kcol:kcol + HEAD_DIM]), axis=-1, keepdims=True)
                m = part if m is None else jnp.maximum(m, part)
            m_scr[idx] = jnp.broadcast_to(m, m_scr.shape[1:])


def _stream_softmax_pv(q, shift, chunks, kcol, vcols):
    o = None
    l = None
    for k_ref, v_ref, s, n in chunks:
        sc = _dot_nt(q, k_ref[s:s + n, kcol:kcol + HEAD_DIM])
        tiles = [jnp.exp(sc[:, t * LANES:(t + 1) * LANES] - shift) for t in range(n // LANES)]
        for e_t in tiles:
            l = e_t if l is None else l + e_t
        e = tiles[0] if len(tiles) == 1 else jnp.concatenate(tiles, axis=1)
        pv = _dot(e.astype(BF16), v_ref[s:s + n, vcols])
        o = pv if o is None else o + pv
    return o, jnp.sum(l, axis=-1, keepdims=True)


def _diff_attn_kernel(bound_ref, dl_ref, sg_ref, q_ref, *refs, lambda_init, n_pieces):
    k_refs = refs[0:2 * n_pieces:2]
    v_refs = refs[1:2 * n_pieces:2]
    o_ref = refs[2 * n_pieces]
    m_scr = refs[2 * n_pieces + 1]
    chunks = _key_chunks(k_refs, v_refs)
    heads = [(s * HEAD_DIM, s * HEAD_DIM) for s in range(2 * DIFF_HEADS)]
    _fill_shifts(bound_ref, m_scr, q_ref, chunks, heads)
    dl = dl_ref[...]
    lam = (jnp.exp(jnp.sum(dl[0:1] * dl[1:2], axis=-1, keepdims=True))
           - jnp.exp(jnp.sum(dl[2:3] * dl[3:4], axis=-1, keepdims=True)) + lambda_init)
    for h in range(DIFF_HEADS):
        c1 = 2 * h * HEAD_DIM
        c2 = c1 + HEAD_DIM
        vs = slice(h * DIFF_V_DIM, (h + 1) * DIFF_V_DIM)
        o1, l1 = _stream_softmax_pv(q_ref[:, c1:c1 + HEAD_DIM], m_scr[2 * h], chunks, c1, vs)
        o2, l2 = _stream_softmax_pv(q_ref[:, c2:c2 + HEAD_DIM], m_scr[2 * h + 1], chunks, c2, vs)
        o = o1 * (1.0 / l1) - o2 * (lam / l2)
        ms = jnp.mean(o * o, axis=-1, keepdims=True)
        o = o * lax.rsqrt(ms + EPS) * sg_ref[...] * (1.0 - lambda_init)
        o_ref[:, vs] = o.astype(BF16)


def _gqa_kernel(bound_ref, q_ref, *refs, n_pieces):
    k_refs = refs[0:2 * n_pieces:2]
    v_refs = refs[1:2 * n_pieces:2]
    o_ref = refs[2 * n_pieces]
    m_scr = refs[2 * n_pieces + 1]
    chunks = _key_chunks(k_refs, v_refs)
    group = GQA_HEADS // GQA_KV_HEADS
    heads = [(h * HEAD_DIM, (h // group) * HEAD_DIM) for h in range(GQA_HEADS)]
    _fill_shifts(bound_ref, m_scr, q_ref, chunks, heads)
    for h, (qcol, kcol) in enumerate(heads):
        o, l = _stream_softmax_pv(q_ref[:, qcol:qcol + HEAD_DIM], m_scr[h], chunks, kcol,
                                  slice(kcol, kcol + HEAD_DIM))
        o_ref[:, qcol:qcol + HEAD_DIM] = (o * (1.0 / l)).astype(BF16)


def _attention(kind, qk, vb, bound, extra, *, batch, seq, ctx, tq, latent, lambda_init=None):
    n_lat = batch * seq
    ctx_blk0 = n_lat // ctx
    if kind == "diff":
        qcol, kcol, vcol, kvw, v_arr = QK_AQ // 1024, QK_AK // 1024, VB_AV // 1024, 1024, vb
    else:
        qcol, kcol, vcol, kvw, v_arr = QK_CQ // 1024, QK_CK // 256, QK_CV // 256, 256, qk
    if latent:
        q_tiles = seq // tq
        q_spec = pl.BlockSpec((tq, 1024), lambda b, i: (b * q_tiles + i, qcol))
        o_spec = pl.BlockSpec((tq, 1024), lambda b, i: (b * q_tiles + i, 0))
        n_q = n_lat
    else:
        q_tiles = 1
        tq = ctx
        q_spec = pl.BlockSpec((ctx, 1024), lambda b, i: (ctx_blk0 + b, qcol))
        o_spec = pl.BlockSpec((ctx, 1024), lambda b, i: (b, 0))
        n_q = batch * ctx
    kv_specs = [
        pl.BlockSpec((ctx, kvw), lambda b, i: (ctx_blk0 + b, kcol)),
        pl.BlockSpec((ctx, kvw), lambda b, i: (ctx_blk0 + b, vcol)),
    ]
    kv_args = [qk, v_arr]
    if latent:
        kv_specs += [
            pl.BlockSpec((seq, kvw), lambda b, i: (b, kcol)),
            pl.BlockSpec((seq, kvw), lambda b, i: (b, vcol)),
        ]
        kv_args += [qk, v_arr]
    n_pieces = len(kv_args) // 2
    pre_specs = [pl.BlockSpec(memory_space=pltpu.SMEM)]
    pre_args = [bound]
    if kind == "diff":
        diff_lambda, sub_g = extra
        body = functools.partial(_diff_attn_kernel, lambda_init=lambda_init, n_pieces=n_pieces)
        pre_specs += [
            pl.BlockSpec((4, HEAD_DIM), lambda b, i: (0, 0)),
            pl.BlockSpec((1, DIFF_V_DIM), lambda b, i: (0, 0)),
        ]
        pre_args += [diff_lambda, sub_g]
        n_softmax = 2 * DIFF_HEADS
    else:
        body = functools.partial(_gqa_kernel, n_pieces=n_pieces)
        n_softmax = GQA_HEADS
    return pl.pallas_call(
        body,
        grid=(batch, q_tiles),
        in_specs=pre_specs + [q_spec] + kv_specs,
        out_specs=o_spec,
        out_shape=jax.ShapeDtypeStruct((n_q, 1024), BF16),
        scratch_shapes=[pltpu.VMEM((n_softmax, tq, LANES), F32)],
        compiler_params=_cparams("parallel", "arbitrary"),
        name=f"{kind}_attn_{'lat' if latent else 'ctx'}",
    )(*pre_args, qk, *kv_args)


LRU_CW = 512
LRU_PAD = 8


def _lru_kernel(bxc_ref, byc_ref, bxl_ref, byl_ref, cw_ref, cb_ref, wa_ref, ba_ref, wx_ref, bx_ref, lam_ref,
                *refs, ctx, seq, need_ctx):
    if need_ctx:
        yc_ref, yl_ref = refs[0], refs[1]
        scratch = refs[2:]
    else:
        yc_ref, yl_ref = None, refs[0]
        scratch = refs[1:]
    xpad, a_f, b_f, a_b, b_b = scratch
    a_scr = (a_f, a_b)
    b_scr = (b_f, b_b)
    lam = lam_ref[...]
    neg_sp = -LRU_C * jax.nn.softplus(-lam)

    def gates(x_ref, n):
        zeros = jnp.zeros((LRU_PAD, LRU_CW), F32)
        xpad[0:LRU_PAD, :] = zeros
        xpad[LRU_PAD:LRU_PAD + n, :] = x_ref[...].astype(F32)
        xpad[LRU_PAD + n:2 * LRU_PAD + n, :] = zeros
        xc = cb_ref[...] + cw_ref[0:1, :] * xpad[LRU_PAD - 2:LRU_PAD - 2 + n, :]
        for k in range(1, CONV_W):
            xc = xc + cw_ref[k:k + 1, :] * xpad[LRU_PAD - 2 + k:LRU_PAD - 2 + k + n, :]
        xcb = xc.astype(BF16)
        for d in range(2):
            for blk in range(LRU_CW // LRU_BLOCK_W):
                sl = slice(blk * LRU_BLOCK_W, (blk + 1) * LRU_BLOCK_W)
                xs = xcb[:, sl]
                r = _sigmoid(_dot(xs, wa_ref[d, blk].astype(BF16)) + ba_ref[d:d + 1, sl])
                gi = _sigmoid(_dot(xs, wx_ref[d, blk].astype(BF16)) + bx_ref[d:d + 1, sl])
                log_a = r * neg_sp[d:d + 1, sl]
                a = jnp.exp(log_a)
                mult = jnp.sqrt(1.0 - a * a)
                a_scr[d][0:n, sl] = a
                b_scr[d][0:n, sl] = mult * gi * xc[:, sl]

    def scan(n, hf0, hb0):
        def body(t, carry):
            hf, hb = carry
            tb = n - 1 - t
            hf = a_f[pl.ds(t, 1), :] * hf + b_f[pl.ds(t, 1), :]
            b_f[pl.ds(t, 1), :] = hf
            hb = a_b[pl.ds(tb, 1), :] * hb + b_b[pl.ds(tb, 1), :]
            b_b[pl.ds(tb, 1), :] = hb
            return hf, hb
        return lax.fori_loop(0, n, body, (hf0, hb0), unroll=8)

    zero = jnp.zeros((1, LRU_CW), F32)
    gates(bxc_ref, ctx)
    hf, hb = scan(ctx, zero, zero)
    if need_ctx:
        yc_ref[...] = ((b_f[0:ctx, :] + b_b[0:ctx, :])
                       * jax.nn.gelu(byc_ref[...].astype(F32), approximate=True)).astype(BF16)
    gates(bxl_ref, seq)
    scan(seq, hf, hb)
    yl_ref[...] = ((b_f[0:seq, :] + b_b[0:seq, :])
                   * jax.nn.gelu(byl_ref[...].astype(F32), approximate=True)).astype(BF16)


def _rglru(vb, conv_w, conv_b, w_a, b_a, w_x, b_x, lam, *, batch, seq, ctx, need_ctx):
    n_lat = batch * seq
    ctx_blk0 = n_lat // ctx
    cbx, cby = VB_BX // LRU_CW, VB_BY // LRU_CW
    nblk = LRU_CW // LRU_BLOCK_W
    tmax = max(seq, ctx)
    out_specs = [pl.BlockSpec((seq, LRU_CW), lambda b, c: (b, c))]
    out_shape = [jax.ShapeDtypeStruct((n_lat, LRU_WIDTH), BF16)]
    if need_ctx:
        out_specs = [pl.BlockSpec((ctx, LRU_CW), lambda b, c: (b, c))] + out_specs
        out_shape = [jax.ShapeDtypeStruct((batch * ctx, LRU_WIDTH), BF16)] + out_shape
    outs = pl.pallas_call(
        functools.partial(_lru_kernel, ctx=ctx, seq=seq, need_ctx=need_ctx),
        grid=(batch, LRU_WIDTH // LRU_CW),
        in_specs=[
            pl.BlockSpec((ctx, LRU_CW), lambda b, c: (ctx_blk0 + b, cbx + c)),
            pl.BlockSpec((ctx, LRU_CW), lambda b, c: (ctx_blk0 + b, cby + c)),
            pl.BlockSpec((seq, LRU_CW), lambda b, c: (b, cbx + c)),
            pl.BlockSpec((seq, LRU_CW), lambda b, c: (b, cby + c)),
            pl.BlockSpec((CONV_W, LRU_CW), lambda b, c: (0, c)),
            pl.BlockSpec((1, LRU_CW), lambda b, c: (0, c)),
            pl.BlockSpec((2, nblk, LRU_BLOCK_W, LRU_BLOCK_W), lambda b, c: (0, c, 0, 0)),
            pl.BlockSpec((2, LRU_CW), lambda b, c: (0, c)),
            pl.BlockSpec((2, nblk, LRU_BLOCK_W, LRU_BLOCK_W), lambda b, c: (0, c, 0, 0)),
            pl.BlockSpec((2, LRU_CW), lambda b, c: (0, c)),
            pl.BlockSpec((2, LRU_CW), lambda b, c: (0, c)),
        ],
        out_specs=out_specs,
        out_shape=out_shape,
        scratch_shapes=[
            pltpu.VMEM((tmax + 2 * LRU_PAD, LRU_CW), F32),
            pltpu.VMEM((tmax, LRU_CW), F32),
            pltpu.VMEM((tmax, LRU_CW), F32),
            pltpu.VMEM((tmax, LRU_CW), F32),
            pltpu.VMEM((tmax, LRU_CW), F32),
        ],
        compiler_params=_cparams("parallel", "arbitrary"),
        name="rglru",
    )(vb, vb, vb, vb, conv_w, conv_b, w_a, b_a, w_x, b_x, lam)
    if need_ctx:
        return outs[1], outs[0]
    return outs[0], None


MERGE_TN = 256


def _merge_kernel(u_ref, *refs, n_lat_tiles, has_ctx):
    n_y = 6 if has_ctx else 3
    y_refs = refs[:n_y]
    wg_refs = refs[n_y:n_y + 3]
    bg_ref = refs[n_y + 3]
    wbr_refs = refs[n_y + 4:n_y + 7]
    m_ref = refs[n_y + 7]

    def body(ys):
        u = u_ref[...]
        m = None
        for k in range(N_BRANCH):
            g = jax.nn.sigmoid(_dot(u, wg_refs[k][...]) + bg_ref[k:k + 1, :])
            part = g * _dot(ys[k][...], wbr_refs[k][...])
            m = part if m is None else m + part
        m_ref[...] = m.astype(BF16)

    if not has_ctx:
        body(y_refs)
        return
    is_lat = pl.program_id(0) < n_lat_tiles

    @pl.when(is_lat)
    def _():
        body(y_refs[0:3])

    @pl.when(jnp.logical_not(is_lat))
    def _():
        body(y_refs[3:6])


def _merge(u, y_lat, y_ctx, w_gate, b_gate, w_branch, *, n_tok, n_lat, tm):
    d = u.shape[1]
    nj = d // MERGE_TN
    nl = n_lat // tm
    has_ctx = y_ctx is not None
    y_specs = [pl.BlockSpec((tm, BRANCH_W), lambda i, j: (jnp.minimum(i, nl - 1), 0))] * 3
    y_args = list(y_lat)
    if has_ctx:
        y_specs += [pl.BlockSpec((tm, BRANCH_W), lambda i, j: (jnp.maximum(i - nl, 0), 0))] * 3
        y_args += list(y_ctx)
    wbr_spec = pl.BlockSpec((BRANCH_W, MERGE_TN), lambda i, j: (0, j))
    return pl.pallas_call(
        functools.partial(_merge_kernel, n_lat_tiles=nl, has_ctx=has_ctx),
        grid=(n_tok // tm, nj),
        in_specs=[pl.BlockSpec((tm, d), lambda i, j: (i, 0))] + y_specs + [
            pl.BlockSpec((d, MERGE_TN), lambda i, j: (0, j)),
            pl.BlockSpec((d, MERGE_TN), lambda i, j: (0, nj + j)),
            pl.BlockSpec((d, MERGE_TN), lambda i, j: (0, 2 * nj + j)),
            pl.BlockSpec((N_BRANCH, MERGE_TN), lambda i, j: (0, j)),
            wbr_spec, wbr_spec, wbr_spec,
        ],
        out_specs=pl.BlockSpec((tm, MERGE_TN), lambda i, j: (i, j)),
        out_shape=jax.ShapeDtypeStruct((n_tok, d), BF16),
        compiler_params=_cparams("parallel", "arbitrary"),
        name="branch_merge",
    )(u, *y_args, w_gate, w_gate, w_gate, b_gate, *w_branch)


OUT_RC = 256
ROUTE_E, ROUTE_P, ROUTE_RANK = 0, 2, 4
COUNT_ROWS = 8


def _outproj_kernel(m_ref, *refs, n_lat_tiles):
    mod_ref, g_ref, wo_ref, wr_ref, br_ref, h1_ref, vp_ref, route_ref, cnt_ref, v_scr = refs[-10:]
    h_refs = refs[:-10]
    tm, d = h1_ref.shape
    rc = min(OUT_RC, tm)
    for r in range(tm // rc):
        rows = slice(r * rc, (r + 1) * rc)
        h1 = (_read_stream(h_refs, rows, n_lat_tiles)
              + mod_ref[0, 2:3, :] * _dot(m_ref[rows, :], wo_ref[...]))
        h1_ref[rows, :] = h1
        ms = jnp.mean(h1 * h1, axis=-1, keepdims=True)
        y = h1 * lax.rsqrt(ms + EPS) * g_ref[...]
        v_scr[rows, :] = (y * (1.0 + mod_ref[0, 4:5, :]) + mod_ref[0, 3:4, :]).astype(BF16)
    v = v_scr[...]
    half = d // 2
    lo = lax.bitcast_convert_type(v[:, :half].astype(F32), U32)
    hi = lax.bitcast_convert_type(v[:, half:].astype(F32), U32)
    vp_ref[...] = (lo >> 16) | (hi & jnp.uint32(0xFFFF0000))

    logits = _dot(v, wr_ref[...]) + br_ref[...]
    lane = lax.broadcasted_iota(I32, logits.shape, 1).astype(F32)
    neg = jnp.float32(-jnp.inf)
    big = jnp.float32(1e9)

    def masked_argmax(mask):
        val = jnp.max(jnp.where(mask, logits, neg), axis=-1, keepdims=True)
        idx = jnp.min(jnp.where(jnp.logical_and(mask, logits == val), lane, big), axis=-1, keepdims=True)
        return val, idx

    gmask = lane < N_GROUPS
    gmax, gidx = masked_argmax(gmask)
    gsum = jnp.sum(jnp.where(gmask, jnp.exp(logits - gmax), 0.0), axis=-1, keepdims=True)
    g_w = 1.0 / gsum
    lo_lane = N_GROUPS + EXPERTS_PER_GROUP * gidx
    emask = jnp.logical_and(lane >= lo_lane, lane < lo_lane + EXPERTS_PER_GROUP)
    v1, i1 = masked_argmax(emask)
    v2, i2 = masked_argmax(jnp.logical_and(emask, lane != i1))
    e21 = jnp.exp(v2 - v1)
    p1 = g_w / (1.0 + e21)
    p2 = g_w * e21 / (1.0 + e21)
    e1 = i1 - N_GROUPS
    e2 = i2 - N_GROUPS
    oh1 = jnp.where(lane == e1, 1.0, 0.0)
    oh2 = jnp.where(lane == e2, 1.0, 0.0)
    cnt = oh1 + oh2
    row = lax.broadcasted_iota(I32, (tm, tm), 0)
    col = lax.broadcasted_iota(I32, (tm, tm), 1)
    before = jnp.where(row > col, 1.0, 0.0).astype(BF16)
    prefix = _dot(before, cnt.astype(BF16))
    rank1 = jnp.sum(prefix * oh1, axis=-1, keepdims=True)
    rank2 = jnp.sum(prefix * oh2, axis=-1, keepdims=True)
    route = jnp.where(lane == 0, e1, jnp.where(lane == 1, e2, jnp.where(lane == 2, p1, jnp.where(
        lane == 3, p2, jnp.where(lane == 4, rank1, jnp.where(lane == 5, rank2, 0.0))))))
    route_ref[...] = route
    cnt_ref[0] = jnp.broadcast_to(jnp.sum(cnt, axis=0, keepdims=True), (COUNT_ROWS, LANES))


def _out_projection(m, h_src, mod, g2, w_out, w_router, b_router, *, n_tok, tm, n_lat, seq, batch):
    d = m.shape[1]
    n_tiles = n_tok // tm

    def mod_idx(i):
        return (jnp.where(i * tm < n_lat, (i * tm) // seq, batch), 0, 0)

    h_args, h_specs = _stream_specs(h_src, tm, n_lat, n_tiles)
    return pl.pallas_call(
        functools.partial(_outproj_kernel, n_lat_tiles=n_lat // tm),
        grid=(n_tiles,),
        in_specs=[pl.BlockSpec((tm, d), lambda i: (i, 0))] + h_specs + [
            pl.BlockSpec((1, N_MOD, d), mod_idx),
            pl.BlockSpec((1, d), lambda i: (0, 0)),
            pl.BlockSpec((d, d), lambda i: (0, 0), pipeline_mode=pl.Buffered(1)),
            pl.BlockSpec((d, LANES), lambda i: (0, 0)),
            pl.BlockSpec((1, LANES), lambda i: (0, 0)),
        ],
        out_specs=[
            pl.BlockSpec((tm, d), lambda i: (i, 0)),
            pl.BlockSpec((tm, d // 2), lambda i: (i, 0)),
            pl.BlockSpec((tm, LANES), lambda i: (i, 0)),
            pl.BlockSpec((1, COUNT_ROWS, LANES), lambda i: (i, 0, 0)),
        ],
        out_shape=[
            jax.ShapeDtypeStruct((n_tok, d), F32),
            jax.ShapeDtypeStruct((n_tok, d // 2), U32),
            jax.ShapeDtypeStruct((n_tok, LANES), F32),
            jax.ShapeDtypeStruct((n_tiles, COUNT_ROWS, LANES), F32),
        ],
        scratch_shapes=[pltpu.VMEM((tm, d), BF16)],
        compiler_params=_cparams("parallel"),
        name="out_projection_router",
    )(m, *h_args, mod, g2, w_out, w_router, b_router)


MOE_TM = 256
ROW_TM = 256


def _dispatch(route, counts, n_tok, tm_route):
    n_tiles = (2 * n_tok) // MOE_TM + N_EXPERTS
    counts = counts[:, 0, :N_EXPERTS].astype(I32)
    tile_base = jnp.cumsum(counts, axis=0) - counts
    total = jnp.sum(counts, axis=0)
    padded = ((total + MOE_TM - 1) // MOE_TM) * MOE_TM
    seg_end = jnp.cumsum(padded)
    base = (seg_end - padded)[None, :] + tile_base
    r3 = route.reshape(n_tok // tm_route, tm_route, LANES)
    experts = jnp.arange(N_EXPERTS, dtype=I32)
    dest = []
    for k in range(2):
        hit = r3[:, :, ROUTE_E + k].astype(I32)[:, :, None] == experts
        picked = jnp.sum(jnp.where(hit, base[:, None, :], 0), axis=-1)
        dest.append((picked + r3[:, :, ROUTE_RANK + k].astype(I32)).reshape(n_tok // ROW_TM, 1, ROW_TM))
    tile_start = jnp.arange(n_tiles, dtype=I32) * MOE_TM
    n_used = seg_end[-1] // MOE_TM
    tile_valid = (tile_start < seg_end[-1]).astype(I32)
    tile_index = jnp.minimum(jnp.arange(n_tiles, dtype=I32), n_used - 1)
    tile_expert = jnp.sum((seg_end[None, :] <= (tile_index * MOE_TM)[:, None]).astype(I32), axis=1)
    prev = jnp.concatenate([jnp.full((1,), -1, I32), tile_expert[:-1]])
    tile_first = (tile_expert != prev).astype(I32)
    tails = jnp.maximum(seg_end - MOE_TM, 0).astype(I32)
    return dest, tile_expert, tile_first, tile_valid, tile_index, tails


def _row_scatter_kernel(tail_ref, d0_ref, d1_ref, v_ref, x_hbm, zbuf, sem_z, sem):
    @pl.when(pl.program_id(0) == 0)
    def _():
        zbuf[...] = jnp.zeros_like(zbuf)
        def zero_tile(start):
            start = pl.multiple_of(start, MOE_TM)
            return pltpu.make_async_copy(zbuf, x_hbm.at[pl.ds(start, MOE_TM), :], sem_z)
        for e in range(N_EXPERTS):
            zero_tile(tail_ref[e]).start()
        for e in range(N_EXPERTS):
            zero_tile(tail_ref[e]).wait()
        first_unused = tail_ref[N_EXPERTS - 1] // MOE_TM + 1
        n_tiles = x_hbm.shape[0] // MOE_TM

        def start_unused(t, carry):
            zero_tile(t * MOE_TM).start()
            return carry

        def wait_unused(t, carry):
            zero_tile(t * MOE_TM).wait()
            return carry
        lax.fori_loop(first_unused, n_tiles, start_unused, 0)
        lax.fori_loop(first_unused, n_tiles, wait_unused, 0)

    def row_copies(g, k, d0, d1):
        src = v_ref.at[pl.ds(pl.multiple_of(g * 8, 8) + k, 1), :]
        return (pltpu.make_async_copy(src, x_hbm.at[pl.ds(d0, 1), :], sem),
                pltpu.make_async_copy(src, x_hbm.at[pl.ds(d1, 1), :], sem))

    def issue(g, carry):
        for k in range(8):
            for copy in row_copies(g, k, d0_ref[0, 0, g * 8 + k], d1_ref[0, 0, g * 8 + k]):
                copy.start()
        return carry
    lax.fori_loop(0, ROW_TM // 8, issue, 0)

    def drain(g, carry):
        for k in range(8):
            for copy in row_copies(g, k, 0, 0):
                copy.wait()
        return carry
    lax.fori_loop(0, ROW_TM // 8, drain, 0)


def _row_scatter(vp, dest, tails, n_rows):
    n_tok, half = vp.shape
    n_steps = n_tok // ROW_TM
    d0, d1 = dest
    grid_spec = pltpu.PrefetchScalarGridSpec(
        num_scalar_prefetch=1,
        grid=(n_steps,),
        in_specs=[
            pl.BlockSpec((1, 1, ROW_TM), lambda i, t: (i, 0, 0), memory_space=pltpu.SMEM),
            pl.BlockSpec((1, 1, ROW_TM), lambda i, t: (i, 0, 0), memory_space=pltpu.SMEM),
            pl.BlockSpec((ROW_TM, half), lambda i, t: (i, 0)),
        ],
        out_specs=pl.BlockSpec(memory_space=pl.ANY),
        scratch_shapes=[
            pltpu.VMEM((MOE_TM, half), U32),
            pltpu.SemaphoreType.DMA(()),
            pltpu.SemaphoreType.DMA(()),
        ],
    )
    return pl.pallas_call(
        _row_scatter_kernel,
        grid_spec=grid_spec,
        out_shape=jax.ShapeDtypeStruct((n_rows, half), U32),
        compiler_params=_cparams("arbitrary"),
        name="moe_row_scatter",
    )(tails, d0, d1, vp)


def _moe_kernel(te_ref, tf_ref, tv_ref, ti_ref, x_ref, w1_ref, w3_ref, w2_ref, y_ref, w1b, w3b, w2b):
    i = pl.program_id(0)
    valid = tv_ref[i] == 1

    @pl.when(valid)
    def _():
        @pl.when(tf_ref[i] == 1)
        def _():
            w1b[...] = w1_ref[0, 0].astype(BF16)
            w3b[...] = w3_ref[0, 0].astype(BF16)
            w2b[...] = w2_ref[0, 0].astype(BF16)

        half = x_ref.shape[1]
        xp = x_ref[...]
        lo = lax.bitcast_convert_type(xp << 16, F32).astype(BF16)
        hi = lax.bitcast_convert_type(xp & jnp.uint32(0xFFFF0000), F32).astype(BF16)
        h1 = _dot(lo, w1b[0:half, :]) + _dot(hi, w1b[half:, :])
        h3 = _dot(lo, w3b[0:half, :]) + _dot(hi, w3b[half:, :])
        hh = (h1 * jax.nn.sigmoid(h1) * h3).astype(BF16)
        y_ref[...] = _dot(hh, w2b[...])

    @pl.when(jnp.logical_not(valid))
    def _():
        y_ref[...] = jnp.zeros_like(y_ref)


def _moe_experts(layer, x_sorted, tile_expert, tile_first, tile_valid, tile_index, w1, w3, w2):
    n_tiles = tile_expert.shape[0]
    half = x_sorted.shape[1]
    d = 2 * half
    grid_spec = pltpu.PrefetchScalarGridSpec(
        num_scalar_prefetch=4,
        grid=(n_tiles,),
        in_specs=[
            pl.BlockSpec((MOE_TM, half), lambda i, te, tf, tv, ti: (ti[i], 0)),
            pl.BlockSpec((1, 1, d, D_EXPERT), lambda i, te, tf, tv, ti: (layer, te[i], 0, 0)),
            pl.BlockSpec((1, 1, d, D_EXPERT), lambda i, te, tf, tv, ti: (layer, te[i], 0, 0)),
            pl.BlockSpec((1, 1, D_EXPERT, d), lambda i, te, tf, tv, ti: (layer, te[i], 0, 0)),
        ],
        out_specs=pl.BlockSpec((MOE_TM, d), lambda i, te, tf, tv, ti: (i, 0)),
        scratch_shapes=[
            pltpu.VMEM((d, D_EXPERT), BF16),
            pltpu.VMEM((d, D_EXPERT), BF16),
            pltpu.VMEM((D_EXPERT, d), BF16),
        ],
    )
    return pl.pallas_call(
        _moe_kernel,
        grid_spec=grid_spec,
        out_shape=jax.ShapeDtypeStruct((n_tiles * MOE_TM, d), F32),
        compiler_params=_cparams("arbitrary"),
        name="moe_experts",
    )(tile_expert, tile_first, tile_valid, tile_index, x_sorted, w1, w3, w2)


def _combine_kernel(d0_ref, d1_ref, h_ref, route_ref, mod_ref, y_hbm, o_ref, ybuf, sem):
    def row_copies(g, k, d0, d1):
        rows = pl.ds(pl.multiple_of(g * 8, 8) + k, 1)
        return (pltpu.make_async_copy(y_hbm.at[pl.ds(d0, 1), :], ybuf.at[0, rows, :], sem),
                pltpu.make_async_copy(y_hbm.at[pl.ds(d1, 1), :], ybuf.at[1, rows, :], sem))

    def issue(g, carry):
        for k in range(8):
            for copy in row_copies(g, k, d0_ref[0, 0, g * 8 + k], d1_ref[0, 0, g * 8 + k]):
                copy.start()
        return carry
    lax.fori_loop(0, ROW_TM // 8, issue, 0)

    def drain(g, carry):
        for k in range(8):
            for copy in row_copies(g, k, 0, 0):
                copy.wait()
        return carry
    lax.fori_loop(0, ROW_TM // 8, drain, 0)
    p0 = route_ref[:, ROUTE_P:ROUTE_P + 1]
    p1 = route_ref[:, ROUTE_P + 1:ROUTE_P + 2]
    o_ref[...] = h_ref[...] + mod_ref[0, 5:6, :] * (p0 * ybuf[0] + p1 * ybuf[1])


def _combine(h1, route, mod, y_sorted, dest, *, n_tok, n_lat, seq, batch):
    d = h1.shape[1]
    tm = ROW_TM
    n_steps = n_tok // tm
    d0, d1 = dest

    def mod_idx(i):
        return (jnp.where(i * tm < n_lat, (i * tm) // seq, batch), 0, 0)

    return pl.pallas_call(
        _combine_kernel,
        grid=(n_steps,),
        in_specs=[
            pl.BlockSpec((1, 1, tm), lambda i: (i, 0, 0), memory_space=pltpu.SMEM),
            pl.BlockSpec((1, 1, tm), lambda i: (i, 0, 0), memory_space=pltpu.SMEM),
            pl.BlockSpec((tm, d), lambda i: (i, 0)),
            pl.BlockSpec((tm, LANES), lambda i: (i, 0)),
            pl.BlockSpec((1, N_MOD, d), mod_idx),
            pl.BlockSpec(memory_space=pl.ANY),
        ],
        out_specs=pl.BlockSpec((tm, d), lambda i: (i, 0)),
        out_shape=jax.ShapeDtypeStruct((n_tok, d), F32),
        scratch_shapes=[pltpu.VMEM((2, tm, d), F32), pltpu.SemaphoreType.DMA(())],
        compiler_params=_cparams("arbitrary"),
        name="moe_combine",
    )(d0, d1, h1, route, mod, y_sorted)


def _rope_tables(seq, tm):
    n_rows = seq // GRID_W
    row = jnp.repeat(jnp.arange(n_rows), GRID_W).astype(F32)
    col = jnp.tile(jnp.arange(GRID_W), n_rows).astype(F32)
    half = HEAD_DIM // 2
    inv = 1.0 / (ROPE_THETA ** (jnp.arange(0, half, 2, dtype=F32) / half))
    ang_r = row[:, None] * inv
    ang_c = col[:, None] * inv
    ang = jnp.concatenate([ang_r, ang_c, ang_r, ang_c], axis=-1)
    cos, sin = jnp.cos(ang), jnp.sin(ang)
    sin_signed = jnp.where(jnp.arange(HEAD_DIM) < half, -sin, sin)
    ident = jnp.zeros((tm, HEAD_DIM), F32)
    return jnp.concatenate([cos, ident + 1.0]), jnp.concatenate([sin_signed, ident])


def _permute_heads(a):
    lead = a.shape[:-1]
    quarter = HEAD_DIM // 4
    assert HEAD_PERM[quarter] == 2 * quarter and HEAD_PERM[2 * quarter] == quarter
    q = a.reshape(*lead, a.shape[-1] // HEAD_DIM, 4, quarter)
    q = jnp.stack([q[..., 0, :], q[..., 2, :], q[..., 1, :], q[..., 3, :]], axis=-2)
    return q.reshape(*lead, a.shape[-1])


def _score_bound(gq, gk):
    return (1.02 * HEAD_DIM * jnp.max(jnp.abs(gq)) * jnp.max(jnp.abs(gk))).reshape(1).astype(F32)


def kernel(x, c, ctx, c_ctx, w_ada, b_ada, norm1_g, norm2_g, w_in, b_gate, q_norm_a, k_norm_a, diff_lambda,
           sub_norm_a, q_norm_c, k_norm_c, conv_w, conv_b, lru_w_a, lru_b_a, lru_w_x, lru_b_x, lru_lambda,
           w_branch_a, w_branch_b, w_branch_c, w_out, w_group, b_group, w_route, b_route, w1, w3, w2):
    batch, seq, d = x.shape
    ctx_len = ctx.shape[1]
    depth = w_ada.shape[0]
    n_lat = batch * seq
    n_ctx = batch * ctx_len
    n_all = n_lat + n_ctx
    tm = min(1024, seq, n_ctx)
    assert seq % tm == 0 and n_ctx % tm == 0 and n_lat % ctx_len == 0 and batch < MOD_ROWS
    assert seq % GRID_W == 0 and seq % ROW_TM == 0 and n_ctx % ROW_TM == 0
    tq = min(256, seq)
    tm_out = min(512, tm)

    cc = jnp.zeros((MOD_ROWS, d), F32).at[:batch].set(c).at[batch].set(c_ctx)
    mod_all = _ada_modulation(cc, w_ada, b_ada).reshape(depth, MOD_ROWS, N_MOD, d)
    rope = _rope_tables(seq, tm)
    h_src = (x.reshape(n_lat, d), ctx.reshape(n_ctx, d))
    scale = HEAD_DIM ** -0.5

    for l in range(depth):
        last = l == depth - 1
        lambda_init = 0.8 - 0.6 * math.exp(-0.3 * l)
        mod = mod_all[l]
        wl = w_in[l]
        w_qk = jnp.concatenate([_permute_heads(wl[:, 0:2048]), _permute_heads(wl[:, 5120:6400]),
                                wl[:, 6400:6656]], axis=1).astype(BF16)
        w_vb = wl[:, 2048:5120].astype(BF16)
        w_gate = wl[:, 6656:].astype(BF16)
        gq_a, gq_c = q_norm_a[l] * scale, q_norm_c[l] * scale
        gcol = jnp.concatenate([
            _permute_heads(jnp.concatenate([
                jnp.tile(gq_a, 2 * DIFF_HEADS), jnp.tile(k_norm_a[l], 2 * DIFF_HEADS),
                jnp.tile(gq_c, GQA_HEADS), jnp.tile(k_norm_c[l], GQA_KV_HEADS)])),
            jnp.ones((QK_COLS - QK_CV,), F32)]).reshape(1, QK_COLS)
        bound_a = _score_bound(gq_a, k_norm_a[l])
        bound_c = _score_bound(gq_c, k_norm_c[l])

        qk, vb, u = _in_projection(h_src, mod, norm1_g[l].reshape(1, d), w_qk, w_vb, gcol, rope, n_all=n_all,
                                   tm=tm, n_lat_tiles=n_lat // tm, tiles_per_seq=seq // tm, batch=batch)
        dims = dict(batch=batch, seq=seq, ctx=ctx_len)
        extra_a = (diff_lambda[l], sub_norm_a[l].reshape(1, DIFF_V_DIM))
        ya = _attention("diff", qk, vb, bound_a, extra_a, tq=tq, latent=True, lambda_init=lambda_init, **dims)
        yc = _attention("gqa", qk, vb, bound_c, None, tq=tq, latent=True, **dims)
        yb, yb_c = _rglru(vb, conv_w[l], conv_b[l].reshape(1, LRU_WIDTH), lru_w_a[l], lru_b_a[l],
                          lru_w_x[l], lru_b_x[l], lru_lambda[l], need_ctx=not last, **dims)
        if last:
            n_tok, y_ctx = n_lat, None
        else:
            n_tok = n_all
            ya_c = _attention("diff", qk, vb, bound_a, extra_a, tq=tq, latent=False, lambda_init=lambda_init,
                              **dims)
            yc_c = _attention("gqa", qk, vb, bound_c, None, tq=tq, latent=False, **dims)
            y_ctx = (ya_c, yb_c, yc_c)
        w_branch = (w_branch_a[l].astype(BF16), w_branch_b[l].astype(BF16), w_branch_c[l].astype(BF16))
        m = _merge(u, (ya, yb, yc), y_ctx, w_gate, b_gate[l], w_branch, n_tok=n_tok, n_lat=n_lat, tm=tm)
        w_router = jnp.zeros((d, LANES), F32).at[:, :N_GROUPS].set(w_group[l])
        w_router = w_router.at[:, N_GROUPS:N_GROUPS + N_EXPERTS].set(w_route[l]).astype(BF16)
        b_router = jnp.zeros((1, LANES), F32).at[0, :N_GROUPS].set(b_group[l])
        b_router = b_router.at[0, N_GROUPS:N_GROUPS + N_EXPERTS].set(b_route[l])
        h1, vp, route, counts = _out_projection(m, h_src, mod, norm2_g[l].reshape(1, d),
                                                w_out[l].astype(BF16), w_router, b_router, n_tok=n_tok,
                                                tm=tm_out, n_lat=n_lat, seq=seq, batch=batch)
        dest, tile_expert, tile_first, tile_valid, tile_index, tails = _dispatch(route, counts, n_tok, tm_out)
        x_sorted = _row_scatter(vp, dest, tails, tile_expert.shape[0] * MOE_TM)
        y_sorted = _moe_experts(l, x_sorted, tile_expert, tile_first, tile_valid, tile_index, w1, w3, w2)
        h_src = _combine(h1, route, mod, y_sorted, dest, n_tok=n_tok, n_lat=n_lat, seq=seq, batch=batch)
    return h_src.reshape(batch, seq, d)
```

```python
import functools
import math

import jax
import jax.numpy as jnp
from jax import lax
from jax.experimental import pallas as pl
from jax.experimental.pallas import tpu as pltpu

F32 = jnp.float32
BF16 = jnp.bfloat16
I32 = jnp.int32
U32 = jnp.uint32

EPS = 1e-6
HEAD_DIM = 128
GRID_W = 64
ROPE_THETA = 10000.0
DIFF_HEADS = 4
DIFF_V_DIM = 2 * HEAD_DIM
LRU_WIDTH = 1024
LRU_BLOCK_W = 128
CONV_W = 4
LRU_C = 8.0
GQA_HEADS = 8
GQA_KV_HEADS = 2
N_BRANCH = 3
BRANCH_W = 1024
N_GROUPS = 4
EXPERTS_PER_GROUP = 8
N_EXPERTS = N_GROUPS * EXPERTS_PER_GROUP
D_EXPERT = 512
N_MOD = 6
MOD_ROWS = 16
LANES = 128

QK_AQ, QK_AK, QK_CQ, QK_CK, QK_CV = 0, 1024, 2048, 3072, 3328
VB_AV, VB_BX, VB_BY = 0, 1024, 2048
QK_COLS = 3584
VB_COLS = 3072
QK_TN = QK_COLS // 2
VB_TN = VB_COLS // 2
PROJ_CW = 256
PROJ_RC = 512
HEAD_PERM = tuple(range(0, 32)) + tuple(range(64, 96)) + tuple(range(32, 64)) + tuple(range(96, 128))

VMEM_LIMIT = 56 * 1024 * 1024


def _cparams(*sem):
    return pltpu.CompilerParams(dimension_semantics=sem, vmem_limit_bytes=VMEM_LIMIT)


def _dot(a, b):
    return jnp.dot(a, b, preferred_element_type=F32)


def _dot_nt(a, b):
    return lax.dot_general(a, b, (((1,), (1,)), ((), ())), preferred_element_type=F32)


def _sigmoid(x):
    return 0.5 * jnp.tanh(0.5 * x) + 0.5


def _lane_mean(x2):
    return _dot(x2.astype(BF16), jnp.full((LANES, LANES), 1.0 / LANES, BF16))


def _ada_kernel(c_ref, w_ref, b_ref, o_ref):
    c = c_ref[...]
    s = (c * jax.nn.sigmoid(c)).astype(BF16)
    o_ref[0] = _dot(s, w_ref[0].astype(BF16)) + b_ref[0]


def _ada_modulation(cc, w_ada, b_ada):
    depth, d, n = w_ada.shape
    tn = 1024
    return pl.pallas_call(
        _ada_kernel,
        grid=(depth, n // tn),
        in_specs=[
            pl.BlockSpec((MOD_ROWS, d), lambda l, j: (0, 0)),
            pl.BlockSpec((1, d, tn), lambda l, j: (l, 0, j)),
            pl.BlockSpec((1, 1, tn), lambda l, j: (l, 0, j)),
        ],
        out_specs=pl.BlockSpec((1, MOD_ROWS, tn), lambda l, j: (l, 0, j)),
        out_shape=jax.ShapeDtypeStruct((depth, MOD_ROWS, n), F32),
        compiler_params=_cparams("parallel", "parallel"),
        name="ada_modulation",
    )(cc, w_ada, b_ada.reshape(depth, 1, n))


NORM_ROWS = 32


def _stream_specs(h_src, tm, n_lat, n_tiles):
    nl = n_lat // tm
    if isinstance(h_src, tuple) and n_tiles > nl:
        d = h_src[0].shape[1]
        return list(h_src), [pl.BlockSpec((tm, d), lambda i: (jnp.minimum(i, nl - 1), 0)),
                             pl.BlockSpec((tm, d), lambda i: (jnp.maximum(i - nl, 0), 0))]
    arr = h_src[0] if isinstance(h_src, tuple) else h_src
    return [arr], [pl.BlockSpec((tm, arr.shape[1]), lambda i: (i, 0))]


def _read_stream(h_refs, rows, n_lat_tiles):
    if len(h_refs) == 1:
        return h_refs[0][rows, :]
    return jnp.where(pl.program_id(0) < n_lat_tiles, h_refs[0][rows, :], h_refs[1][rows, :])


def _norm_modulate_rows(h_refs, n_lat_tiles, g_ref, mod_ref, shift_row, scale_row, out_ref):
    g = g_ref[...]
    sc = 1.0 + mod_ref[0, scale_row:scale_row + 1, :]
    sh = mod_ref[0, shift_row:shift_row + 1, :]

    def body(i, carry):
        rows = pl.ds(pl.multiple_of(i * NORM_ROWS, NORM_ROWS), NORM_ROWS)
        x = _read_stream(h_refs, rows, n_lat_tiles)
        ms = jnp.mean(x * x, axis=-1, keepdims=True)
        y = (x * lax.rsqrt(ms + EPS) * g) * sc + sh
        out_ref[rows, :] = y.astype(out_ref.dtype)
        return carry
    lax.fori_loop(0, out_ref.shape[0] // NORM_ROWS, body, 0)


def _norm_kernel(*refs, n_lat_tiles):
    mod_ref, g_ref, u_ref = refs[-3:]
    _norm_modulate_rows(refs[:-3], n_lat_tiles, g_ref, mod_ref, 0, 1, u_ref)


def _project_tile(u_ref, w_ref, epilogue):
    tm, tn = u_ref.shape[0], w_ref.shape[1]
    rc = min(PROJ_RC, tm)
    pending = []
    for c in range(tn // PROJ_CW):
        cols = slice(c * PROJ_CW, (c + 1) * PROJ_CW)
        accs = []
        for r in range(tm // rc):
            rows = slice(r * rc, (r + 1) * rc)
            accs.append((rows, cols, _dot(u_ref[rows, :], w_ref[:, cols])))
        for item in pending:
            epilogue(*item)
        pending = accs
    for item in pending:
        epilogue(*item)


def _inproj_qk_kernel(u_ref, w_ref, gc_ref, cos_ref, sin_ref, qk_ref):
    last_tile = pl.program_id(1) == pl.num_programs(1) - 1
    tn = w_ref.shape[1]

    def epilogue(rows, cols, acc):
        plain = jnp.logical_and(last_tile, cols.stop == tn)
        for c in range(PROJ_CW // HEAD_DIM):
            sl = slice(cols.start + c * HEAD_DIM, cols.start + (c + 1) * HEAD_DIM)
            x = acc[:, c * HEAD_DIM:(c + 1) * HEAD_DIM]
            y = x * lax.rsqrt(_lane_mean(x * x) + EPS) * gc_ref[:, sl]
            y = y * cos_ref[rows, :] + pltpu.roll(y, HEAD_DIM // 2, 1) * sin_ref[rows, :]
            qk_ref[rows, sl] = jnp.where(plain, x, y).astype(BF16)

    _project_tile(u_ref, w_ref, epilogue)


def _inproj_vb_kernel(u_ref, w_ref, vb_ref):
    def epilogue(rows, cols, acc):
        vb_ref[rows, cols] = acc.astype(BF16)

    _project_tile(u_ref, w_ref, epilogue)


def _in_projection(h_src, mod, g1, w_qk, w_vb, gcol, rope, *, n_all, tm, n_lat_tiles, tiles_per_seq, batch):
    d = g1.shape[1]
    cos, sin = rope

    def mod_idx(i):
        return (jnp.where(i < n_lat_tiles, i // tiles_per_seq, batch), 0, 0)

    def rope_idx(i, j):
        return (jnp.where(i < n_lat_tiles, i % tiles_per_seq, tiles_per_seq), 0)

    h_args, h_specs = _stream_specs(h_src, tm, n_lat_tiles * tm, n_all // tm)
    u = pl.pallas_call(
        functools.partial(_norm_kernel, n_lat_tiles=n_lat_tiles),
        grid=(n_all // tm,),
        in_specs=h_specs + [
            pl.BlockSpec((1, N_MOD, d), mod_idx),
            pl.BlockSpec((1, d), lambda i: (0, 0)),
        ],
        out_specs=pl.BlockSpec((tm, d), lambda i: (i, 0)),
        out_shape=jax.ShapeDtypeStruct((n_all, d), BF16),
        compiler_params=_cparams("parallel"),
        name="norm_modulate",
    )(*h_args, mod, g1)
    qk = pl.pallas_call(
        _inproj_qk_kernel,
        grid=(n_all // tm, QK_COLS // QK_TN),
        in_specs=[
            pl.BlockSpec((tm, d), lambda i, j: (i, 0)),
            pl.BlockSpec((d, QK_TN), lambda i, j: (0, j)),
            pl.BlockSpec((1, QK_TN), lambda i, j: (0, j)),
            pl.BlockSpec((tm, HEAD_DIM), rope_idx),
            pl.BlockSpec((tm, HEAD_DIM), rope_idx),
        ],
        out_specs=pl.BlockSpec((tm, QK_TN), lambda i, j: (i, j)),
        out_shape=jax.ShapeDtypeStruct((n_all, QK_COLS), BF16),
        compiler_params=_cparams("parallel", "arbitrary"),
        name="in_projection_qk",
    )(u, w_qk, gcol, cos, sin)
    vb = pl.pallas_call(
        _inproj_vb_kernel,
        grid=(n_all // tm, VB_COLS // VB_TN),
        in_specs=[
            pl.BlockSpec((tm, d), lambda i, j: (i, 0)),
            pl.BlockSpec((d, VB_TN), lambda i, j: (0, j)),
        ],
        out_specs=pl.BlockSpec((tm, VB_TN), lambda i, j: (i, j)),
        out_shape=jax.ShapeDtypeStruct((n_all, VB_COLS), BF16),
        compiler_params=_cparams("parallel", "arbitrary"),
        name="in_projection_vb",
    )(u, w_vb)
    return qk, vb, u


ATT_KC = 256
SAFE_SHIFT = 40.0


def _key_chunks(k_refs, v_refs):
    chunks = []
    for k_ref, v_ref in zip(k_refs, v_refs):
        n = k_ref.shape[0]
        for s in range(0, n, ATT_KC):
            chunks.append((k_ref, v_ref, s, min(ATT_KC, n - s)))
    return chunks


def _fill_shifts(bound_ref, m_scr, q_ref, chunks, heads):
    bound = bound_ref[0]

    @pl.when(bound <= SAFE_SHIFT)
    def _():
        m_scr[...] = jnp.full(m_scr.shape, bound, F32)

    @pl.when(bound > SAFE_SHIFT)
    def _():
        for idx, (qcol, kcol) in enumerate(heads):
            q = q_ref[:, qcol:qcol + HEAD_DIM]
            m = None
            for k_ref, _, s, n in chunks:
                part = jnp.max(_dot_nt(q, k_ref[s:s + n, kcol:kcol + HEAD_DIM]), axis=-1, keepdims=True)
                m = part if m is None else jnp.maximum(m, part)
            m_scr[idx] = jnp.broadcast_to(m, m_scr.shape[1:])


def _stream_softmax_pv(q, shift, chunks, kcol, vcols):
    o = None
    l = None
    for k_ref, v_ref, s, n in chunks:
        sc = _dot_nt(q, k_ref[s:s + n, kcol:kcol + HEAD_DIM])
        tiles = [jnp.exp(sc[:, t * LANES:(t + 1) * LANES] - shift) for t in range(n // LANES)]
        for e_t in tiles:
            l = e_t if l is None else l + e_t
        e = tiles[0] if len(tiles) == 1 else jnp.concatenate(tiles, axis=1)
        pv = _dot(e.astype(BF16), v_ref[s:s + n, vcols])
        o = pv if o is None else o + pv
    return o, jnp.sum(l, axis=-1, keepdims=True)


def _diff_attn_kernel(bound_ref, dl_ref, sg_ref, q_ref, *refs, lambda_init, n_pieces):
    k_refs = refs[0:2 * n_pieces:2]
    v_refs = refs[1:2 * n_pieces:2]
    o_ref = refs[2 * n_pieces]
    m_scr = refs[2 * n_pieces + 1]
    chunks = _key_chunks(k_refs, v_refs)
    heads = [(s * HEAD_DIM, s * HEAD_DIM) for s in range(2 * DIFF_HEADS)]
    _fill_shifts(bound_ref, m_scr, q_ref, chunks, heads)
    dl = dl_ref[...]
    lam = (jnp.exp(jnp.sum(dl[0:1] * dl[1:2], axis=-1, keepdims=True))
           - jnp.exp(jnp.sum(dl[2:3] * dl[3:4], axis=-1, keepdims=True)) + lambda_init)
    for h in range(DIFF_HEADS):
        c1 = 2 * h * HEAD_DIM
        c2 = c1 + HEAD_DIM
        vs = slice(h * DIFF_V_DIM, (h + 1) * DIFF_V_DIM)
        o1, l1 = _stream_softmax_pv(q_ref[:, c1:c1 + HEAD_DIM], m_scr[2 * h], chunks, c1, vs)
        o2, l2 = _stream_softmax_pv(q_ref[:, c2:c2 + HEAD_DIM], m_scr[2 * h + 1], chunks, c2, vs)
        o = o1 * (1.0 / l1) - o2 * (lam / l2)
        ms = jnp.mean(o * o, axis=-1, keepdims=True)
        o = o * lax.rsqrt(ms + EPS) * sg_ref[...] * (1.0 - lambda_init)
        o_ref[:, vs] = o.astype(BF16)


def _gqa_kernel(bound_ref, q_ref, *refs, n_pieces):
    k_refs = refs[0:2 * n_pieces:2]
    v_refs = refs[1:2 * n_pieces:2]
    o_ref = refs[2 * n_pieces]
    m_scr = refs[2 * n_pieces + 1]
    chunks = _key_chunks(k_refs, v_refs)
    group = GQA_HEADS // GQA_KV_HEADS
    heads = [(h * HEAD_DIM, (h // group) * HEAD_DIM) for h in range(GQA_HEADS)]
    _fill_shifts(bound_ref, m_scr, q_ref, chunks, heads)
    for h, (qcol, kcol) in enumerate(heads):
        o, l = _stream_softmax_pv(q_ref[:, qcol:qcol + HEAD_DIM], m_scr[h], chunks, kcol,
                                  slice(kcol, kcol + HEAD_DIM))
        o_ref[:, qcol:qcol + HEAD_DIM] = (o * (1.0 / l)).astype(BF16)


def _attention(kind, qk, vb, bound, extra, *, batch, seq, ctx, tq, latent, lambda_init=None):
    n_lat = batch * seq
    ctx_blk0 = n_lat // ctx
    if kind == "diff":
        qcol, kcol, vcol, kvw, v_arr = QK_AQ // 1024, QK_AK // 1024, VB_AV // 1024, 1024, vb
    else:
        qcol, kcol, vcol, kvw, v_arr = QK_CQ // 1024, QK_CK // 256, QK_CV // 256, 256, qk
    if latent:
        q_tiles = seq // tq
        q_spec = pl.BlockSpec((tq, 1024), lambda b, i: (b * q_tiles + i, qcol))
        o_spec = pl.BlockSpec((tq, 1024), lambda b, i: (b * q_tiles + i, 0))
        n_q = n_lat
    else:
        q_tiles = 1
        tq = ctx
        q_spec = pl.BlockSpec((ctx, 1024), lambda b, i: (ctx_blk0 + b, qcol))
        o_spec = pl.BlockSpec((ctx, 1024), lambda b, i: (b, 0))
        n_q = batch * ctx
    kv_specs = [
        pl.BlockSpec((ctx, kvw), lambda b, i: (ctx_blk0 + b, kcol)),
        pl.BlockSpec((ctx, kvw), lambda b, i: (ctx_blk0 + b, vcol)),
    ]
    kv_args = [qk, v_arr]
    if latent:
        kv_specs += [
            pl.BlockSpec((seq, kvw), lambda b, i: (b, kcol)),
            pl.BlockSpec((seq, kvw), lambda b, i: (b, vcol)),
        ]
        kv_args += [qk, v_arr]
    n_pieces = len(kv_args) // 2
    pre_specs = [pl.BlockSpec(memory_space=pltpu.SMEM)]
    pre_args = [bound]
    if kind == "diff":
        diff_lambda, sub_g = extra
        body = functools.partial(_diff_attn_kernel, lambda_init=lambda_init, n_pieces=n_pieces)
        pre_specs += [
            pl.BlockSpec((4, HEAD_DIM), lambda b, i: (0, 0)),
            pl.BlockSpec((1, DIFF_V_DIM), lambda b, i: (0, 0)),
        ]
        pre_args += [diff_lambda, sub_g]
        n_softmax = 2 * DIFF_HEADS
    else:
        body = functools.partial(_gqa_kernel, n_pieces=n_pieces)
        n_softmax = GQA_HEADS
    return pl.pallas_call(
        body,
        grid=(batch, q_tiles),
        in_specs=pre_specs + [q_spec] + kv_specs,
        out_specs=o_spec,
        out_shape=jax.ShapeDtypeStruct((n_q, 1024), BF16),
        scratch_shapes=[pltpu.VMEM((n_softmax, tq, LANES), F32)],
        compiler_params=_cparams("parallel", "arbitrary"),
        name=f"{kind}_attn_{'lat' if latent else 'ctx'}",
    )(*pre_args, qk, *kv_args)


LRU_CW = 512
LRU_PAD = 8
SQRT_GUARD = 1e-30


def _lru_kernel(bxc_ref, byc_ref, bxl_ref, byl_ref, cw_ref, cb_ref, wa_ref, ba_ref, wx_ref, bx_ref, lam_ref,
                *refs, ctx, seq, need_ctx):
    if need_ctx:
        yc_ref, yl_ref = refs[0], refs[1]
        scratch = refs[2:]
    else:
        yc_ref, yl_ref = None, refs[0]
        scratch = refs[1:]
    xpad, a_f, b_f, a_b, b_b = scratch
    a_scr = (a_f, a_b)
    b_scr = (b_f, b_b)
    lam = lam_ref[...]
    neg_sp = -LRU_C * jax.nn.softplus(-lam)

    def gates(x_ref, n):
        zeros = jnp.zeros((LRU_PAD, LRU_CW), F32)
        xpad[0:LRU_PAD, :] = zeros
        xpad[LRU_PAD:LRU_PAD + n, :] = x_ref[...].astype(F32)
        xpad[LRU_PAD + n:2 * LRU_PAD + n, :] = zeros
        xc = cb_ref[...] + cw_ref[0:1, :] * xpad[LRU_PAD - 2:LRU_PAD - 2 + n, :]
        for k in range(1, CONV_W):
            xc = xc + cw_ref[k:k + 1, :] * xpad[LRU_PAD - 2 + k:LRU_PAD - 2 + k + n, :]
        xcb = xc.astype(BF16)
        for d in range(2):
            for blk in range(LRU_CW // LRU_BLOCK_W):
                sl = slice(blk * LRU_BLOCK_W, (blk + 1) * LRU_BLOCK_W)
                xs = xcb[:, sl]
                r = _sigmoid(_dot(xs, wa_ref[d, blk].astype(BF16)) + ba_ref[d:d + 1, sl])
                gi = _sigmoid(_dot(xs, wx_ref[d, blk].astype(BF16)) + bx_ref[d:d + 1, sl])
                log_a = r * neg_sp[d:d + 1, sl]
                a = jnp.exp(log_a)
                one_m_a2 = 1.0 - a * a
                mult = one_m_a2 * lax.rsqrt(jnp.maximum(one_m_a2, SQRT_GUARD))
                a_scr[d][0:n, sl] = a
                b_scr[d][0:n, sl] = mult * gi * xc[:, sl]

    def scan(n, hf0, hb0):
        def body(t, carry):
            hf, hb = carry
            tb = n - 1 - t
            hf = a_f[pl.ds(t, 1), :] * hf + b_f[pl.ds(t, 1), :]
            b_f[pl.ds(t, 1), :] = hf
            hb = a_b[pl.ds(tb, 1), :] * hb + b_b[pl.ds(tb, 1), :]
            b_b[pl.ds(tb, 1), :] = hb
            return hf, hb
        return lax.fori_loop(0, n, body, (hf0, hb0), unroll=8)

    zero = jnp.zeros((1, LRU_CW), F32)
    gates(bxc_ref, ctx)
    hf, hb = scan(ctx, zero, zero)
    if need_ctx:
        yc_ref[...] = ((b_f[0:ctx, :] + b_b[0:ctx, :])
                       * jax.nn.gelu(byc_ref[...].astype(F32), approximate=True)).astype(BF16)
    gates(bxl_ref, seq)
    scan(seq, hf, hb)
    yl_ref[...] = ((b_f[0:seq, :] + b_b[0:seq, :])
                   * jax.nn.gelu(byl_ref[...].astype(F32), approximate=True)).astype(BF16)


def _rglru(vb, conv_w, conv_b, w_a, b_a, w_x, b_x, lam, *, batch, seq, ctx, need_ctx):
    n_lat = batch * seq
    ctx_blk0 = n_lat // ctx
    cbx, cby = VB_BX // LRU_CW, VB_BY // LRU_CW
    nblk = LRU_CW // LRU_BLOCK_W
    tmax = max(seq, ctx)
    out_specs = [pl.BlockSpec((seq, LRU_CW), lambda b, c: (b, c))]
    out_shape = [jax.ShapeDtypeStruct((n_lat, LRU_WIDTH), BF16)]
    if need_ctx:
        out_specs = [pl.BlockSpec((ctx, LRU_CW), lambda b, c: (b, c))] + out_specs
        out_shape = [jax.ShapeDtypeStruct((batch * ctx, LRU_WIDTH), BF16)] + out_shape
    outs = pl.pallas_call(
        functools.partial(_lru_kernel, ctx=ctx, seq=seq, need_ctx=need_ctx),
        grid=(batch, LRU_WIDTH // LRU_CW),
        in_specs=[
            pl.BlockSpec((ctx, LRU_CW), lambda b, c: (ctx_blk0 + b, cbx + c)),
            pl.BlockSpec((ctx, LRU_CW), lambda b, c: (ctx_blk0 + b, cby + c)),
            pl.BlockSpec((seq, LRU_CW), lambda b, c: (b, cbx + c)),
            pl.BlockSpec((seq, LRU_CW), lambda b, c: (b, cby + c)),
            pl.BlockSpec((CONV_W, LRU_CW), lambda b, c: (0, c)),
            pl.BlockSpec((1, LRU_CW), lambda b, c: (0, c)),
            pl.BlockSpec((2, nblk, LRU_BLOCK_W, LRU_BLOCK_W), lambda b, c: (0, c, 0, 0)),
            pl.BlockSpec((2, LRU_CW), lambda b, c: (0, c)),
            pl.BlockSpec((2, nblk, LRU_BLOCK_W, LRU_BLOCK_W), lambda b, c: (0, c, 0, 0)),
            pl.BlockSpec((2, LRU_CW), lambda b, c: (0, c)),
            pl.BlockSpec((2, LRU_CW), lambda b, c: (0, c)),
        ],
        out_specs=out_specs,
        out_shape=out_shape,
        scratch_shapes=[
            pltpu.VMEM((tmax + 2 * LRU_PAD, LRU_CW), F32),
            pltpu.VMEM((tmax, LRU_CW), F32),
            pltpu.VMEM((tmax, LRU_CW), F32),
            pltpu.VMEM((tmax, LRU_CW), F32),
            pltpu.VMEM((tmax, LRU_CW), F32),
        ],
        compiler_params=_cparams("parallel", "arbitrary"),
        name="rglru",
    )(vb, vb, vb, vb, conv_w, conv_b, w_a, b_a, w_x, b_x, lam)
    if need_ctx:
        return outs[1], outs[0]
    return outs[0], None


MERGE_TN = 256


def _merge_kernel(u_ref, *refs, n_lat_tiles, has_ctx):
    n_y = 6 if has_ctx else 3
    y_refs = refs[:n_y]
    wg_refs = refs[n_y:n_y + 3]
    bg_ref = refs[n_y + 3]
    wbr_refs = refs[n_y + 4:n_y + 7]
    m_ref = refs[n_y + 7]

    def body(ys):
        u = u_ref[...]
        m = None
        for k in range(N_BRANCH):
            g = jax.nn.sigmoid(_dot(u, wg_refs[k][...]) + bg_ref[k:k + 1, :])
            part = g * _dot(ys[k][...], wbr_refs[k][...])
            m = part if m is None else m + part
        m_ref[...] = m.astype(BF16)

    if not has_ctx:
        body(y_refs)
        return
    is_lat = pl.program_id(0) < n_lat_tiles

    @pl.when(is_lat)
    def _():
        body(y_refs[0:3])

    @pl.when(jnp.logical_not(is_lat))
    def _():
        body(y_refs[3:6])


def _merge(u, y_lat, y_ctx, w_gate, b_gate, w_branch, *, n_tok, n_lat, tm):
    d = u.shape[1]
    nj = d // MERGE_TN
    nl = n_lat // tm
    has_ctx = y_ctx is not None
    y_specs = [pl.BlockSpec((tm, BRANCH_W), lambda i, j: (jnp.minimum(i, nl - 1), 0))] * 3
    y_args = list(y_lat)
    if has_ctx:
        y_specs += [pl.BlockSpec((tm, BRANCH_W), lambda i, j: (jnp.maximum(i - nl, 0), 0))] * 3
        y_args += list(y_ctx)
    wbr_spec = pl.BlockSpec((BRANCH_W, MERGE_TN), lambda i, j: (0, j))
    return pl.pallas_call(
        functools.partial(_merge_kernel, n_lat_tiles=nl, has_ctx=has_ctx),
        grid=(n_tok // tm, nj),
        in_specs=[pl.BlockSpec((tm, d), lambda i, j: (i, 0))] + y_specs + [
            pl.BlockSpec((d, MERGE_TN), lambda i, j: (0, j)),
            pl.BlockSpec((d, MERGE_TN), lambda i, j: (0, nj + j)),
            pl.BlockSpec((d, MERGE_TN), lambda i, j: (0, 2 * nj + j)),
            pl.BlockSpec((N_BRANCH, MERGE_TN), lambda i, j: (0, j)),
            wbr_spec, wbr_spec, wbr_spec,
        ],
        out_specs=pl.BlockSpec((tm, MERGE_TN), lambda i, j: (i, j)),
        out_shape=jax.ShapeDtypeStruct((n_tok, d), BF16),
        compiler_params=_cparams("parallel", "arbitrary"),
        name="branch_merge",
    )(u, *y_args, w_gate, w_gate, w_gate, b_gate, *w_branch)


OUT_RC = 256
ROUTE_E, ROUTE_P, ROUTE_RANK = 0, 2, 4
COUNT_ROWS = 8


def _outproj_kernel(m_ref, *refs, n_lat_tiles):
    mod_ref, g_ref, wo_ref, wr_ref, br_ref, h1_ref, vp_ref, route_ref, cnt_ref, v_scr = refs[-10:]
    h_refs = refs[:-10]
    tm, d = h1_ref.shape
    rc = min(OUT_RC, tm)
    for r in range(tm // rc):
        rows = slice(r * rc, (r + 1) * rc)
        h1 = (_read_stream(h_refs, rows, n_lat_tiles)
              + mod_ref[0, 2:3, :] * _dot(m_ref[rows, :], wo_ref[...]))
        h1_ref[rows, :] = h1
        ms = jnp.mean(h1 * h1, axis=-1, keepdims=True)
        y = h1 * lax.rsqrt(ms + EPS) * g_ref[...]
        v_scr[rows, :] = (y * (1.0 + mod_ref[0, 4:5, :]) + mod_ref[0, 3:4, :]).astype(BF16)
    v = v_scr[...]
    half = d // 2
    lo = lax.bitcast_convert_type(v[:, :half].astype(F32), U32)
    hi = lax.bitcast_convert_type(v[:, half:].astype(F32), U32)
    vp_ref[...] = (lo >> 16) | (hi & jnp.uint32(0xFFFF0000))

    logits = _dot(v, wr_ref[...]) + br_ref[...]
    lane = lax.broadcasted_iota(I32, logits.shape, 1).astype(F32)
    neg = jnp.float32(-jnp.inf)
    big = jnp.float32(1e9)

    def masked_argmax(mask):
        val = jnp.max(jnp.where(mask, logits, neg), axis=-1, keepdims=True)
        idx = jnp.min(jnp.where(jnp.logical_and(mask, logits == val), lane, big), axis=-1, keepdims=True)
        return val, idx

    gmask = lane < N_GROUPS
    gmax, gidx = masked_argmax(gmask)
    gsum = jnp.sum(jnp.where(gmask, jnp.exp(logits - gmax), 0.0), axis=-1, keepdims=True)
    g_w = 1.0 / gsum
    lo_lane = N_GROUPS + EXPERTS_PER_GROUP * gidx
    emask = jnp.logical_and(lane >= lo_lane, lane < lo_lane + EXPERTS_PER_GROUP)
    v1, i1 = masked_argmax(emask)
    v2, i2 = masked_argmax(jnp.logical_and(emask, lane != i1))
    e21 = jnp.exp(v2 - v1)
    p1 = g_w / (1.0 + e21)
    p2 = g_w * e21 / (1.0 + e21)
    e1 = i1 - N_GROUPS
    e2 = i2 - N_GROUPS
    oh1 = jnp.where(lane == e1, 1.0, 0.0)
    oh2 = jnp.where(lane == e2, 1.0, 0.0)
    cnt = oh1 + oh2
    row = lax.broadcasted_iota(I32, (tm, tm), 0)
    col = lax.broadcasted_iota(I32, (tm, tm), 1)
    before = jnp.where(row > col, 1.0, 0.0).astype(BF16)
    prefix = _dot(before, cnt.astype(BF16))
    rank1 = jnp.sum(prefix * oh1, axis=-1, keepdims=True)
    rank2 = jnp.sum(prefix * oh2, axis=-1, keepdims=True)
    route = jnp.where(lane == 0, e1, jnp.where(lane == 1, e2, jnp.where(lane == 2, p1, jnp.where(
        lane == 3, p2, jnp.where(lane == 4, rank1, jnp.where(lane == 5, rank2, 0.0))))))
    route_ref[...] = route
    cnt_ref[0] = jnp.broadcast_to(jnp.sum(cnt, axis=0, keepdims=True), (COUNT_ROWS, LANES))


def _out_projection(m, h_src, mod, g2, w_out, w_router, b_router, *, n_tok, tm, n_lat, seq, batch):
    d = m.shape[1]
    n_tiles = n_tok // tm

    def mod_idx(i):
        return (jnp.where(i * tm < n_lat, (i * tm) // seq, batch), 0, 0)

    h_args, h_specs = _stream_specs(h_src, tm, n_lat, n_tiles)
    return pl.pallas_call(
        functools.partial(_outproj_kernel, n_lat_tiles=n_lat // tm),
        grid=(n_tiles,),
        in_specs=[pl.BlockSpec((tm, d), lambda i: (i, 0))] + h_specs + [
            pl.BlockSpec((1, N_MOD, d), mod_idx),
            pl.BlockSpec((1, d), lambda i: (0, 0)),
            pl.BlockSpec((d, d), lambda i: (0, 0), pipeline_mode=pl.Buffered(1)),
            pl.BlockSpec((d, LANES), lambda i: (0, 0)),
            pl.BlockSpec((1, LANES), lambda i: (0, 0)),
        ],
        out_specs=[
            pl.BlockSpec((tm, d), lambda i: (i, 0)),
            pl.BlockSpec((tm, d // 2), lambda i: (i, 0)),
            pl.BlockSpec((tm, LANES), lambda i: (i, 0)),
            pl.BlockSpec((1, COUNT_ROWS, LANES), lambda i: (i, 0, 0)),
        ],
        out_shape=[
            jax.ShapeDtypeStruct((n_tok, d), F32),
            jax.ShapeDtypeStruct((n_tok, d // 2), U32),
            jax.ShapeDtypeStruct((n_tok, LANES), F32),
            jax.ShapeDtypeStruct((n_tiles, COUNT_ROWS, LANES), F32),
        ],
        scratch_shapes=[pltpu.VMEM((tm, d), BF16)],
        compiler_params=_cparams("parallel"),
        name="out_projection_router",
    )(m, *h_args, mod, g2, w_out, w_router, b_router)


MOE_TM = 256
CAST_ROWS = 64
ROW_TM = 256


def _dispatch(route, counts, n_tok, tm_route):
    n_tiles = (2 * n_tok) // MOE_TM + N_EXPERTS
    counts = counts[:, 0, :N_EXPERTS].astype(I32)
    tile_base = jnp.cumsum(counts, axis=0) - counts
    total = jnp.sum(counts, axis=0)
    padded = ((total + MOE_TM - 1) // MOE_TM) * MOE_TM
    seg_end = jnp.cumsum(padded)
    base = (seg_end - padded)[None, :] + tile_base
    r3 = route.reshape(n_tok // tm_route, tm_route, LANES)
    experts = jnp.arange(N_EXPERTS, dtype=I32)
    dest = []
    for k in range(2):
        hit = r3[:, :, ROUTE_E + k].astype(I32)[:, :, None] == experts
        picked = jnp.sum(jnp.where(hit, base[:, None, :], 0), axis=-1)
        dest.append((picked + r3[:, :, ROUTE_RANK + k].astype(I32)).reshape(n_tok // ROW_TM, 1, ROW_TM))
    tile_start = jnp.arange(n_tiles, dtype=I32) * MOE_TM
    n_used = seg_end[-1] // MOE_TM
    tile_valid = (tile_start < seg_end[-1]).astype(I32)
    tile_index = jnp.minimum(jnp.arange(n_tiles, dtype=I32), n_used - 1)
    tile_expert = jnp.sum((seg_end[None, :] <= (tile_index * MOE_TM)[:, None]).astype(I32), axis=1)
    prev = jnp.concatenate([jnp.full((1,), -1, I32), tile_expert[:-1]])
    tile_first = (tile_expert != prev).astype(I32)
    later_first = jnp.logical_and(tile_first == 1, tile_valid == 1)
    pos = jnp.arange(n_tiles, dtype=I32)
    first_pos = jnp.where(later_first, pos, n_tiles)
    next_pos = jnp.min(jnp.where(pos[None, :] > pos[:, None], first_pos[None, :], n_tiles), axis=1)
    tile_next = jnp.where(next_pos < n_tiles, tile_expert[jnp.minimum(next_pos, n_tiles - 1)], -1).astype(I32)
    tails = jnp.maximum(seg_end - MOE_TM, 0).astype(I32)
    return dest, (tile_expert, tile_first, tile_valid, tile_index, tile_next), tails


def _row_scatter_kernel(tail_ref, d0_ref, d1_ref, v_ref, x_hbm, zbuf, sem_z, sem):
    @pl.when(pl.program_id(0) == 0)
    def _():
        zbuf[...] = jnp.zeros_like(zbuf)
        def zero_tile(start):
            start = pl.multiple_of(start, MOE_TM)
            return pltpu.make_async_copy(zbuf, x_hbm.at[pl.ds(start, MOE_TM), :], sem_z)
        for e in range(N_EXPERTS):
            zero_tile(tail_ref[e]).start()
        for e in range(N_EXPERTS):
            zero_tile(tail_ref[e]).wait()
        first_unused = tail_ref[N_EXPERTS - 1] // MOE_TM + 1
        n_tiles = x_hbm.shape[0] // MOE_TM

        def start_unused(t, carry):
            zero_tile(t * MOE_TM).start()
            return carry

        def wait_unused(t, carry):
            zero_tile(t * MOE_TM).wait()
            return carry
        lax.fori_loop(first_unused, n_tiles, start_unused, 0)
        lax.fori_loop(first_unused, n_tiles, wait_unused, 0)

    def row_copies(g, k, d0, d1):
        src = v_ref.at[pl.ds(pl.multiple_of(g * 8, 8) + k, 1), :]
        return (pltpu.make_async_copy(src, x_hbm.at[pl.ds(d0, 1), :], sem),
                pltpu.make_async_copy(src, x_hbm.at[pl.ds(d1, 1), :], sem))

    def issue(g, carry):
        for k in range(8):
            for copy in row_copies(g, k, d0_ref[0, 0, g * 8 + k], d1_ref[0, 0, g * 8 + k]):
                copy.start()
        return carry
    lax.fori_loop(0, ROW_TM // 8, issue, 0)

    def drain(g, carry):
        for k in range(8):
            for copy in row_copies(g, k, 0, 0):
                copy.wait()
        return carry
    lax.fori_loop(0, ROW_TM // 8, drain, 0)


def _row_scatter(vp, dest, tails, n_rows):
    n_tok, half = vp.shape
    n_steps = n_tok // ROW_TM
    d0, d1 = dest
    grid_spec = pltpu.PrefetchScalarGridSpec(
        num_scalar_prefetch=1,
        grid=(n_steps,),
        in_specs=[
            pl.BlockSpec((1, 1, ROW_TM), lambda i, t: (i, 0, 0), memory_space=pltpu.SMEM),
            pl.BlockSpec((1, 1, ROW_TM), lambda i, t: (i, 0, 0), memory_space=pltpu.SMEM),
            pl.BlockSpec((ROW_TM, half), lambda i, t: (i, 0)),
        ],
        out_specs=pl.BlockSpec(memory_space=pl.ANY),
        scratch_shapes=[
            pltpu.VMEM((MOE_TM, half), U32),
            pltpu.SemaphoreType.DMA(()),
            pltpu.SemaphoreType.DMA(()),
        ],
    )
    return pl.pallas_call(
        _row_scatter_kernel,
        grid_spec=grid_spec,
        out_shape=jax.ShapeDtypeStruct((n_rows, half), U32),
        compiler_params=_cparams("arbitrary"),
        name="moe_row_scatter",
    )(tails, d0, d1, vp)


def _moe_kernel(te_ref, tf_ref, tv_ref, ti_ref, tn_ref, x_ref, w1_hbm, w3_hbm, w2_hbm, y_ref,
                s1, s3, s2, w1b, w3b, w2b, sems, *, layer):
    i = pl.program_id(0)
    valid = tv_ref[i] == 1

    def weight_copies(e):
        return (pltpu.make_async_copy(w1_hbm.at[layer, e], s1, sems.at[0]),
                pltpu.make_async_copy(w3_hbm.at[layer, e], s3, sems.at[1]),
                pltpu.make_async_copy(w2_hbm.at[layer, e], s2, sems.at[2]))

    @pl.when(i == 0)
    def _():
        for copy in weight_copies(te_ref[0]):
            copy.start()

    @pl.when(valid)
    def _():
        @pl.when(tf_ref[i] == 1)
        def _():
            for copy in weight_copies(te_ref[i]):
                copy.wait()
            for src, dst in ((s1, w1b), (s3, w3b), (s2, w2b)):
                rows_per = CAST_ROWS * 512 // src.shape[1]

                def cast_rows(c, carry, src=src, dst=dst, rows_per=rows_per):
                    rows = pl.ds(pl.multiple_of(c * rows_per, rows_per), rows_per)
                    dst[rows, :] = src[rows, :].astype(BF16)
                    return carry
                lax.fori_loop(0, src.shape[0] // rows_per, cast_rows, 0)

            @pl.when(tn_ref[i] >= 0)
            def _():
                for copy in weight_copies(tn_ref[i]):
                    copy.start()

        half = x_ref.shape[1]
        xp = x_ref[...]
        lo = lax.bitcast_convert_type(xp << 16, F32).astype(BF16)
        hi = lax.bitcast_convert_type(xp & jnp.uint32(0xFFFF0000), F32).astype(BF16)
        h1 = _dot(lo, w1b[0:half, :]) + _dot(hi, w1b[half:, :])
        h3 = _dot(lo, w3b[0:half, :]) + _dot(hi, w3b[half:, :])
        hh = (h1 * jax.nn.sigmoid(h1) * h3).astype(BF16)
        y_ref[...] = _dot(hh, w2b[...])

    @pl.when(jnp.logical_not(valid))
    def _():
        y_ref[...] = jnp.zeros_like(y_ref)


def _moe_experts(layer, x_sorted, tiles, w1, w3, w2):
    tile_expert, tile_first, tile_valid, tile_index, tile_next = tiles
    n_tiles = tile_expert.shape[0]
    half = x_sorted.shape[1]
    d = 2 * half
    grid_spec = pltpu.PrefetchScalarGridSpec(
        num_scalar_prefetch=5,
        grid=(n_tiles,),
        in_specs=[
            pl.BlockSpec((MOE_TM, half), lambda i, te, tf, tv, ti, tn: (ti[i], 0)),
            pl.BlockSpec(memory_space=pl.ANY),
            pl.BlockSpec(memory_space=pl.ANY),
            pl.BlockSpec(memory_space=pl.ANY),
        ],
        out_specs=pl.BlockSpec((MOE_TM, d), lambda i, te, tf, tv, ti, tn: (i, 0)),
        scratch_shapes=[
            pltpu.VMEM((d, D_EXPERT), F32),
            pltpu.VMEM((d, D_EXPERT), F32),
            pltpu.VMEM((D_EXPERT, d), F32),
            pltpu.VMEM((d, D_EXPERT), BF16),
            pltpu.VMEM((d, D_EXPERT), BF16),
            pltpu.VMEM((D_EXPERT, d), BF16),
            pltpu.SemaphoreType.DMA((3,)),
        ],
    )
    return pl.pallas_call(
        functools.partial(_moe_kernel, layer=layer),
        grid_spec=grid_spec,
        out_shape=jax.ShapeDtypeStruct((n_tiles * MOE_TM, d), F32),
        compiler_params=_cparams("arbitrary"),
        name="moe_experts",
    )(tile_expert, tile_first, tile_valid, tile_index, tile_next, x_sorted, w1, w3, w2)


def _combine_kernel(d0_ref, d1_ref, n0_ref, n1_ref, h_ref, route_ref, mod_ref, y_hbm, o_ref, ybuf, sems):
    i = pl.program_id(0)
    n_steps = pl.num_programs(0)

    def row_copies(slot, g, k, d0, d1):
        rows = pl.ds(pl.multiple_of(g * 8, 8) + k, 1)
        return (pltpu.make_async_copy(y_hbm.at[pl.ds(d0, 1), :], ybuf.at[slot, 0, rows, :], sems.at[slot]),
                pltpu.make_async_copy(y_hbm.at[pl.ds(d1, 1), :], ybuf.at[slot, 1, rows, :], sems.at[slot]))

    def request(slot, i0_ref, i1_ref):
        def body(g, carry):
            for k in range(8):
                for copy in row_copies(slot, g, k, i0_ref[0, 0, g * 8 + k], i1_ref[0, 0, g * 8 + k]):
                    copy.start()
            return carry
        lax.fori_loop(0, ROW_TM // 8, body, 0)

    def await_rows(slot):
        def body(g, carry):
            for k in range(8):
                for copy in row_copies(slot, g, k, 0, 0):
                    copy.wait()
            return carry
        lax.fori_loop(0, ROW_TM // 8, body, 0)

    slot = i % 2

    @pl.when(i == 0)
    def _():
        request(0, d0_ref, d1_ref)

    @pl.when(i + 1 < n_steps)
    def _():
        request(1 - slot, n0_ref, n1_ref)

    await_rows(slot)
    p0 = route_ref[:, ROUTE_P:ROUTE_P + 1]
    p1 = route_ref[:, ROUTE_P + 1:ROUTE_P + 2]
    o_ref[...] = h_ref[...] + mod_ref[0, 5:6, :] * (p0 * ybuf[slot, 0] + p1 * ybuf[slot, 1])


def _combine(h1, route, mod, y_sorted, dest, *, n_tok, n_lat, seq, batch):
    d = h1.shape[1]
    tm = ROW_TM
    n_steps = n_tok // tm
    d0, d1 = dest

    def mod_idx(i):
        return (jnp.where(i * tm < n_lat, (i * tm) // seq, batch), 0, 0)

    return pl.pallas_call(
        _combine_kernel,
        grid=(n_steps,),
        in_specs=[
            pl.BlockSpec((1, 1, tm), lambda i: (i, 0, 0), memory_space=pltpu.SMEM),
            pl.BlockSpec((1, 1, tm), lambda i: (i, 0, 0), memory_space=pltpu.SMEM),
            pl.BlockSpec((1, 1, tm), lambda i: (jnp.minimum(i + 1, n_steps - 1), 0, 0), memory_space=pltpu.SMEM),
            pl.BlockSpec((1, 1, tm), lambda i: (jnp.minimum(i + 1, n_steps - 1), 0, 0), memory_space=pltpu.SMEM),
            pl.BlockSpec((tm, d), lambda i: (i, 0)),
            pl.BlockSpec((tm, LANES), lambda i: (i, 0)),
            pl.BlockSpec((1, N_MOD, d), mod_idx),
            pl.BlockSpec(memory_space=pl.ANY),
        ],
        out_specs=pl.BlockSpec((tm, d), lambda i: (i, 0)),
        out_shape=jax.ShapeDtypeStruct((n_tok, d), F32),
        scratch_shapes=[pltpu.VMEM((2, 2, tm, d), F32), pltpu.SemaphoreType.DMA((2,))],
        compiler_params=_cparams("arbitrary"),
        name="moe_combine",
    )(d0, d1, d0, d1, h1, route, mod, y_sorted)


def _rope_tables(seq, tm):
    n_rows = seq // GRID_W
    row = jnp.repeat(jnp.arange(n_rows), GRID_W).astype(F32)
    col = jnp.tile(jnp.arange(GRID_W), n_rows).astype(F32)
    half = HEAD_DIM // 2
    inv = 1.0 / (ROPE_THETA ** (jnp.arange(0, half, 2, dtype=F32) / half))
    ang_r = row[:, None] * inv
    ang_c = col[:, None] * inv
    ang = jnp.concatenate([ang_r, ang_c, ang_r, ang_c], axis=-1)
    cos, sin = jnp.cos(ang), jnp.sin(ang)
    sin_signed = jnp.where(jnp.arange(HEAD_DIM) < half, -sin, sin)
    ident = jnp.zeros((tm, HEAD_DIM), F32)
    return jnp.concatenate([cos, ident + 1.0]), jnp.concatenate([sin_signed, ident])


def _permute_heads(a):
    lead = a.shape[:-1]
    quarter = HEAD_DIM // 4
    assert HEAD_PERM[quarter] == 2 * quarter and HEAD_PERM[2 * quarter] == quarter
    q = a.reshape(*lead, a.shape[-1] // HEAD_DIM, 4, quarter)
    q = jnp.stack([q[..., 0, :], q[..., 2, :], q[..., 1, :], q[..., 3, :]], axis=-2)
    return q.reshape(*lead, a.shape[-1])


def _score_bound(gq, gk):
    return (1.02 * HEAD_DIM * jnp.max(jnp.abs(gq)) * jnp.max(jnp.abs(gk))).reshape(1).astype(F32)


def kernel(x, c, ctx, c_ctx, w_ada, b_ada, norm1_g, norm2_g, w_in, b_gate, q_norm_a, k_norm_a, diff_lambda,
           sub_norm_a, q_norm_c, k_norm_c, conv_w, conv_b, lru_w_a, lru_b_a, lru_w_x, lru_b_x, lru_lambda,
           w_branch_a, w_branch_b, w_branch_c, w_out, w_group, b_group, w_route, b_route, w1, w3, w2):
    batch, seq, d = x.shape
    ctx_len = ctx.shape[1]
    depth = w_ada.shape[0]
    n_lat = batch * seq
    n_ctx = batch * ctx_len
    n_all = n_lat + n_ctx
    tm = min(1024, seq, n_ctx)
    assert seq % tm == 0 and n_ctx % tm == 0 and n_lat % ctx_len == 0 and batch < MOD_ROWS
    assert seq % GRID_W == 0 and seq % ROW_TM == 0 and n_ctx % ROW_TM == 0
    tq = min(256, seq)
    tm_out = min(512, tm)

    cc = jnp.zeros((MOD_ROWS, d), F32).at[:batch].set(c).at[batch].set(c_ctx)
    mod_all = _ada_modulation(cc, w_ada, b_ada).reshape(depth, MOD_ROWS, N_MOD, d)
    rope = _rope_tables(seq, tm)
    h_src = (x.reshape(n_lat, d), ctx.reshape(n_ctx, d))
    scale = HEAD_DIM ** -0.5

    for l in range(depth):
        last = l == depth - 1
        lambda_init = 0.8 - 0.6 * math.exp(-0.3 * l)
        mod = mod_all[l]
        wl = w_in[l]
        w_qk = jnp.concatenate([_permute_heads(wl[:, 0:2048]), _permute_heads(wl[:, 5120:6400]),
                                wl[:, 6400:6656]], axis=1).astype(BF16)
        w_vb = wl[:, 2048:5120].astype(BF16)
        w_gate = wl[:, 6656:].astype(BF16)
        gq_a, gq_c = q_norm_a[l] * scale, q_norm_c[l] * scale
        gcol = jnp.concatenate([
            _permute_heads(jnp.concatenate([
                jnp.tile(gq_a, 2 * DIFF_HEADS), jnp.tile(k_norm_a[l], 2 * DIFF_HEADS),
                jnp.tile(gq_c, GQA_HEADS), jnp.tile(k_norm_c[l], GQA_KV_HEADS)])),
            jnp.ones((QK_COLS - QK_CV,), F32)]).reshape(1, QK_COLS)
        bound_a = _score_bound(gq_a, k_norm_a[l])
        bound_c = _score_bound(gq_c, k_norm_c[l])

        qk, vb, u = _in_projection(h_src, mod, norm1_g[l].reshape(1, d), w_qk, w_vb, gcol, rope, n_all=n_all,
                                   tm=tm, n_lat_tiles=n_lat // tm, tiles_per_seq=seq // tm, batch=batch)
        dims = dict(batch=batch, seq=seq, ctx=ctx_len)
        extra_a = (diff_lambda[l], sub_norm_a[l].reshape(1, DIFF_V_DIM))
        ya = _attention("diff", qk, vb, bound_a, extra_a, tq=tq, latent=True, lambda_init=lambda_init, **dims)
        yc = _attention("gqa", qk, vb, bound_c, None, tq=tq, latent=True, **dims)
        yb, yb_c = _rglru(vb, conv_w[l], conv_b[l].reshape(1, LRU_WIDTH), lru_w_a[l], lru_b_a[l],
                          lru_w_x[l], lru_b_x[l], lru_lambda[l], need_ctx=not last, **dims)
        if last:
            n_tok, y_ctx = n_lat, None
        else:
            n_tok = n_all
            ya_c = _attention("diff", qk, vb, bound_a, extra_a, tq=tq, latent=False, lambda_init=lambda_init,
                              **dims)
            yc_c = _attention("gqa", qk, vb, bound_c, None, tq=tq, latent=False, **dims)
            y_ctx = (ya_c, yb_c, yc_c)
        w_branch = (w_branch_a[l].astype(BF16), w_branch_b[l].astype(BF16), w_branch_c[l].astype(BF16))
        m = _merge(u, (ya, yb, yc), y_ctx, w_gate, b_gate[l], w_branch, n_tok=n_tok, n_lat=n_lat, tm=tm)
        w_router = jnp.zeros((d, LANES), F32).at[:, :N_GROUPS].set(w_group[l])
        w_router = w_router.at[:, N_GROUPS:N_GROUPS + N_EXPERTS].set(w_route[l]).astype(BF16)
        b_router = jnp.zeros((1, LANES), F32).at[0, :N_GROUPS].set(b_group[l])
        b_router = b_router.at[0, N_GROUPS:N_GROUPS + N_EXPERTS].set(b_route[l])
        h1, vp, route, counts = _out_projection(m, h_src, mod, norm2_g[l].reshape(1, d),
                                                w_out[l].astype(BF16), w_router, b_router, n_tok=n_tok,
                                                tm=tm_out, n_lat=n_lat, seq=seq, batch=batch)
        dest, tiles, tails = _dispatch(route, counts, n_tok, tm_out)
        x_sorted = _row_scatter(vp, dest, tails, tiles[0].shape[0] * MOE_TM)
        y_sorted = _moe_experts(l, x_sorted, tiles, w1, w3, w2)
        h_src = _combine(h1, route, mod, y_sorted, dest, n_tok=n_tok, n_lat=n_lat, seq=seq, batch=batch)
    return h_src.reshape(batch, seq, d)
```

```python
import functools
import math

import jax
import jax.numpy as jnp
from jax import lax
from jax.experimental import pallas as pl
from jax.experimental.pallas import tpu as pltpu

F32 = jnp.float32
BF16 = jnp.bfloat16
I32 = jnp.int32
U32 = jnp.uint32

EPS = 1e-6
HEAD_DIM = 128
GRID_W = 64
ROPE_THETA = 10000.0
DIFF_HEADS = 4
DIFF_V_DIM = 2 * HEAD_DIM
LRU_WIDTH = 1024
LRU_BLOCK_W = 128
CONV_W = 4
LRU_C = 8.0
GQA_HEADS = 8
GQA_KV_HEADS = 2
N_BRANCH = 3
BRANCH_W = 1024
N_GROUPS = 4
EXPERTS_PER_GROUP = 8
N_EXPERTS = N_GROUPS * EXPERTS_PER_GROUP
D_EXPERT = 512
N_MOD = 6
MOD_ROWS = 16
LANES = 128

QK_AQ, QK_AK, QK_CQ, QK_CK, QK_CV = 0, 1024, 2048, 3072, 3328
VB_AV, VB_BX, VB_BY = 0, 1024, 2048
QK_COLS = 3584
VB_COLS = 3072
QK_TN = QK_COLS // 2
VB_TN = VB_COLS // 2
PROJ_CW = 256
PROJ_RC = 512
HEAD_PERM = tuple(range(0, 32)) + tuple(range(64, 96)) + tuple(range(32, 64)) + tuple(range(96, 128))

VMEM_LIMIT = 56 * 1024 * 1024


def _cparams(*sem):
    return pltpu.CompilerParams(dimension_semantics=sem, vmem_limit_bytes=VMEM_LIMIT)


def _dot(a, b):
    return jnp.dot(a, b, preferred_element_type=F32)


def _dot_nt(a, b):
    return lax.dot_general(a, b, (((1,), (1,)), ((), ())), preferred_element_type=F32)


def _sigmoid(x):
    return 0.5 * jnp.tanh(0.5 * x) + 0.5


def _lane_mean(x2):
    return _dot(x2.astype(BF16), jnp.full((LANES, LANES), 1.0 / LANES, BF16))


def _ada_kernel(c_ref, w_ref, b_ref, o_ref):
    c = c_ref[...]
    s = (c * jax.nn.sigmoid(c)).astype(BF16)
    o_ref[0] = _dot(s, w_ref[0].astype(BF16)) + b_ref[0]


def _ada_modulation(cc, w_ada, b_ada):
    depth, d, n = w_ada.shape
    tn = 1024
    return pl.pallas_call(
        _ada_kernel,
        grid=(depth, n // tn),
        in_specs=[
            pl.BlockSpec((MOD_ROWS, d), lambda l, j: (0, 0)),
            pl.BlockSpec((1, d, tn), lambda l, j: (l, 0, j)),
            pl.BlockSpec((1, 1, tn), lambda l, j: (l, 0, j)),
        ],
        out_specs=pl.BlockSpec((1, MOD_ROWS, tn), lambda l, j: (l, 0, j)),
        out_shape=jax.ShapeDtypeStruct((depth, MOD_ROWS, n), F32),
        compiler_params=_cparams("parallel", "parallel"),
        name="ada_modulation",
    )(cc, w_ada, b_ada.reshape(depth, 1, n))


PREP_TN = 256


def _wprep_kernel(w_ref, o_ref, *, n_perm_tiles):
    def plain():
        o_ref[...] = w_ref[0].astype(BF16)

    def permuted():
        quarter = lax.broadcasted_iota(I32, (1, HEAD_DIM), 1) // (HEAD_DIM // 4)
        for c in range(PREP_TN // HEAD_DIM):
            sl = slice(c * HEAD_DIM, (c + 1) * HEAD_DIM)
            x = w_ref[0, :, sl]
            y = jnp.where(quarter == 1, pltpu.roll(x, 96, 1), jnp.where(quarter == 2, pltpu.roll(x, 32, 1), x))
            o_ref[:, sl] = y.astype(BF16)

    if n_perm_tiles == 0:
        plain()
        return
    is_perm = pl.program_id(0) < n_perm_tiles
    pl.when(is_perm)(permuted)
    pl.when(jnp.logical_not(is_perm))(plain)


def _prepare_weight(w_in, layer, n_cols, col_tile_of, n_perm_tiles, name):
    d = w_in.shape[1]
    return pl.pallas_call(
        functools.partial(_wprep_kernel, n_perm_tiles=n_perm_tiles),
        grid=(n_cols // PREP_TN,),
        in_specs=[pl.BlockSpec((1, d, PREP_TN), lambda j: (layer, 0, col_tile_of(j)))],
        out_specs=pl.BlockSpec((d, PREP_TN), lambda j: (0, j)),
        out_shape=jax.ShapeDtypeStruct((d, n_cols), BF16),
        compiler_params=_cparams("parallel"),
        name=name,
    )(w_in)


NORM_ROWS = 32


def _stream_specs(h_src, tm, n_lat, n_tiles):
    nl = n_lat // tm
    if isinstance(h_src, tuple) and n_tiles > nl:
        d = h_src[0].shape[1]
        return list(h_src), [pl.BlockSpec((tm, d), lambda i: (jnp.minimum(i, nl - 1), 0)),
                             pl.BlockSpec((tm, d), lambda i: (jnp.maximum(i - nl, 0), 0))]
    arr = h_src[0] if isinstance(h_src, tuple) else h_src
    return [arr], [pl.BlockSpec((tm, arr.shape[1]), lambda i: (i, 0))]


def _read_stream(h_refs, rows, n_lat_tiles):
    if len(h_refs) == 1:
        return h_refs[0][rows, :]
    return jnp.where(pl.program_id(0) < n_lat_tiles, h_refs[0][rows, :], h_refs[1][rows, :])


def _norm_modulate_rows(h_refs, n_lat_tiles, g_ref, mod_ref, shift_row, scale_row, out_ref):
    g = g_ref[...]
    sc = 1.0 + mod_ref[0, scale_row:scale_row + 1, :]
    sh = mod_ref[0, shift_row:shift_row + 1, :]

    def body(i, carry):
        rows = pl.ds(pl.multiple_of(i * NORM_ROWS, NORM_ROWS), NORM_ROWS)
        x = _read_stream(h_refs, rows, n_lat_tiles)
        ms = jnp.mean(x * x, axis=-1, keepdims=True)
        y = (x * lax.rsqrt(ms + EPS) * g) * sc + sh
        out_ref[rows, :] = y.astype(out_ref.dtype)
        return carry
    lax.fori_loop(0, out_ref.shape[0] // NORM_ROWS, body, 0)


def _norm_kernel(*refs, n_lat_tiles):
    mod_ref, g_ref, u_ref = refs[-3:]
    _norm_modulate_rows(refs[:-3], n_lat_tiles, g_ref, mod_ref, 0, 1, u_ref)


def _project_tile(u_ref, w_ref, epilogue):
    tm, tn = u_ref.shape[0], w_ref.shape[1]
    rc = min(PROJ_RC, tm)
    pending = []
    for c in range(tn // PROJ_CW):
        cols = slice(c * PROJ_CW, (c + 1) * PROJ_CW)
        accs = []
        for r in range(tm // rc):
            rows = slice(r * rc, (r + 1) * rc)
            accs.append((rows, cols, _dot(u_ref[rows, :], w_ref[:, cols])))
        for item in pending:
            epilogue(*item)
        pending = accs
    for item in pending:
        epilogue(*item)


def _inproj_qk_kernel(u_ref, w_ref, gc_ref, cos_ref, sin_ref, qk_ref):
    last_tile = pl.program_id(1) == pl.num_programs(1) - 1
    tn = w_ref.shape[1]

    def epilogue(rows, cols, acc):
        plain = jnp.logical_and(last_tile, cols.stop == tn)
        for c in range(PROJ_CW // HEAD_DIM):
            sl = slice(cols.start + c * HEAD_DIM, cols.start + (c + 1) * HEAD_DIM)
            x = acc[:, c * HEAD_DIM:(c + 1) * HEAD_DIM]
            y = x * lax.rsqrt(_lane_mean(x * x) + EPS) * gc_ref[:, sl]
            y = y * cos_ref[rows, :] + pltpu.roll(y, HEAD_DIM // 2, 1) * sin_ref[rows, :]
            qk_ref[rows, sl] = jnp.where(plain, x, y).astype(BF16)

    _project_tile(u_ref, w_ref, epilogue)


def _inproj_vb_kernel(u_ref, w_ref, vb_ref):
    def epilogue(rows, cols, acc):
        vb_ref[rows, cols] = acc.astype(BF16)

    _project_tile(u_ref, w_ref, epilogue)


def _norm_modulate_stream(h_src, mod, g1, *, n_all, tm, n_lat_tiles, tiles_per_seq, batch):
    d = g1.shape[1]

    def mod_idx(i):
        return (jnp.where(i < n_lat_tiles, i // tiles_per_seq, batch), 0, 0)

    h_args, h_specs = _stream_specs(h_src, tm, n_lat_tiles * tm, n_all // tm)
    return pl.pallas_call(
        functools.partial(_norm_kernel, n_lat_tiles=n_lat_tiles),
        grid=(n_all // tm,),
        in_specs=h_specs + [
            pl.BlockSpec((1, N_MOD, d), mod_idx),
            pl.BlockSpec((1, d), lambda i: (0, 0)),
        ],
        out_specs=pl.BlockSpec((tm, d), lambda i: (i, 0)),
        out_shape=jax.ShapeDtypeStruct((n_all, d), BF16),
        compiler_params=_cparams("parallel"),
        name="norm_modulate",
    )(*h_args, mod, g1)


def _in_projection(u, w_qk, w_vb, gcol, rope, *, tm, n_lat_tiles, tiles_per_seq):
    n_all, d = u.shape
    cos, sin = rope

    def rope_idx(i, j):
        return (jnp.where(i < n_lat_tiles, i % tiles_per_seq, tiles_per_seq), 0)

    qk = pl.pallas_call(
        _inproj_qk_kernel,
        grid=(n_all // tm, QK_COLS // QK_TN),
        in_specs=[
            pl.BlockSpec((tm, d), lambda i, j: (i, 0)),
            pl.BlockSpec((d, QK_TN), lambda i, j: (0, j)),
            pl.BlockSpec((1, QK_TN), lambda i, j: (0, j)),
            pl.BlockSpec((tm, HEAD_DIM), rope_idx),
            pl.BlockSpec((tm, HEAD_DIM), rope_idx),
        ],
        out_specs=pl.BlockSpec((tm, QK_TN), lambda i, j: (i, j)),
        out_shape=jax.ShapeDtypeStruct((n_all, QK_COLS), BF16),
        compiler_params=_cparams("parallel", "arbitrary"),
        name="in_projection_qk",
    )(u, w_qk, gcol, cos, sin)
    vb = pl.pallas_call(
        _inproj_vb_kernel,
        grid=(n_all // tm, VB_COLS // VB_TN),
        in_specs=[
            pl.BlockSpec((tm, d), lambda i, j: (i, 0)),
            pl.BlockSpec((d, VB_TN), lambda i, j: (0, j)),
        ],
        out_specs=pl.BlockSpec((tm, VB_TN), lambda i, j: (i, j)),
        out_shape=jax.ShapeDtypeStruct((n_all, VB_COLS), BF16),
        compiler_params=_cparams("parallel", "arbitrary"),
        name="in_projection_vb",
    )(u, w_vb)
    return qk, vb


ATT_KC = 256
SAFE_SHIFT = 40.0


def _key_chunks(k_refs, v_refs):
    chunks = []
    for k_ref, v_ref in zip(k_refs, v_refs):
        n = k_ref.shape[0]
        for s in range(0, n, ATT_KC):
            chunks.append((k_ref, v_ref, s, min(ATT_KC, n - s)))
    return chunks


def _fill_shifts(bound_ref, m_scr, q_ref, chunks, heads):
    bound = bound_ref[0]

    @pl.when(bound <= SAFE_SHIFT)
    def _():
        m_scr[...] = jnp.full(m_scr.shape, bound, F32)

    @pl.when(bound > SAFE_SHIFT)
    def _():
        for idx, (qcol, kcol) in enumerate(heads):
            q = q_ref[:, qcol:qcol + HEAD_DIM]
            m = None
            for k_ref, _, s, n in chunks:
                part = jnp.max(_dot_nt(q, k_ref[s:s + n, kcol:kcol + HEAD_DIM]), axis=-1, keepdims=True)
                m = part if m is None else jnp.maximum(m, part)
            m_scr[idx] = jnp.broadcast_to(m, m_scr.shape[1:])


def _stream_softmax_pv(q, shift, chunks, kcol, vcols):
    o = None
    l = None
    for k_ref, v_ref, s, n in chunks:
        sc = _dot_nt(q, k_ref[s:s + n, kcol:kcol + HEAD_DIM])
        tiles = [jnp.exp(sc[:, t * LANES:(t + 1) * LANES] - shift) for t in range(n // LANES)]
        for e_t in tiles:
            l = e_t if l is None else l + e_t
        e = tiles[0] if len(tiles) == 1 else jnp.concatenate(tiles, axis=1)
        pv = _dot(e.astype(BF16), v_ref[s:s + n, vcols])
        o = pv if o is None else o + pv
    return o, jnp.sum(l, axis=-1, keepdims=True)


def _diff_attn_kernel(bound_ref, dl_ref, sg_ref, q_ref, *refs, lambda_init, n_pieces):
    k_refs = refs[0:2 * n_pieces:2]
    v_refs = refs[1:2 * n_pieces:2]
    o_ref = refs[2 * n_pieces]
    m_scr = refs[2 * n_pieces + 1]
    chunks = _key_chunks(k_refs, v_refs)
    heads = [(s * HEAD_DIM, s * HEAD_DIM) for s in range(2 * DIFF_HEADS)]
    _fill_shifts(bound_ref, m_scr, q_ref, chunks, heads)
    dl = dl_ref[...]
    lam = (jnp.exp(jnp.sum(dl[0:1] * dl[1:2], axis=-1, keepdims=True))
           - jnp.exp(jnp.sum(dl[2:3] * dl[3:4], axis=-1, keepdims=True)) + lambda_init)
    for h in range(DIFF_HEADS):
        c1 = 2 * h * HEAD_DIM
        c2 = c1 + HEAD_DIM
        vs = slice(h * DIFF_V_DIM, (h + 1) * DIFF_V_DIM)
        o1, l1 = _stream_softmax_pv(q_ref[:, c1:c1 + HEAD_DIM], m_scr[2 * h], chunks, c1, vs)
        o2, l2 = _stream_softmax_pv(q_ref[:, c2:c2 + HEAD_DIM], m_scr[2 * h + 1], chunks, c2, vs)
        o = o1 * (1.0 / l1) - o2 * (lam / l2)
        ms = jnp.mean(o * o, axis=-1, keepdims=True)
        o = o * lax.rsqrt(ms + EPS) * sg_ref[...] * (1.0 - lambda_init)
        o_ref[:, vs] = o.astype(BF16)


def _gqa_kernel(bound_ref, q_ref, *refs, n_pieces):
    k_refs = refs[0:2 * n_pieces:2]
    v_refs = refs[1:2 * n_pieces:2]
    o_ref = refs[2 * n_pieces]
    m_scr = refs[2 * n_pieces + 1]
    chunks = _key_chunks(k_refs, v_refs)
    group = GQA_HEADS // GQA_KV_HEADS
    heads = [(h * HEAD_DIM, (h // group) * HEAD_DIM) for h in range(GQA_HEADS)]
    _fill_shifts(bound_ref, m_scr, q_ref, chunks, heads)
    for h, (qcol, kcol) in enumerate(heads):
        o, l = _stream_softmax_pv(q_ref[:, qcol:qcol + HEAD_DIM], m_scr[h], chunks, kcol,
                                  slice(kcol, kcol + HEAD_DIM))
        o_ref[:, qcol:qcol + HEAD_DIM] = (o * (1.0 / l)).astype(BF16)


def _attention(kind, qk, vb, bound, extra, *, batch, seq, ctx, tq, latent, lambda_init=None):
    n_lat = batch * seq
    ctx_blk0 = n_lat // ctx
    if kind == "diff":
        qcol, kcol, vcol, kvw, v_arr = QK_AQ // 1024, QK_AK // 1024, VB_AV // 1024, 1024, vb
    else:
        qcol, kcol, vcol, kvw, v_arr = QK_CQ // 1024, QK_CK // 256, QK_CV // 256, 256, qk
    if latent:
        q_tiles = seq // tq
        q_spec = pl.BlockSpec((tq, 1024), lambda b, i: (b * q_tiles + i, qcol))
        o_spec = pl.BlockSpec((tq, 1024), lambda b, i: (b * q_tiles + i, 0))
        n_q = n_lat
    else:
        q_tiles = 1
        tq = ctx
        q_spec = pl.BlockSpec((ctx, 1024), lambda b, i: (ctx_blk0 + b, qcol))
        o_spec = pl.BlockSpec((ctx, 1024), lambda b, i: (b, 0))
        n_q = batch * ctx
    kv_specs = [
        pl.BlockSpec((ctx, kvw), lambda b, i: (ctx_blk0 + b, kcol)),
        pl.BlockSpec((ctx, kvw), lambda b, i: (ctx_blk0 + b, vcol)),
    ]
    kv_args = [qk, v_arr]
    if latent:
        kv_specs += [
            pl.BlockSpec((seq, kvw), lambda b, i: (b, kcol)),
            pl.BlockSpec((seq, kvw), lambda b, i: (b, vcol)),
        ]
        kv_args += [qk, v_arr]
    n_pieces = len(kv_args) // 2
    pre_specs = [pl.BlockSpec(memory_space=pltpu.SMEM)]
    pre_args = [bound]
    if kind == "diff":
        diff_lambda, sub_g = extra
        body = functools.partial(_diff_attn_kernel, lambda_init=lambda_init, n_pieces=n_pieces)
        pre_specs += [
            pl.BlockSpec((4, HEAD_DIM), lambda b, i: (0, 0)),
            pl.BlockSpec((1, DIFF_V_DIM), lambda b, i: (0, 0)),
        ]
        pre_args += [diff_lambda, sub_g]
        n_softmax = 2 * DIFF_HEADS
    else:
        body = functools.partial(_gqa_kernel, n_pieces=n_pieces)
        n_softmax = GQA_HEADS
    return pl.pallas_call(
        body,
        grid=(batch, q_tiles),
        in_specs=pre_specs + [q_spec] + kv_specs,
        out_specs=o_spec,
        out_shape=jax.ShapeDtypeStruct((n_q, 1024), BF16),
        scratch_shapes=[pltpu.VMEM((n_softmax, tq, LANES), F32)],
        compiler_params=_cparams("parallel", "arbitrary"),
        name=f"{kind}_attn_{'lat' if latent else 'ctx'}",
    )(*pre_args, qk, *kv_args)


LRU_CW = 512
LRU_PAD = 8
SQRT_GUARD = 1e-30


def _lru_kernel(bxc_ref, byc_ref, bxl_ref, byl_ref, cw_ref, cb_ref, wa_ref, ba_ref, wx_ref, bx_ref, lam_ref,
                *refs, ctx, seq, need_ctx):
    if need_ctx:
        yc_ref, yl_ref = refs[0], refs[1]
        scratch = refs[2:]
    else:
        yc_ref, yl_ref = None, refs[0]
        scratch = refs[1:]
    xpad, a_f, b_f, a_b, b_b = scratch
    a_scr = (a_f, a_b)
    b_scr = (b_f, b_b)
    lam = lam_ref[...]
    neg_sp = -LRU_C * jax.nn.softplus(-lam)

    def gates(x_ref, n):
        zeros = jnp.zeros((LRU_PAD, LRU_CW), F32)
        xpad[0:LRU_PAD, :] = zeros
        xpad[LRU_PAD:LRU_PAD + n, :] = x_ref[...].astype(F32)
        xpad[LRU_PAD + n:2 * LRU_PAD + n, :] = zeros
        xc = cb_ref[...] + cw_ref[0:1, :] * xpad[LRU_PAD - 2:LRU_PAD - 2 + n, :]
        for k in range(1, CONV_W):
            xc = xc + cw_ref[k:k + 1, :] * xpad[LRU_PAD - 2 + k:LRU_PAD - 2 + k + n, :]
        xcb = xc.astype(BF16)
        for d in range(2):
            for blk in range(LRU_CW // LRU_BLOCK_W):
                sl = slice(blk * LRU_BLOCK_W, (blk + 1) * LRU_BLOCK_W)
                xs = xcb[:, sl]
                r = _sigmoid(_dot(xs, wa_ref[d, blk].astype(BF16)) + ba_ref[d:d + 1, sl])
                gi = _sigmoid(_dot(xs, wx_ref[d, blk].astype(BF16)) + bx_ref[d:d + 1, sl])
                log_a = r * neg_sp[d:d + 1, sl]
                a = jnp.exp(log_a)
                one_m_a2 = 1.0 - a * a
                mult = one_m_a2 * lax.rsqrt(jnp.maximum(one_m_a2, SQRT_GUARD))
                a_scr[d][0:n, sl] = a
                b_scr[d][0:n, sl] = mult * gi * xc[:, sl]

    def scan(n, hf0, hb0):
        def body(g, carry):
            hf, hb = carry
            t0 = pl.multiple_of(g * 8, 8)
            tb0 = pl.multiple_of(n - 8 - g * 8, 8)
            for k in range(8):
                rf = pl.ds(t0 + k, 1)
                rb = pl.ds(tb0 + (7 - k), 1)
                hf = a_f[rf, :] * hf + b_f[rf, :]
                b_f[rf, :] = hf
                hb = a_b[rb, :] * hb + b_b[rb, :]
                b_b[rb, :] = hb
            return hf, hb
        return lax.fori_loop(0, n // 8, body, (hf0, hb0))

    zero = jnp.zeros((1, LRU_CW), F32)
    gates(bxc_ref, ctx)
    hf, hb = scan(ctx, zero, zero)
    if need_ctx:
        yc_ref[...] = ((b_f[0:ctx, :] + b_b[0:ctx, :])
                       * jax.nn.gelu(byc_ref[...].astype(F32), approximate=True)).astype(BF16)
    gates(bxl_ref, seq)
    scan(seq, hf, hb)
    yl_ref[...] = ((b_f[0:seq, :] + b_b[0:seq, :])
                   * jax.nn.gelu(byl_ref[...].astype(F32), approximate=True)).astype(BF16)


def _rglru(vb, conv_w, conv_b, w_a, b_a, w_x, b_x, lam, *, batch, seq, ctx, need_ctx):
    n_lat = batch * seq
    ctx_blk0 = n_lat // ctx
    cbx, cby = VB_BX // LRU_CW, VB_BY // LRU_CW
    nblk = LRU_CW // LRU_BLOCK_W
    tmax = max(seq, ctx)
    out_specs = [pl.BlockSpec((seq, LRU_CW), lambda b, c: (b, c))]
    out_shape = [jax.ShapeDtypeStruct((n_lat, LRU_WIDTH), BF16)]
    if need_ctx:
        out_specs = [pl.BlockSpec((ctx, LRU_CW), lambda b, c: (b, c))] + out_specs
        out_shape = [jax.ShapeDtypeStruct((batch * ctx, LRU_WIDTH), BF16)] + out_shape
    outs = pl.pallas_call(
        functools.partial(_lru_kernel, ctx=ctx, seq=seq, need_ctx=need_ctx),
        grid=(batch, LRU_WIDTH // LRU_CW),
        in_specs=[
            pl.BlockSpec((ctx, LRU_CW), lambda b, c: (ctx_blk0 + b, cbx + c)),
            pl.BlockSpec((ctx, LRU_CW), lambda b, c: (ctx_blk0 + b, cby + c)),
            pl.BlockSpec((seq, LRU_CW), lambda b, c: (b, cbx + c)),
            pl.BlockSpec((seq, LRU_CW), lambda b, c: (b, cby + c)),
            pl.BlockSpec((CONV_W, LRU_CW), lambda b, c: (0, c)),
            pl.BlockSpec((1, LRU_CW), lambda b, c: (0, c)),
            pl.BlockSpec((2, nblk, LRU_BLOCK_W, LRU_BLOCK_W), lambda b, c: (0, c, 0, 0)),
            pl.BlockSpec((2, LRU_CW), lambda b, c: (0, c)),
            pl.BlockSpec((2, nblk, LRU_BLOCK_W, LRU_BLOCK_W), lambda b, c: (0, c, 0, 0)),
            pl.BlockSpec((2, LRU_CW), lambda b, c: (0, c)),
            pl.BlockSpec((2, LRU_CW), lambda b, c: (0, c)),
        ],
        out_specs=out_specs,
        out_shape=out_shape,
        scratch_shapes=[
            pltpu.VMEM((tmax + 2 * LRU_PAD, LRU_CW), F32),
            pltpu.VMEM((tmax, LRU_CW), F32),
            pltpu.VMEM((tmax, LRU_CW), F32),
            pltpu.VMEM((tmax, LRU_CW), F32),
            pltpu.VMEM((tmax, LRU_CW), F32),
        ],
        compiler_params=_cparams("parallel", "arbitrary"),
        name="rglru",
    )(vb, vb, vb, vb, conv_w, conv_b, w_a, b_a, w_x, b_x, lam)
    if need_ctx:
        return outs[1], outs[0]
    return outs[0], None


MERGE_TN = 256


def _merge_kernel(u_ref, *refs, n_lat_tiles, has_ctx):
    n_y = 6 if has_ctx else 3
    y_refs = refs[:n_y]
    wg_refs = refs[n_y:n_y + 3]
    bg_ref = refs[n_y + 3]
    wbr_refs = refs[n_y + 4:n_y + 7]
    m_ref = refs[n_y + 7]

    def body(ys):
        u = u_ref[...]
        m = None
        for k in range(N_BRANCH):
            g = jax.nn.sigmoid(_dot(u, wg_refs[k][...]) + bg_ref[k:k + 1, :])
            part = g * _dot(ys[k][...], wbr_refs[k][...])
            m = part if m is None else m + part
        m_ref[...] = m.astype(BF16)

    if not has_ctx:
        body(y_refs)
        return
    is_lat = pl.program_id(0) < n_lat_tiles

    @pl.when(is_lat)
    def _():
        body(y_refs[0:3])

    @pl.when(jnp.logical_not(is_lat))
    def _():
        body(y_refs[3:6])


def _merge(u, y_lat, y_ctx, w_gate, b_gate, w_branch, *, n_tok, n_lat, tm):
    d = u.shape[1]
    nj = d // MERGE_TN
    nl = n_lat // tm
    has_ctx = y_ctx is not None
    y_specs = [pl.BlockSpec((tm, BRANCH_W), lambda i, j: (jnp.minimum(i, nl - 1), 0))] * 3
    y_args = list(y_lat)
    if has_ctx:
        y_specs += [pl.BlockSpec((tm, BRANCH_W), lambda i, j: (jnp.maximum(i - nl, 0), 0))] * 3
        y_args += list(y_ctx)
    wbr_spec = pl.BlockSpec((BRANCH_W, MERGE_TN), lambda i, j: (0, j))
    return pl.pallas_call(
        functools.partial(_merge_kernel, n_lat_tiles=nl, has_ctx=has_ctx),
        grid=(n_tok // tm, nj),
        in_specs=[pl.BlockSpec((tm, d), lambda i, j: (i, 0))] + y_specs + [
            pl.BlockSpec((d, MERGE_TN), lambda i, j: (0, j)),
            pl.BlockSpec((d, MERGE_TN), lambda i, j: (0, nj + j)),
            pl.BlockSpec((d, MERGE_TN), lambda i, j: (0, 2 * nj + j)),
            pl.BlockSpec((N_BRANCH, MERGE_TN), lambda i, j: (0, j)),
            wbr_spec, wbr_spec, wbr_spec,
        ],
        out_specs=pl.BlockSpec((tm, MERGE_TN), lambda i, j: (i, j)),
        out_shape=jax.ShapeDtypeStruct((n_tok, d), BF16),
        compiler_params=_cparams("parallel", "arbitrary"),
        name="branch_merge",
    )(u, *y_args, w_gate, w_gate, w_gate, b_gate, *w_branch)


OUT_RC = 256
ROUTE_E, ROUTE_P, ROUTE_RANK = 0, 2, 4
COUNT_ROWS = 8


def _outproj_kernel(m_ref, *refs, n_lat_tiles):
    mod_ref, g_ref, wo_ref, wr_ref, br_ref, h1_ref, vp_ref, route_ref, cnt_ref, v_scr = refs[-10:]
    h_refs = refs[:-10]
    tm, d = h1_ref.shape
    rc = min(OUT_RC, tm)
    for r in range(tm // rc):
        rows = slice(r * rc, (r + 1) * rc)
        h1 = (_read_stream(h_refs, rows, n_lat_tiles)
              + mod_ref[0, 2:3, :] * _dot(m_ref[rows, :], wo_ref[...]))
        h1_ref[rows, :] = h1
        ms = jnp.mean(h1 * h1, axis=-1, keepdims=True)
        y = h1 * lax.rsqrt(ms + EPS) * g_ref[...]
        v_scr[rows, :] = (y * (1.0 + mod_ref[0, 4:5, :]) + mod_ref[0, 3:4, :]).astype(BF16)
    v = v_scr[...]
    half = d // 2
    lo = lax.bitcast_convert_type(v[:, :half].astype(F32), U32)
    hi = lax.bitcast_convert_type(v[:, half:].astype(F32), U32)
    vp_ref[...] = (lo >> 16) | (hi & jnp.uint32(0xFFFF0000))

    logits = _dot(v, wr_ref[...]) + br_ref[...]
    lane = lax.broadcasted_iota(I32, logits.shape, 1).astype(F32)
    neg = jnp.float32(-jnp.inf)
    big = jnp.float32(1e9)

    def masked_argmax(mask):
        val = jnp.max(jnp.where(mask, logits, neg), axis=-1, keepdims=True)
        idx = jnp.min(jnp.where(jnp.logical_and(mask, logits == val), lane, big), axis=-1, keepdims=True)
        return val, idx

    gmask = lane < N_GROUPS
    gmax, gidx = masked_argmax(gmask)
    gsum = jnp.sum(jnp.where(gmask, jnp.exp(logits - gmax), 0.0), axis=-1, keepdims=True)
    g_w = 1.0 / gsum
    lo_lane = N_GROUPS + EXPERTS_PER_GROUP * gidx
    emask = jnp.logical_and(lane >= lo_lane, lane < lo_lane + EXPERTS_PER_GROUP)
    v1, i1 = masked_argmax(emask)
    v2, i2 = masked_argmax(jnp.logical_and(emask, lane != i1))
    e21 = jnp.exp(v2 - v1)
    p1 = g_w / (1.0 + e21)
    p2 = g_w * e21 / (1.0 + e21)
    e1 = i1 - N_GROUPS
    e2 = i2 - N_GROUPS
    oh1 = jnp.where(lane == e1, 1.0, 0.0)
    oh2 = jnp.where(lane == e2, 1.0, 0.0)
    cnt = oh1 + oh2
    row = lax.broadcasted_iota(I32, (tm, tm), 0)
    col = lax.broadcasted_iota(I32, (tm, tm), 1)
    before = jnp.where(row > col, 1.0, 0.0).astype(BF16)
    prefix = _dot(before, cnt.astype(BF16))
    rank1 = jnp.sum(prefix * oh1, axis=-1, keepdims=True)
    rank2 = jnp.sum(prefix * oh2, axis=-1, keepdims=True)
    route = jnp.where(lane == 0, e1, jnp.where(lane == 1, e2, jnp.where(lane == 2, p1, jnp.where(
        lane == 3, p2, jnp.where(lane == 4, rank1, jnp.where(lane == 5, rank2, 0.0))))))
    route_ref[...] = route
    cnt_ref[0] = jnp.broadcast_to(jnp.sum(cnt, axis=0, keepdims=True), (COUNT_ROWS, LANES))


def _out_projection(m, h_src, mod, g2, w_out, w_router, b_router, *, n_tok, tm, n_lat, seq, batch):
    d = m.shape[1]
    n_tiles = n_tok // tm

    def mod_idx(i):
        return (jnp.where(i * tm < n_lat, (i * tm) // seq, batch), 0, 0)

    h_args, h_specs = _stream_specs(h_src, tm, n_lat, n_tiles)
    return pl.pallas_call(
        functools.partial(_outproj_kernel, n_lat_tiles=n_lat // tm),
        grid=(n_tiles,),
        in_specs=[pl.BlockSpec((tm, d), lambda i: (i, 0))] + h_specs + [
            pl.BlockSpec((1, N_MOD, d), mod_idx),
            pl.BlockSpec((1, d), lambda i: (0, 0)),
            pl.BlockSpec((d, d), lambda i: (0, 0), pipeline_mode=pl.Buffered(1)),
            pl.BlockSpec((d, LANES), lambda i: (0, 0)),
            pl.BlockSpec((1, LANES), lambda i: (0, 0)),
        ],
        out_specs=[
            pl.BlockSpec((tm, d), lambda i: (i, 0)),
            pl.BlockSpec((tm, d // 2), lambda i: (i, 0)),
            pl.BlockSpec((tm, LANES), lambda i: (i, 0)),
            pl.BlockSpec((1, COUNT_ROWS, LANES), lambda i: (i, 0, 0)),
        ],
        out_shape=[
            jax.ShapeDtypeStruct((n_tok, d), F32),
            jax.ShapeDtypeStruct((n_tok, d // 2), U32),
            jax.ShapeDtypeStruct((n_tok, LANES), F32),
            jax.ShapeDtypeStruct((n_tiles, COUNT_ROWS, LANES), F32),
        ],
        scratch_shapes=[pltpu.VMEM((tm, d), BF16)],
        compiler_params=_cparams("parallel"),
        name="out_projection_router",
    )(m, *h_args, mod, g2, w_out, w_router, b_router)


MOE_TM = 256
CAST_ROWS = 64
ROW_TM = 512


def _dispatch(route, counts, n_tok, tm_route):
    n_tiles = (2 * n_tok) // MOE_TM + N_EXPERTS
    counts = counts[:, 0, :N_EXPERTS].astype(I32)
    tile_base = jnp.cumsum(counts, axis=0) - counts
    total = jnp.sum(counts, axis=0)
    padded = ((total + MOE_TM - 1) // MOE_TM) * MOE_TM
    seg_end = jnp.cumsum(padded)
    base = (seg_end - padded)[None, :] + tile_base
    r3 = route.reshape(n_tok // tm_route, tm_route, LANES)
    experts = jnp.arange(N_EXPERTS, dtype=I32)
    dest = []
    for k in range(2):
        hit = r3[:, :, ROUTE_E + k].astype(I32)[:, :, None] == experts
        picked = jnp.sum(jnp.where(hit, base[:, None, :], 0), axis=-1)
        dest.append((picked + r3[:, :, ROUTE_RANK + k].astype(I32)).reshape(n_tok // ROW_TM, 1, ROW_TM))
    tile_start = jnp.arange(n_tiles, dtype=I32) * MOE_TM
    n_used = seg_end[-1] // MOE_TM
    tile_valid = (tile_start < seg_end[-1]).astype(I32)
    tile_index = jnp.minimum(jnp.arange(n_tiles, dtype=I32), n_used - 1)
    tile_expert = jnp.sum((seg_end[None, :] <= (tile_index * MOE_TM)[:, None]).astype(I32), axis=1)
    prev = jnp.concatenate([jnp.full((1,), -1, I32), tile_expert[:-1]])
    tile_first = (tile_expert != prev).astype(I32)
    later_first = jnp.logical_and(tile_first == 1, tile_valid == 1)
    pos = jnp.arange(n_tiles, dtype=I32)
    first_pos = jnp.where(later_first, pos, n_tiles)
    next_pos = jnp.min(jnp.where(pos[None, :] > pos[:, None], first_pos[None, :], n_tiles), axis=1)
    tile_next = jnp.where(next_pos < n_tiles, tile_expert[jnp.minimum(next_pos, n_tiles - 1)], -1).astype(I32)
    tails = jnp.maximum(seg_end - MOE_TM, 0).astype(I32)
    return dest, (tile_expert, tile_first, tile_valid, tile_index, tile_next), tails


def _row_scatter_kernel(tail_ref, d0_ref, d1_ref, v_ref, x_hbm, zbuf, sem_z, sem):
    @pl.when(pl.program_id(0) == 0)
    def _():
        zbuf[...] = jnp.zeros_like(zbuf)
        def zero_tile(start):
            start = pl.multiple_of(start, MOE_TM)
            return pltpu.make_async_copy(zbuf, x_hbm.at[pl.ds(start, MOE_TM), :], sem_z)
        for e in range(N_EXPERTS):
            zero_tile(tail_ref[e]).start()
        for e in range(N_EXPERTS):
            zero_tile(tail_ref[e]).wait()
        first_unused = tail_ref[N_EXPERTS - 1] // MOE_TM + 1
        n_tiles = x_hbm.shape[0] // MOE_TM

        def start_unused(t, carry):
            zero_tile(t * MOE_TM).start()
            return carry

        def wait_unused(t, carry):
            zero_tile(t * MOE_TM).wait()
            return carry
        lax.fori_loop(first_unused, n_tiles, start_unused, 0)
        lax.fori_loop(first_unused, n_tiles, wait_unused, 0)

    def row_copies(g, k, d0, d1):
        src = v_ref.at[pl.ds(pl.multiple_of(g * 8, 8) + k, 1), :]
        return (pltpu.make_async_copy(src, x_hbm.at[pl.ds(d0, 1), :], sem),
                pltpu.make_async_copy(src, x_hbm.at[pl.ds(d1, 1), :], sem))

    def issue(g, carry):
        for k in range(8):
            for copy in row_copies(g, k, d0_ref[0, 0, g * 8 + k], d1_ref[0, 0, g * 8 + k]):
                copy.start()
        return carry
    lax.fori_loop(0, ROW_TM // 8, issue, 0)

    def drain(g, carry):
        for k in range(8):
            for copy in row_copies(g, k, 0, 0):
                copy.wait()
        return carry
    lax.fori_loop(0, ROW_TM // 8, drain, 0)


def _row_scatter(vp, dest, tails, n_rows):
    n_tok, half = vp.shape
    n_steps = n_tok // ROW_TM
    d0, d1 = dest
    grid_spec = pltpu.PrefetchScalarGridSpec(
        num_scalar_prefetch=1,
        grid=(n_steps,),
        in_specs=[
            pl.BlockSpec((1, 1, ROW_TM), lambda i, t: (i, 0, 0), memory_space=pltpu.SMEM),
            pl.BlockSpec((1, 1, ROW_TM), lambda i, t: (i, 0, 0), memory_space=pltpu.SMEM),
            pl.BlockSpec((ROW_TM, half), lambda i, t: (i, 0)),
        ],
        out_specs=pl.BlockSpec(memory_space=pl.ANY),
        scratch_shapes=[
            pltpu.VMEM((MOE_TM, half), U32),
            pltpu.SemaphoreType.DMA(()),
            pltpu.SemaphoreType.DMA(()),
        ],
    )
    return pl.pallas_call(
        _row_scatter_kernel,
        grid_spec=grid_spec,
        out_shape=jax.ShapeDtypeStruct((n_rows, half), U32),
        compiler_params=_cparams("arbitrary"),
        name="moe_row_scatter",
    )(tails, d0, d1, vp)


def _moe_kernel(te_ref, tf_ref, tv_ref, ti_ref, tn_ref, x_ref, w1_hbm, w3_hbm, w2_hbm, y_ref,
                s1, s3, s2, w1b, w3b, w2b, sems, *, layer):
    i = pl.program_id(0)
    valid = tv_ref[i] == 1

    def weight_copies(e):
        return (pltpu.make_async_copy(w1_hbm.at[layer, e], s1, sems.at[0]),
                pltpu.make_async_copy(w3_hbm.at[layer, e], s3, sems.at[1]),
                pltpu.make_async_copy(w2_hbm.at[layer, e], s2, sems.at[2]))

    @pl.when(i == 0)
    def _():
        for copy in weight_copies(te_ref[0]):
            copy.start()

    @pl.when(valid)
    def _():
        @pl.when(tf_ref[i] == 1)
        def _():
            for copy in weight_copies(te_ref[i]):
                copy.wait()
            for src, dst in ((s1, w1b), (s3, w3b), (s2, w2b)):
                rows_per = CAST_ROWS * 512 // src.shape[1]

                def cast_rows(c, carry, src=src, dst=dst, rows_per=rows_per):
                    rows = pl.ds(pl.multiple_of(c * rows_per, rows_per), rows_per)
                    dst[rows, :] = src[rows, :].astype(BF16)
                    return carry
                lax.fori_loop(0, src.shape[0] // rows_per, cast_rows, 0)

            @pl.when(tn_ref[i] >= 0)
            def _():
                for copy in weight_copies(tn_ref[i]):
                    copy.start()

        half = x_ref.shape[1]
        xp = x_ref[...]
        lo = lax.bitcast_convert_type(xp << 16, F32).astype(BF16)
        hi = lax.bitcast_convert_type(xp & jnp.uint32(0xFFFF0000), F32).astype(BF16)
        h1 = _dot(lo, w1b[0:half, :]) + _dot(hi, w1b[half:, :])
        h3 = _dot(lo, w3b[0:half, :]) + _dot(hi, w3b[half:, :])
        hh = (h1 * jax.nn.sigmoid(h1) * h3).astype(BF16)
        y_ref[...] = _dot(hh, w2b[...])

    @pl.when(jnp.logical_not(valid))
    def _():
        y_ref[...] = jnp.zeros_like(y_ref)


def _moe_experts(layer, x_sorted, tiles, w1, w3, w2):
    tile_expert, tile_first, tile_valid, tile_index, tile_next = tiles
    n_tiles = tile_expert.shape[0]
    half = x_sorted.shape[1]
    d = 2 * half
    grid_spec = pltpu.PrefetchScalarGridSpec(
        num_scalar_prefetch=5,
        grid=(n_tiles,),
        in_specs=[
            pl.BlockSpec((MOE_TM, half), lambda i, te, tf, tv, ti, tn: (ti[i], 0)),
            pl.BlockSpec(memory_space=pl.ANY),
            pl.BlockSpec(memory_space=pl.ANY),
            pl.BlockSpec(memory_space=pl.ANY),
        ],
        out_specs=pl.BlockSpec((MOE_TM, d), lambda i, te, tf, tv, ti, tn: (i, 0)),
        scratch_shapes=[
            pltpu.VMEM((d, D_EXPERT), F32),
            pltpu.VMEM((d, D_EXPERT), F32),
            pltpu.VMEM((D_EXPERT, d), F32),
            pltpu.VMEM((d, D_EXPERT), BF16),
            pltpu.VMEM((d, D_EXPERT), BF16),
            pltpu.VMEM((D_EXPERT, d), BF16),
            pltpu.SemaphoreType.DMA((3,)),
        ],
    )
    return pl.pallas_call(
        functools.partial(_moe_kernel, layer=layer),
        grid_spec=grid_spec,
        out_shape=jax.ShapeDtypeStruct((n_tiles * MOE_TM, d), F32),
        compiler_params=_cparams("arbitrary"),
        name="moe_experts",
    )(tile_expert, tile_first, tile_valid, tile_index, tile_next, x_sorted, w1, w3, w2)


def _combine_kernel(d0_ref, d1_ref, n0_ref, n1_ref, h_ref, route_ref, mod_ref, *refs, with_next):
    if with_next:
        modn_ref, gn_ref, y_hbm, o_ref, u_ref, ybuf, sems = refs
    else:
        y_hbm, o_ref, ybuf, sems = refs
    i = pl.program_id(0)
    n_steps = pl.num_programs(0)

    def row_copies(slot, g, k, d0, d1):
        rows = pl.ds(pl.multiple_of(g * 8, 8) + k, 1)
        return (pltpu.make_async_copy(y_hbm.at[pl.ds(d0, 1), :], ybuf.at[slot, 0, rows, :], sems.at[slot]),
                pltpu.make_async_copy(y_hbm.at[pl.ds(d1, 1), :], ybuf.at[slot, 1, rows, :], sems.at[slot]))

    def request(slot, i0_ref, i1_ref):
        def body(g, carry):
            for k in range(8):
                for copy in row_copies(slot, g, k, i0_ref[0, 0, g * 8 + k], i1_ref[0, 0, g * 8 + k]):
                    copy.start()
            return carry
        lax.fori_loop(0, ROW_TM // 8, body, 0)

    def await_rows(slot):
        def body(g, carry):
            for k in range(8):
                for copy in row_copies(slot, g, k, 0, 0):
                    copy.wait()
            return carry
        lax.fori_loop(0, ROW_TM // 8, body, 0)

    slot = i % 2

    @pl.when(i == 0)
    def _():
        request(0, d0_ref, d1_ref)

    @pl.when(i + 1 < n_steps)
    def _():
        request(1 - slot, n0_ref, n1_ref)

    await_rows(slot)
    p0 = route_ref[:, ROUTE_P:ROUTE_P + 1]
    p1 = route_ref[:, ROUTE_P + 1:ROUTE_P + 2]
    o_ref[...] = h_ref[...] + mod_ref[0, 5:6, :] * (p0 * ybuf[slot, 0] + p1 * ybuf[slot, 1])
    if with_next:
        _norm_modulate_rows((o_ref,), 0, gn_ref, modn_ref, 0, 1, u_ref)


def _combine(h1, route, mod, y_sorted, dest, next_norm, *, n_tok, n_lat, seq, batch):
    d = h1.shape[1]
    tm = ROW_TM
    n_steps = n_tok // tm
    d0, d1 = dest
    with_next = next_norm is not None

    def mod_idx(i):
        return (jnp.where(i * tm < n_lat, (i * tm) // seq, batch), 0, 0)

    row_spec = pl.BlockSpec((tm, d), lambda i: (i, 0))
    mod_spec = pl.BlockSpec((1, N_MOD, d), mod_idx)
    next_specs = [mod_spec, pl.BlockSpec((1, d), lambda i: (0, 0))] if with_next else []
    next_args = list(next_norm) if with_next else []
    out = pl.pallas_call(
        functools.partial(_combine_kernel, with_next=with_next),
        grid=(n_steps,),
        in_specs=[
            pl.BlockSpec((1, 1, tm), lambda i: (i, 0, 0), memory_space=pltpu.SMEM),
            pl.BlockSpec((1, 1, tm), lambda i: (i, 0, 0), memory_space=pltpu.SMEM),
            pl.BlockSpec((1, 1, tm), lambda i: (jnp.minimum(i + 1, n_steps - 1), 0, 0), memory_space=pltpu.SMEM),
            pl.BlockSpec((1, 1, tm), lambda i: (jnp.minimum(i + 1, n_steps - 1), 0, 0), memory_space=pltpu.SMEM),
            row_spec,
            pl.BlockSpec((tm, LANES), lambda i: (i, 0)),
            mod_spec,
        ] + next_specs + [pl.BlockSpec(memory_space=pl.ANY)],
        out_specs=[row_spec, row_spec] if with_next else row_spec,
        out_shape=([jax.ShapeDtypeStruct((n_tok, d), F32), jax.ShapeDtypeStruct((n_tok, d), BF16)] if with_next
                   else jax.ShapeDtypeStruct((n_tok, d), F32)),
        scratch_shapes=[pltpu.VMEM((2, 2, tm, d), F32), pltpu.SemaphoreType.DMA((2,))],
        compiler_params=_cparams("arbitrary"),
        name="moe_combine",
    )(d0, d1, d0, d1, h1, route, mod, *next_args, y_sorted)
    return out if with_next else (out, None)


def _rope_tables(seq, tm):
    n_rows = seq // GRID_W
    row = jnp.repeat(jnp.arange(n_rows), GRID_W).astype(F32)
    col = jnp.tile(jnp.arange(GRID_W), n_rows).astype(F32)
    half = HEAD_DIM // 2
    inv = 1.0 / (ROPE_THETA ** (jnp.arange(0, half, 2, dtype=F32) / half))
    ang_r = row[:, None] * inv
    ang_c = col[:, None] * inv
    ang = jnp.concatenate([ang_r, ang_c, ang_r, ang_c], axis=-1)
    cos, sin = jnp.cos(ang), jnp.sin(ang)
    sin_signed = jnp.where(jnp.arange(HEAD_DIM) < half, -sin, sin)
    ident = jnp.zeros((tm, HEAD_DIM), F32)
    return jnp.concatenate([cos, ident + 1.0]), jnp.concatenate([sin_signed, ident])


def _permute_heads(a):
    lead = a.shape[:-1]
    quarter = HEAD_DIM // 4
    assert HEAD_PERM[quarter] == 2 * quarter and HEAD_PERM[2 * quarter] == quarter
    q = a.reshape(*lead, a.shape[-1] // HEAD_DIM, 4, quarter)
    q = jnp.stack([q[..., 0, :], q[..., 2, :], q[..., 1, :], q[..., 3, :]], axis=-2)
    return q.reshape(*lead, a.shape[-1])


def _score_bound(gq, gk):
    return (1.02 * HEAD_DIM * jnp.max(jnp.abs(gq)) * jnp.max(jnp.abs(gk))).reshape(1).astype(F32)


def kernel(x, c, ctx, c_ctx, w_ada, b_ada, norm1_g, norm2_g, w_in, b_gate, q_norm_a, k_norm_a, diff_lambda,
           sub_norm_a, q_norm_c, k_norm_c, conv_w, conv_b, lru_w_a, lru_b_a, lru_w_x, lru_b_x, lru_lambda,
           w_branch_a, w_branch_b, w_branch_c, w_out, w_group, b_group, w_route, b_route, w1, w3, w2):
    batch, seq, d = x.shape
    ctx_len = ctx.shape[1]
    depth = w_ada.shape[0]
    n_lat = batch * seq
    n_ctx = batch * ctx_len
    n_all = n_lat + n_ctx
    tm = min(1024, seq, n_ctx)
    assert seq % tm == 0 and n_ctx % tm == 0 and n_lat % ctx_len == 0 and batch < MOD_ROWS
    assert seq % GRID_W == 0 and seq % ROW_TM == 0 and n_ctx % ROW_TM == 0
    tq = min(256, seq)
    tm_out = min(512, tm)

    cc = jnp.zeros((MOD_ROWS, d), F32).at[:batch].set(c).at[batch].set(c_ctx)
    mod_all = _ada_modulation(cc, w_ada, b_ada).reshape(depth, MOD_ROWS, N_MOD, d)
    rope = _rope_tables(seq, tm)
    h_src = (x.reshape(n_lat, d), ctx.reshape(n_ctx, d))
    scale = HEAD_DIM ** -0.5

    for l in range(depth):
        last = l == depth - 1
        lambda_init = 0.8 - 0.6 * math.exp(-0.3 * l)
        mod = mod_all[l]
        t_av, t_cq, t_gate = 2048 // PREP_TN, 5120 // PREP_TN, 6656 // PREP_TN
        w_qk = _prepare_weight(w_in, l, QK_COLS, lambda j: jnp.where(j < t_av, j, j + t_cq - t_av),
                               QK_CV // PREP_TN, "prep_w_qk")
        w_vb = _prepare_weight(w_in, l, VB_COLS, lambda j: j + t_av, 0, "prep_w_vb")
        w_gate = _prepare_weight(w_in, l, N_BRANCH * d, lambda j: j + t_gate, 0, "prep_w_gate")
        gq_a, gq_c = q_norm_a[l] * scale, q_norm_c[l] * scale
        gcol = jnp.concatenate([
            _permute_heads(jnp.concatenate([
                jnp.tile(gq_a, 2 * DIFF_HEADS), jnp.tile(k_norm_a[l], 2 * DIFF_HEADS),
                jnp.tile(gq_c, GQA_HEADS), jnp.tile(k_norm_c[l], GQA_KV_HEADS)])),
            jnp.ones((QK_COLS - QK_CV,), F32)]).reshape(1, QK_COLS)
        bound_a = _score_bound(gq_a, k_norm_a[l])
        bound_c = _score_bound(gq_c, k_norm_c[l])

        if l == 0:
            u = _norm_modulate_stream(h_src, mod, norm1_g[l].reshape(1, d), n_all=n_all, tm=tm,
                                      n_lat_tiles=n_lat // tm, tiles_per_seq=seq // tm, batch=batch)
        qk, vb = _in_projection(u, w_qk, w_vb, gcol, rope, tm=tm, n_lat_tiles=n_lat // tm,
                                tiles_per_seq=seq // tm)
        dims = dict(batch=batch, seq=seq, ctx=ctx_len)
        extra_a = (diff_lambda[l], sub_norm_a[l].reshape(1, DIFF_V_DIM))
        ya = _attention("diff", qk, vb, bound_a, extra_a, tq=tq, latent=True, lambda_init=lambda_init, **dims)
        yc = _attention("gqa", qk, vb, bound_c, None, tq=tq, latent=True, **dims)
        yb, yb_c = _rglru(vb, conv_w[l], conv_b[l].reshape(1, LRU_WIDTH), lru_w_a[l], lru_b_a[l],
                          lru_w_x[l], lru_b_x[l], lru_lambda[l], need_ctx=not last, **dims)
        if last:
            n_tok, y_ctx = n_lat, None
        else:
            n_tok = n_all
            ya_c = _attention("diff", qk, vb, bound_a, extra_a, tq=tq, latent=False, lambda_init=lambda_init,
                              **dims)
            yc_c = _attention("gqa", qk, vb, bound_c, None, tq=tq, latent=False, **dims)
            y_ctx = (ya_c, yb_c, yc_c)
        w_branch = (w_branch_a[l].astype(BF16), w_branch_b[l].astype(BF16), w_branch_c[l].astype(BF16))
        m = _merge(u, (ya, yb, yc), y_ctx, w_gate, b_gate[l], w_branch, n_tok=n_tok, n_lat=n_lat, tm=tm)
        w_router = jnp.zeros((d, LANES), F32).at[:, :N_GROUPS].set(w_group[l])
        w_router = w_router.at[:, N_GROUPS:N_GROUPS + N_EXPERTS].set(w_route[l]).astype(BF16)
        b_router = jnp.zeros((1, LANES), F32).at[0, :N_GROUPS].set(b_group[l])
        b_router = b_router.at[0, N_GROUPS:N_GROUPS + N_EXPERTS].set(b_route[l])
        h1, vp, route, counts = _out_projection(m, h_src, mod, norm2_g[l].reshape(1, d),
                                                w_out[l].astype(BF16), w_router, b_router, n_tok=n_tok,
                                                tm=tm_out, n_lat=n_lat, seq=seq, batch=batch)
        dest, tiles, tails = _dispatch(route, counts, n_tok, tm_out)
        x_sorted = _row_scatter(vp, dest, tails, tiles[0].shape[0] * MOE_TM)
        y_sorted = _moe_experts(l, x_sorted, tiles, w1, w3, w2)
        next_norm = None if last else (mod_all[l + 1], norm1_g[l + 1].reshape(1, d))
        h_src, u = _combine(h1, route, mod, y_sorted, dest, next_norm, n_tok=n_tok, n_lat=n_lat, seq=seq,
                            batch=batch)
    return h_src.reshape(batch, seq, d)
```

```python
import functools
import math

import jax
import jax.numpy as jnp
from jax import lax
from jax.experimental import pallas as pl
from jax.experimental.pallas import tpu as pltpu

F32 = jnp.float32
BF16 = jnp.bfloat16
I32 = jnp.int32
U32 = jnp.uint32

EPS = 1e-6
HEAD_DIM = 128
GRID_W = 64
ROPE_THETA = 10000.0
DIFF_HEADS = 4
DIFF_V_DIM = 2 * HEAD_DIM
LRU_WIDTH = 1024
LRU_BLOCK_W = 128
CONV_W = 4
LRU_C = 8.0
GQA_HEADS = 8
GQA_KV_HEADS = 2
N_BRANCH = 3
BRANCH_W = 1024
N_GROUPS = 4
EXPERTS_PER_GROUP = 8
N_EXPERTS = N_GROUPS * EXPERTS_PER_GROUP
D_EXPERT = 512
N_MOD = 6
MOD_ROWS = 16
LANES = 128

QK_AQ, QK_AK, QK_CQ, QK_CK, QK_CV = 0, 1024, 2048, 3072, 3328
VB_AV, VB_BX, VB_BY = 0, 1024, 2048
QK_COLS = 3584
VB_COLS = 3072
QK_TN = QK_COLS // 2
VB_TN = VB_COLS // 2
PROJ_CW = 256
PROJ_RC = 512
HEAD_PERM = tuple(range(0, 32)) + tuple(range(64, 96)) + tuple(range(32, 64)) + tuple(range(96, 128))

VMEM_LIMIT = 56 * 1024 * 1024


def _cparams(*sem):
    return pltpu.CompilerParams(dimension_semantics=sem, vmem_limit_bytes=VMEM_LIMIT)


def _dot(a, b):
    return jnp.dot(a, b, preferred_element_type=F32)


def _dot_nt(a, b):
    return lax.dot_general(a, b, (((1,), (1,)), ((), ())), preferred_element_type=F32)


def _sigmoid(x):
    return 0.5 * jnp.tanh(0.5 * x) + 0.5


def _lane_mean(x2):
    return _dot(x2.astype(BF16), jnp.full((LANES, LANES), 1.0 / LANES, BF16))


def _ada_kernel(c_ref, w_ref, b_ref, o_ref):
    c = c_ref[...]
    s = (c * jax.nn.sigmoid(c)).astype(BF16)
    o_ref[0] = _dot(s, w_ref[0].astype(BF16)) + b_ref[0]


def _ada_modulation(cc, w_ada, b_ada):
    depth, d, n = w_ada.shape
    tn = 1024
    return pl.pallas_call(
        _ada_kernel,
        grid=(depth, n // tn),
        in_specs=[
            pl.BlockSpec((MOD_ROWS, d), lambda l, j: (0, 0)),
            pl.BlockSpec((1, d, tn), lambda l, j: (l, 0, j)),
            pl.BlockSpec((1, 1, tn), lambda l, j: (l, 0, j)),
        ],
        out_specs=pl.BlockSpec((1, MOD_ROWS, tn), lambda l, j: (l, 0, j)),
        out_shape=jax.ShapeDtypeStruct((depth, MOD_ROWS, n), F32),
        compiler_params=_cparams("parallel", "parallel"),
        name="ada_modulation",
    )(cc, w_ada, b_ada.reshape(depth, 1, n))


PREP_TN = 256


def _wprep_kernel(w_ref, o_ref, *, n_perm_tiles):
    def plain():
        o_ref[...] = w_ref[0].astype(BF16)

    def permuted():
        quarter = lax.broadcasted_iota(I32, (1, HEAD_DIM), 1) // (HEAD_DIM // 4)
        for c in range(PREP_TN // HEAD_DIM):
            sl = slice(c * HEAD_DIM, (c + 1) * HEAD_DIM)
            x = w_ref[0, :, sl]
            y = jnp.where(quarter == 1, pltpu.roll(x, 96, 1), jnp.where(quarter == 2, pltpu.roll(x, 32, 1), x))
            o_ref[:, sl] = y.astype(BF16)

    if n_perm_tiles == 0:
        plain()
        return
    is_perm = pl.program_id(0) < n_perm_tiles
    pl.when(is_perm)(permuted)
    pl.when(jnp.logical_not(is_perm))(plain)


def _prepare_weight(w_in, layer, n_cols, col_tile_of, n_perm_tiles, name):
    d = w_in.shape[1]
    return pl.pallas_call(
        functools.partial(_wprep_kernel, n_perm_tiles=n_perm_tiles),
        grid=(n_cols // PREP_TN,),
        in_specs=[pl.BlockSpec((1, d, PREP_TN), lambda j: (layer, 0, col_tile_of(j)))],
        out_specs=pl.BlockSpec((d, PREP_TN), lambda j: (0, j)),
        out_shape=jax.ShapeDtypeStruct((d, n_cols), BF16),
        compiler_params=_cparams("parallel"),
        name=name,
    )(w_in)


NORM_ROWS = 32


def _stream_specs(h_src, tm, n_lat, n_tiles):
    nl = n_lat // tm
    if isinstance(h_src, tuple) and n_tiles > nl:
        d = h_src[0].shape[1]
        return list(h_src), [pl.BlockSpec((tm, d), lambda i: (jnp.minimum(i, nl - 1), 0)),
                             pl.BlockSpec((tm, d), lambda i: (jnp.maximum(i - nl, 0), 0))]
    arr = h_src[0] if isinstance(h_src, tuple) else h_src
    return [arr], [pl.BlockSpec((tm, arr.shape[1]), lambda i: (i, 0))]


def _read_stream(h_refs, rows, n_lat_tiles):
    if len(h_refs) == 1:
        return h_refs[0][rows, :]
    return jnp.where(pl.program_id(0) < n_lat_tiles, h_refs[0][rows, :], h_refs[1][rows, :])


def _norm_modulate_rows(h_refs, n_lat_tiles, g_ref, mod_ref, shift_row, scale_row, out_ref):
    g = g_ref[...]
    sc = 1.0 + mod_ref[0, scale_row:scale_row + 1, :]
    sh = mod_ref[0, shift_row:shift_row + 1, :]

    def body(i, carry):
        rows = pl.ds(pl.multiple_of(i * NORM_ROWS, NORM_ROWS), NORM_ROWS)
        x = _read_stream(h_refs, rows, n_lat_tiles)
        ms = jnp.mean(x * x, axis=-1, keepdims=True)
        y = (x * lax.rsqrt(ms + EPS) * g) * sc + sh
        out_ref[rows, :] = y.astype(out_ref.dtype)
        return carry
    lax.fori_loop(0, out_ref.shape[0] // NORM_ROWS, body, 0)


def _norm_kernel(*refs, n_lat_tiles):
    mod_ref, g_ref, u_ref = refs[-3:]
    _norm_modulate_rows(refs[:-3], n_lat_tiles, g_ref, mod_ref, 0, 1, u_ref)


def _project_tile(u_ref, w_ref, epilogue):
    tm, tn = u_ref.shape[0], w_ref.shape[1]
    rc = min(PROJ_RC, tm)
    pending = []
    for c in range(tn // PROJ_CW):
        cols = slice(c * PROJ_CW, (c + 1) * PROJ_CW)
        accs = []
        for r in range(tm // rc):
            rows = slice(r * rc, (r + 1) * rc)
            accs.append((rows, cols, _dot(u_ref[rows, :], w_ref[:, cols])))
        for item in pending:
            epilogue(*item)
        pending = accs
    for item in pending:
        epilogue(*item)


def _inproj_qk_kernel(u_ref, w_ref, gc_ref, cos_ref, sin_ref, qk_ref):
    last_tile = pl.program_id(1) == pl.num_programs(1) - 1
    tn = w_ref.shape[1]

    def epilogue(rows, cols, acc):
        plain = jnp.logical_and(last_tile, cols.stop == tn)
        for c in range(PROJ_CW // HEAD_DIM):
            sl = slice(cols.start + c * HEAD_DIM, cols.start + (c + 1) * HEAD_DIM)
            x = acc[:, c * HEAD_DIM:(c + 1) * HEAD_DIM]
            y = x * lax.rsqrt(_lane_mean(x * x) + EPS) * gc_ref[:, sl]
            y = y * cos_ref[rows, :] + pltpu.roll(y, HEAD_DIM // 2, 1) * sin_ref[rows, :]
            qk_ref[rows, sl] = jnp.where(plain, x, y).astype(BF16)

    _project_tile(u_ref, w_ref, epilogue)


def _inproj_vb_kernel(u_ref, w_ref, vb_ref):
    def epilogue(rows, cols, acc):
        vb_ref[rows, cols] = acc.astype(BF16)

    _project_tile(u_ref, w_ref, epilogue)


def _norm_modulate_stream(h_src, mod, g1, *, n_all, tm, n_lat_tiles, tiles_per_seq, batch):
    d = g1.shape[1]

    def mod_idx(i):
        return (jnp.where(i < n_lat_tiles, i // tiles_per_seq, batch), 0, 0)

    h_args, h_specs = _stream_specs(h_src, tm, n_lat_tiles * tm, n_all // tm)
    return pl.pallas_call(
        functools.partial(_norm_kernel, n_lat_tiles=n_lat_tiles),
        grid=(n_all // tm,),
        in_specs=h_specs + [
            pl.BlockSpec((1, N_MOD, d), mod_idx),
            pl.BlockSpec((1, d), lambda i: (0, 0)),
        ],
        out_specs=pl.BlockSpec((tm, d), lambda i: (i, 0)),
        out_shape=jax.ShapeDtypeStruct((n_all, d), BF16),
        compiler_params=_cparams("parallel"),
        name="norm_modulate",
    )(*h_args, mod, g1)


def _in_projection(u, w_qk, w_vb, gcol, rope, *, tm, n_lat_tiles, tiles_per_seq):
    n_all, d = u.shape
    cos, sin = rope

    def rope_idx(i, j):
        return (jnp.where(i < n_lat_tiles, i % tiles_per_seq, tiles_per_seq), 0)

    qk = pl.pallas_call(
        _inproj_qk_kernel,
        grid=(n_all // tm, QK_COLS // QK_TN),
        in_specs=[
            pl.BlockSpec((tm, d), lambda i, j: (i, 0)),
            pl.BlockSpec((d, QK_TN), lambda i, j: (0, j)),
            pl.BlockSpec((1, QK_TN), lambda i, j: (0, j)),
            pl.BlockSpec((tm, HEAD_DIM), rope_idx),
            pl.BlockSpec((tm, HEAD_DIM), rope_idx),
        ],
        out_specs=pl.BlockSpec((tm, QK_TN), lambda i, j: (i, j)),
        out_shape=jax.ShapeDtypeStruct((n_all, QK_COLS), BF16),
        compiler_params=_cparams("parallel", "arbitrary"),
        name="in_projection_qk",
    )(u, w_qk, gcol, cos, sin)
    vb = pl.pallas_call(
        _inproj_vb_kernel,
        grid=(n_all // tm, VB_COLS // VB_TN),
        in_specs=[
            pl.BlockSpec((tm, d), lambda i, j: (i, 0)),
            pl.BlockSpec((d, VB_TN), lambda i, j: (0, j)),
        ],
        out_specs=pl.BlockSpec((tm, VB_TN), lambda i, j: (i, j)),
        out_shape=jax.ShapeDtypeStruct((n_all, VB_COLS), BF16),
        compiler_params=_cparams("parallel", "arbitrary"),
        name="in_projection_vb",
    )(u, w_vb)
    return qk, vb


ATT_KC = 256
SAFE_SHIFT = 40.0
GQA_STACK = 2


def _key_chunks(k_refs, v_refs):
    chunks = []
    for k_ref, v_ref in zip(k_refs, v_refs):
        n = k_ref.shape[0]
        for s in range(0, n, ATT_KC):
            chunks.append((k_ref, v_ref, s, min(ATT_KC, n - s)))
    return chunks


def _fill_shifts(bound_ref, m_scr, q_ref, chunks, heads):
    bound = bound_ref[0]

    @pl.when(bound <= SAFE_SHIFT)
    def _():
        m_scr[...] = jnp.full(m_scr.shape, bound, F32)

    @pl.when(bound > SAFE_SHIFT)
    def _():
        for idx, (qcol, kcol) in enumerate(heads):
            q = q_ref[:, qcol:qcol + HEAD_DIM]
            m = None
            for k_ref, _, s, n in chunks:
                part = jnp.max(_dot_nt(q, k_ref[s:s + n, kcol:kcol + HEAD_DIM]), axis=-1, keepdims=True)
                m = part if m is None else jnp.maximum(m, part)
            m_scr[idx] = jnp.broadcast_to(m, m_scr.shape[1:])


def _stream_softmax_pv(streams, chunks, vcols):
    o = None
    ls = [None] * len(streams)
    for k_ref, v_ref, s, n in chunks:
        es = []
        for idx, (q, shift, kcol) in enumerate(streams):
            sc = _dot_nt(q, k_ref[s:s + n, kcol:kcol + HEAD_DIM])
            tiles = [jnp.exp(sc[:, t * LANES:(t + 1) * LANES] - shift) for t in range(n // LANES)]
            for e_t in tiles:
                ls[idx] = e_t if ls[idx] is None else ls[idx] + e_t
            es.append(tiles[0] if len(tiles) == 1 else jnp.concatenate(tiles, axis=1))
        e = es[0] if len(es) == 1 else jnp.concatenate(es, axis=0)
        pv = _dot(e.astype(BF16), v_ref[s:s + n, vcols])
        o = pv if o is None else o + pv
    rows = streams[0][0].shape[0]
    return [(o[i * rows:(i + 1) * rows, :], jnp.sum(ls[i], axis=-1, keepdims=True)) for i in range(len(streams))]


def _diff_attn_kernel(bound_ref, dl_ref, sg_ref, q_ref, *refs, lambda_init, n_pieces):
    k_refs = refs[0:2 * n_pieces:2]
    v_refs = refs[1:2 * n_pieces:2]
    o_ref = refs[2 * n_pieces]
    m_scr = refs[2 * n_pieces + 1]
    chunks = _key_chunks(k_refs, v_refs)
    heads = [(s * HEAD_DIM, s * HEAD_DIM) for s in range(2 * DIFF_HEADS)]
    _fill_shifts(bound_ref, m_scr, q_ref, chunks, heads)
    dl = dl_ref[...]
    lam = (jnp.exp(jnp.sum(dl[0:1] * dl[1:2], axis=-1, keepdims=True))
           - jnp.exp(jnp.sum(dl[2:3] * dl[3:4], axis=-1, keepdims=True)) + lambda_init)
    for h in range(DIFF_HEADS):
        c1 = 2 * h * HEAD_DIM
        c2 = c1 + HEAD_DIM
        vs = slice(h * DIFF_V_DIM, (h + 1) * DIFF_V_DIM)
        (o1, l1), (o2, l2) = _stream_softmax_pv(
            [(q_ref[:, c1:c1 + HEAD_DIM], m_scr[2 * h], c1), (q_ref[:, c2:c2 + HEAD_DIM], m_scr[2 * h + 1], c2)],
            chunks, vs)
        o = o1 * (1.0 / l1) - o2 * (lam / l2)
        ms = jnp.mean(o * o, axis=-1, keepdims=True)
        o = o * lax.rsqrt(ms + EPS) * sg_ref[...] * (1.0 - lambda_init)
        o_ref[:, vs] = o.astype(BF16)


def _gqa_kernel(bound_ref, q_ref, *refs, n_pieces):
    k_refs = refs[0:2 * n_pieces:2]
    v_refs = refs[1:2 * n_pieces:2]
    o_ref = refs[2 * n_pieces]
    m_scr = refs[2 * n_pieces + 1]
    chunks = _key_chunks(k_refs, v_refs)
    group = GQA_HEADS // GQA_KV_HEADS
    heads = [(h * HEAD_DIM, (h // group) * HEAD_DIM) for h in range(GQA_HEADS)]
    _fill_shifts(bound_ref, m_scr, q_ref, chunks, heads)
    tq = q_ref.shape[0]
    for h0 in range(0, GQA_HEADS, GQA_STACK):
        hs = range(h0, h0 + GQA_STACK)
        kcol = heads[h0][1]
        q = jnp.concatenate([q_ref[:, heads[h][0]:heads[h][0] + HEAD_DIM] for h in hs], axis=0)
        shift = jnp.concatenate([m_scr[h] for h in hs], axis=0)
        (o, l), = _stream_softmax_pv([(q, shift, kcol)], chunks, slice(kcol, kcol + HEAD_DIM))
        o = (o * (1.0 / l)).astype(BF16)
        for idx, h in enumerate(hs):
            o_ref[:, heads[h][0]:heads[h][0] + HEAD_DIM] = o[idx * tq:(idx + 1) * tq, :]


def _attention(kind, qk, vb, bound, extra, *, batch, seq, ctx, tq, latent, lambda_init=None):
    n_lat = batch * seq
    ctx_blk0 = n_lat // ctx
    if kind == "diff":
        qcol, kcol, vcol, kvw, v_arr = QK_AQ // 1024, QK_AK // 1024, VB_AV // 1024, 1024, vb
    else:
        qcol, kcol, vcol, kvw, v_arr = QK_CQ // 1024, QK_CK // 256, QK_CV // 256, 256, qk
    if latent:
        q_tiles = seq // tq
        q_spec = pl.BlockSpec((tq, 1024), lambda b, i: (b * q_tiles + i, qcol))
        o_spec = pl.BlockSpec((tq, 1024), lambda b, i: (b * q_tiles + i, 0))
        n_q = n_lat
    else:
        q_tiles = 1
        tq = ctx
        q_spec = pl.BlockSpec((ctx, 1024), lambda b, i: (ctx_blk0 + b, qcol))
        o_spec = pl.BlockSpec((ctx, 1024), lambda b, i: (b, 0))
        n_q = batch * ctx
    kv_specs = [
        pl.BlockSpec((ctx, kvw), lambda b, i: (ctx_blk0 + b, kcol)),
        pl.BlockSpec((ctx, kvw), lambda b, i: (ctx_blk0 + b, vcol)),
    ]
    kv_args = [qk, v_arr]
    if latent:
        kv_specs += [
            pl.BlockSpec((seq, kvw), lambda b, i: (b, kcol)),
            pl.BlockSpec((seq, kvw), lambda b, i: (b, vcol)),
        ]
        kv_args += [qk, v_arr]
    n_pieces = len(kv_args) // 2
    pre_specs = [pl.BlockSpec(memory_space=pltpu.SMEM)]
    pre_args = [bound]
    if kind == "diff":
        diff_lambda, sub_g = extra
        body = functools.partial(_diff_attn_kernel, lambda_init=lambda_init, n_pieces=n_pieces)
        pre_specs += [
            pl.BlockSpec((4, HEAD_DIM), lambda b, i: (0, 0)),
            pl.BlockSpec((1, DIFF_V_DIM), lambda b, i: (0, 0)),
        ]
        pre_args += [diff_lambda, sub_g]
        n_softmax = 2 * DIFF_HEADS
    else:
        body = functools.partial(_gqa_kernel, n_pieces=n_pieces)
        n_softmax = GQA_HEADS
    return pl.pallas_call(
        body,
        grid=(batch, q_tiles),
        in_specs=pre_specs + [q_spec] + kv_specs,
        out_specs=o_spec,
        out_shape=jax.ShapeDtypeStruct((n_q, 1024), BF16),
        scratch_shapes=[pltpu.VMEM((n_softmax, tq, LANES), F32)],
        compiler_params=_cparams("parallel", "arbitrary"),
        name=f"{kind}_attn_{'lat' if latent else 'ctx'}",
    )(*pre_args, qk, *kv_args)


LRU_CW = 512
LRU_PAD = 8
SQRT_GUARD = 1e-30


def _lru_kernel(bxc_ref, byc_ref, bxl_ref, byl_ref, cw_ref, cb_ref, wa_ref, ba_ref, wx_ref, bx_ref, lam_ref,
                *refs, ctx, seq, need_ctx):
    if need_ctx:
        yc_ref, yl_ref = refs[0], refs[1]
        scratch = refs[2:]
    else:
        yc_ref, yl_ref = None, refs[0]
        scratch = refs[1:]
    xpad, a_f, b_f, a_b, b_b = scratch
    a_scr = (a_f, a_b)
    b_scr = (b_f, b_b)
    lam = lam_ref[...]
    neg_sp = -LRU_C * jax.nn.softplus(-lam)

    def gates(x_ref, n):
        zeros = jnp.zeros((LRU_PAD, LRU_CW), F32)
        xpad[0:LRU_PAD, :] = zeros
        xpad[LRU_PAD:LRU_PAD + n, :] = x_ref[...].astype(F32)
        xpad[LRU_PAD + n:2 * LRU_PAD + n, :] = zeros
        xc = cb_ref[...] + cw_ref[0:1, :] * xpad[LRU_PAD - 2:LRU_PAD - 2 + n, :]
        for k in range(1, CONV_W):
            xc = xc + cw_ref[k:k + 1, :] * xpad[LRU_PAD - 2 + k:LRU_PAD - 2 + k + n, :]
        xcb = xc.astype(BF16)
        for d in range(2):
            for blk in range(LRU_CW // LRU_BLOCK_W):
                sl = slice(blk * LRU_BLOCK_W, (blk + 1) * LRU_BLOCK_W)
                xs = xcb[:, sl]
                r = _sigmoid(_dot(xs, wa_ref[d, blk].astype(BF16)) + ba_ref[d:d + 1, sl])
                gi = _sigmoid(_dot(xs, wx_ref[d, blk].astype(BF16)) + bx_ref[d:d + 1, sl])
                log_a = r * neg_sp[d:d + 1, sl]
                a = jnp.exp(log_a)
                one_m_a2 = 1.0 - a * a
                mult = one_m_a2 * lax.rsqrt(jnp.maximum(one_m_a2, SQRT_GUARD))
                a_scr[d][0:n, sl] = a
                b_scr[d][0:n, sl] = mult * gi * xc[:, sl]

    def scan(n, hf0, hb0):
        def body(t, carry):
            hf, hb = carry
            tb = n - 1 - t
            hf = a_f[pl.ds(t, 1), :] * hf + b_f[pl.ds(t, 1), :]
            b_f[pl.ds(t, 1), :] = hf
            hb = a_b[pl.ds(tb, 1), :] * hb + b_b[pl.ds(tb, 1), :]
            b_b[pl.ds(tb, 1), :] = hb
            return hf, hb
        return lax.fori_loop(0, n, body, (hf0, hb0), unroll=8)

    zero = jnp.zeros((1, LRU_CW), F32)
    gates(bxc_ref, ctx)
    hf, hb = scan(ctx, zero, zero)
    if need_ctx:
        yc_ref[...] = ((b_f[0:ctx, :] + b_b[0:ctx, :])
                       * jax.nn.gelu(byc_ref[...].astype(F32), approximate=True)).astype(BF16)
    gates(bxl_ref, seq)
    scan(seq, hf, hb)
    yl_ref[...] = ((b_f[0:seq, :] + b_b[0:seq, :])
                   * jax.nn.gelu(byl_ref[...].astype(F32), approximate=True)).astype(BF16)


def _rglru(vb, conv_w, conv_b, w_a, b_a, w_x, b_x, lam, *, batch, seq, ctx, need_ctx):
    n_lat = batch * seq
    ctx_blk0 = n_lat // ctx
    cbx, cby = VB_BX // LRU_CW, VB_BY // LRU_CW
    nblk = LRU_CW // LRU_BLOCK_W
    tmax = max(seq, ctx)
    out_specs = [pl.BlockSpec((seq, LRU_CW), lambda b, c: (b, c))]
    out_shape = [jax.ShapeDtypeStruct((n_lat, LRU_WIDTH), BF16)]
    if need_ctx:
        out_specs = [pl.BlockSpec((ctx, LRU_CW), lambda b, c: (b, c))] + out_specs
        out_shape = [jax.ShapeDtypeStruct((batch * ctx, LRU_WIDTH), BF16)] + out_shape
    outs = pl.pallas_call(
        functools.partial(_lru_kernel, ctx=ctx, seq=seq, need_ctx=need_ctx),
        grid=(batch, LRU_WIDTH // LRU_CW),
        in_specs=[
            pl.BlockSpec((ctx, LRU_CW), lambda b, c: (ctx_blk0 + b, cbx + c)),
            pl.BlockSpec((ctx, LRU_CW), lambda b, c: (ctx_blk0 + b, cby + c)),
            pl.BlockSpec((seq, LRU_CW), lambda b, c: (b, cbx + c)),
            pl.BlockSpec((seq, LRU_CW), lambda b, c: (b, cby + c)),
            pl.BlockSpec((CONV_W, LRU_CW), lambda b, c: (0, c)),
            pl.BlockSpec((1, LRU_CW), lambda b, c: (0, c)),
            pl.BlockSpec((2, nblk, LRU_BLOCK_W, LRU_BLOCK_W), lambda b, c: (0, c, 0, 0)),
            pl.BlockSpec((2, LRU_CW), lambda b, c: (0, c)),
            pl.BlockSpec((2, nblk, LRU_BLOCK_W, LRU_BLOCK_W), lambda b, c: (0, c, 0, 0)),
            pl.BlockSpec((2, LRU_CW), lambda b, c: (0, c)),
            pl.BlockSpec((2, LRU_CW), lambda b, c: (0, c)),
        ],
        out_specs=out_specs,
        out_shape=out_shape,
        scratch_shapes=[
            pltpu.VMEM((tmax + 2 * LRU_PAD, LRU_CW), F32),
            pltpu.VMEM((tmax, LRU_CW), F32),
            pltpu.VMEM((tmax, LRU_CW), F32),
            pltpu.VMEM((tmax, LRU_CW), F32),
            pltpu.VMEM((tmax, LRU_CW), F32),
        ],
        compiler_params=_cparams("parallel", "arbitrary"),
        name="rglru",
    )(vb, vb, vb, vb, conv_w, conv_b, w_a, b_a, w_x, b_x, lam)
    if need_ctx:
        return outs[1], outs[0]
    return outs[0], None


MERGE_TN = 256


def _merge_kernel(u_ref, *refs, n_lat_tiles, has_ctx):
    n_y = 6 if has_ctx else 3
    y_refs = refs[:n_y]
    wg_refs = refs[n_y:n_y + 3]
    bg_ref = refs[n_y + 3]
    wbr_refs = refs[n_y + 4:n_y + 7]
    m_ref = refs[n_y + 7]

    def body(ys):
        u = u_ref[...]
        m = None
        for k in range(N_BRANCH):
            g = jax.nn.sigmoid(_dot(u, wg_refs[k][...]) + bg_ref[k:k + 1, :])
            part = g * _dot(ys[k][...], wbr_refs[k][...])
            m = part if m is None else m + part
        m_ref[...] = m.astype(BF16)

    if not has_ctx:
        body(y_refs)
        return
    is_lat = pl.program_id(0) < n_lat_tiles

    @pl.when(is_lat)
    def _():
        body(y_refs[0:3])

    @pl.when(jnp.logical_not(is_lat))
    def _():
        body(y_refs[3:6])


def _merge(u, y_lat, y_ctx, w_gate, b_gate, w_branch, *, n_tok, n_lat, tm):
    d = u.shape[1]
    nj = d // MERGE_TN
    nl = n_lat // tm
    has_ctx = y_ctx is not None
    y_specs = [pl.BlockSpec((tm, BRANCH_W), lambda i, j: (jnp.minimum(i, nl - 1), 0))] * 3
    y_args = list(y_lat)
    if has_ctx:
        y_specs += [pl.BlockSpec((tm, BRANCH_W), lambda i, j: (jnp.maximum(i - nl, 0), 0))] * 3
        y_args += list(y_ctx)
    wbr_spec = pl.BlockSpec((BRANCH_W, MERGE_TN), lambda i, j: (0, j))
    return pl.pallas_call(
        functools.partial(_merge_kernel, n_lat_tiles=nl, has_ctx=has_ctx),
        grid=(n_tok // tm, nj),
        in_specs=[pl.BlockSpec((tm, d), lambda i, j: (i, 0))] + y_specs + [
            pl.BlockSpec((d, MERGE_TN), lambda i, j: (0, j)),
            pl.BlockSpec((d, MERGE_TN), lambda i, j: (0, nj + j)),
            pl.BlockSpec((d, MERGE_TN), lambda i, j: (0, 2 * nj + j)),
            pl.BlockSpec((N_BRANCH, MERGE_TN), lambda i, j: (0, j)),
            wbr_spec, wbr_spec, wbr_spec,
        ],
        out_specs=pl.BlockSpec((tm, MERGE_TN), lambda i, j: (i, j)),
        out_shape=jax.ShapeDtypeStruct((n_tok, d), BF16),
        compiler_params=_cparams("parallel", "arbitrary"),
        name="branch_merge",
    )(u, *y_args, w_gate, w_gate, w_gate, b_gate, *w_branch)


OUT_RC = 256
ROUTE_E, ROUTE_P, ROUTE_RANK = 0, 2, 4
COUNT_ROWS = 8


def _outproj_kernel(m_ref, *refs, n_lat_tiles):
    mod_ref, g_ref, wo_ref, wr_ref, br_ref, h1_ref, vp_ref, route_ref, cnt_ref, v_scr = refs[-10:]
    h_refs = refs[:-10]
    tm, d = h1_ref.shape
    rc = min(OUT_RC, tm)
    for r in range(tm // rc):
        rows = slice(r * rc, (r + 1) * rc)
        h1 = (_read_stream(h_refs, rows, n_lat_tiles)
              + mod_ref[0, 2:3, :] * _dot(m_ref[rows, :], wo_ref[...]))
        h1_ref[rows, :] = h1
        ms = jnp.mean(h1 * h1, axis=-1, keepdims=True)
        y = h1 * lax.rsqrt(ms + EPS) * g_ref[...]
        v_scr[rows, :] = (y * (1.0 + mod_ref[0, 4:5, :]) + mod_ref[0, 3:4, :]).astype(BF16)
    v = v_scr[...]
    half = d // 2
    lo = lax.bitcast_convert_type(v[:, :half].astype(F32), U32)
    hi = lax.bitcast_convert_type(v[:, half:].astype(F32), U32)
    vp_ref[...] = (lo >> 16) | (hi & jnp.uint32(0xFFFF0000))

    logits = _dot(v, wr_ref[...]) + br_ref[...]
    lane = lax.broadcasted_iota(I32, logits.shape, 1).astype(F32)
    neg = jnp.float32(-jnp.inf)
    big = jnp.float32(1e9)

    def masked_argmax(mask):
        val = jnp.max(jnp.where(mask, logits, neg), axis=-1, keepdims=True)
        idx = jnp.min(jnp.where(jnp.logical_and(mask, logits == val), lane, big), axis=-1, keepdims=True)
        return val, idx

    gmask = lane < N_GROUPS
    gmax, gidx = masked_argmax(gmask)
    gsum = jnp.sum(jnp.where(gmask, jnp.exp(logits - gmax), 0.0), axis=-1, keepdims=True)
    g_w = 1.0 / gsum
    lo_lane = N_GROUPS + EXPERTS_PER_GROUP * gidx
    emask = jnp.logical_and(lane >= lo_lane, lane < lo_lane + EXPERTS_PER_GROUP)
    v1, i1 = masked_argmax(emask)
    v2, i2 = masked_argmax(jnp.logical_and(emask, lane != i1))
    e21 = jnp.exp(v2 - v1)
    p1 = g_w / (1.0 + e21)
    p2 = g_w * e21 / (1.0 + e21)
    e1 = i1 - N_GROUPS
    e2 = i2 - N_GROUPS
    oh1 = jnp.where(lane == e1, 1.0, 0.0)
    oh2 = jnp.where(lane == e2, 1.0, 0.0)
    cnt = oh1 + oh2
    row = lax.broadcasted_iota(I32, (tm, tm), 0)
    col = lax.broadcasted_iota(I32, (tm, tm), 1)
    before = jnp.where(row > col, 1.0, 0.0).astype(BF16)
    prefix = _dot(before, cnt.astype(BF16))
    rank1 = jnp.sum(prefix * oh1, axis=-1, keepdims=True)
    rank2 = jnp.sum(prefix * oh2, axis=-1, keepdims=True)
    route = jnp.where(lane == 0, e1, jnp.where(lane == 1, e2, jnp.where(lane == 2, p1, jnp.where(
        lane == 3, p2, jnp.where(lane == 4, rank1, jnp.where(lane == 5, rank2, 0.0))))))
    route_ref[...] = route
    cnt_ref[0] = jnp.broadcast_to(jnp.sum(cnt, axis=0, keepdims=True), (COUNT_ROWS, LANES))


def _out_projection(m, h_src, mod, g2, w_out, w_router, b_router, *, n_tok, tm, n_lat, seq, batch):
    d = m.shape[1]
    n_tiles = n_tok // tm

    def mod_idx(i):
        return (jnp.where(i * tm < n_lat, (i * tm) // seq, batch), 0, 0)

    h_args, h_specs = _stream_specs(h_src, tm, n_lat, n_tiles)
    return pl.pallas_call(
        functools.partial(_outproj_kernel, n_lat_tiles=n_lat // tm),
        grid=(n_tiles,),
        in_specs=[pl.BlockSpec((tm, d), lambda i: (i, 0))] + h_specs + [
            pl.BlockSpec((1, N_MOD, d), mod_idx),
            pl.BlockSpec((1, d), lambda i: (0, 0)),
            pl.BlockSpec((d, d), lambda i: (0, 0), pipeline_mode=pl.Buffered(1)),
            pl.BlockSpec((d, LANES), lambda i: (0, 0)),
            pl.BlockSpec((1, LANES), lambda i: (0, 0)),
        ],
        out_specs=[
            pl.BlockSpec((tm, d), lambda i: (i, 0)),
            pl.BlockSpec((tm, d // 2), lambda i: (i, 0)),
            pl.BlockSpec((tm, LANES), lambda i: (i, 0)),
            pl.BlockSpec((1, COUNT_ROWS, LANES), lambda i: (i, 0, 0)),
        ],
        out_shape=[
            jax.ShapeDtypeStruct((n_tok, d), F32),
            jax.ShapeDtypeStruct((n_tok, d // 2), U32),
            jax.ShapeDtypeStruct((n_tok, LANES), F32),
            jax.ShapeDtypeStruct((n_tiles, COUNT_ROWS, LANES), F32),
        ],
        scratch_shapes=[pltpu.VMEM((tm, d), BF16)],
        compiler_params=_cparams("parallel"),
        name="out_projection_router",
    )(m, *h_args, mod, g2, w_out, w_router, b_router)


MOE_TM = 256
CAST_ROWS = 64
ROW_TM = 512


def _dispatch(route, counts, n_tok, tm_route):
    n_tiles = (2 * n_tok) // MOE_TM + N_EXPERTS
    counts = counts[:, 0, :N_EXPERTS].astype(I32)
    tile_base = jnp.cumsum(counts, axis=0) - counts
    total = jnp.sum(counts, axis=0)
    padded = ((total + MOE_TM - 1) // MOE_TM) * MOE_TM
    seg_end = jnp.cumsum(padded)
    base = (seg_end - padded)[None, :] + tile_base
    r3 = route.reshape(n_tok // tm_route, tm_route, LANES)
    experts = jnp.arange(N_EXPERTS, dtype=I32)
    dest = []
    for k in range(2):
        hit = r3[:, :, ROUTE_E + k].astype(I32)[:, :, None] == experts
        picked = jnp.sum(jnp.where(hit, base[:, None, :], 0), axis=-1)
        dest.append((picked + r3[:, :, ROUTE_RANK + k].astype(I32)).reshape(n_tok // ROW_TM, 1, ROW_TM))
    tile_start = jnp.arange(n_tiles, dtype=I32) * MOE_TM
    n_used = seg_end[-1] // MOE_TM
    tile_valid = (tile_start < seg_end[-1]).astype(I32)
    tile_index = jnp.minimum(jnp.arange(n_tiles, dtype=I32), n_used - 1)
    tile_expert = jnp.sum((seg_end[None, :] <= (tile_index * MOE_TM)[:, None]).astype(I32), axis=1)
    prev = jnp.concatenate([jnp.full((1,), -1, I32), tile_expert[:-1]])
    tile_first = (tile_expert != prev).astype(I32)
    later_first = jnp.logical_and(tile_first == 1, tile_valid == 1)
    pos = jnp.arange(n_tiles, dtype=I32)
    first_pos = jnp.where(later_first, pos, n_tiles)
    next_pos = jnp.min(jnp.where(pos[None, :] > pos[:, None], first_pos[None, :], n_tiles), axis=1)
    tile_next = jnp.where(next_pos < n_tiles, tile_expert[jnp.minimum(next_pos, n_tiles - 1)], -1).astype(I32)
    tails = jnp.maximum(seg_end - MOE_TM, 0).astype(I32)
    return dest, (tile_expert, tile_first, tile_valid, tile_index, tile_next), tails


def _row_scatter_kernel(tail_ref, d0_ref, d1_ref, v_ref, x_hbm, zbuf, sem_z, sem):
    @pl.when(pl.program_id(0) == 0)
    def _():
        zbuf[...] = jnp.zeros_like(zbuf)
        def zero_tile(start):
            start = pl.multiple_of(start, MOE_TM)
            return pltpu.make_async_copy(zbuf, x_hbm.at[pl.ds(start, MOE_TM), :], sem_z)
        for e in range(N_EXPERTS):
            zero_tile(tail_ref[e]).start()
        for e in range(N_EXPERTS):
            zero_tile(tail_ref[e]).wait()
        first_unused = tail_ref[N_EXPERTS - 1] // MOE_TM + 1
        n_tiles = x_hbm.shape[0] // MOE_TM

        def start_unused(t, carry):
            zero_tile(t * MOE_TM).start()
            return carry

        def wait_unused(t, carry):
            zero_tile(t * MOE_TM).wait()
            return carry
        lax.fori_loop(first_unused, n_tiles, start_unused, 0)
        lax.fori_loop(first_unused, n_tiles, wait_unused, 0)

    def row_copies(g, k, d0, d1):
        src = v_ref.at[pl.ds(pl.multiple_of(g * 8, 8) + k, 1), :]
        return (pltpu.make_async_copy(src, x_hbm.at[pl.ds(d0, 1), :], sem),
                pltpu.make_async_copy(src, x_hbm.at[pl.ds(d1, 1), :], sem))

    def issue(g, carry):
        for k in range(8):
            for copy in row_copies(g, k, d0_ref[0, 0, g * 8 + k], d1_ref[0, 0, g * 8 + k]):
                copy.start()
        return carry
    lax.fori_loop(0, ROW_TM // 8, issue, 0)

    def drain(g, carry):
        for k in range(8):
            for copy in row_copies(g, k, 0, 0):
                copy.wait()
        return carry
    lax.fori_loop(0, ROW_TM // 8, drain, 0)


def _row_scatter(vp, dest, tails, n_rows):
    n_tok, half = vp.shape
    n_steps = n_tok // ROW_TM
    d0, d1 = dest
    grid_spec = pltpu.PrefetchScalarGridSpec(
        num_scalar_prefetch=1,
        grid=(n_steps,),
        in_specs=[
            pl.BlockSpec((1, 1, ROW_TM), lambda i, t: (i, 0, 0), memory_space=pltpu.SMEM),
            pl.BlockSpec((1, 1, ROW_TM), lambda i, t: (i, 0, 0), memory_space=pltpu.SMEM),
            pl.BlockSpec((ROW_TM, half), lambda i, t: (i, 0)),
        ],
        out_specs=pl.BlockSpec(memory_space=pl.ANY),
        scratch_shapes=[
            pltpu.VMEM((MOE_TM, half), U32),
            pltpu.SemaphoreType.DMA(()),
            pltpu.SemaphoreType.DMA(()),
        ],
    )
    return pl.pallas_call(
        _row_scatter_kernel,
        grid_spec=grid_spec,
        out_shape=jax.ShapeDtypeStruct((n_rows, half), U32),
        compiler_params=_cparams("arbitrary"),
        name="moe_row_scatter",
    )(tails, d0, d1, vp)


def _moe_kernel(te_ref, tf_ref, tv_ref, ti_ref, tn_ref, x_ref, w1_hbm, w3_hbm, w2_hbm, y_ref,
                s1, s3, s2, w1b, w3b, w2b, sems, *, layer):
    i = pl.program_id(0)
    valid = tv_ref[i] == 1

    def weight_copies(e):
        return (pltpu.make_async_copy(w1_hbm.at[layer, e], s1, sems.at[0]),
                pltpu.make_async_copy(w3_hbm.at[layer, e], s3, sems.at[1]),
                pltpu.make_async_copy(w2_hbm.at[layer, e], s2, sems.at[2]))

    @pl.when(i == 0)
    def _():
        for copy in weight_copies(te_ref[0]):
            copy.start()

    @pl.when(valid)
    def _():
        @pl.when(tf_ref[i] == 1)
        def _():
            for copy in weight_copies(te_ref[i]):
                copy.wait()
            for src, dst in ((s1, w1b), (s3, w3b), (s2, w2b)):
                rows_per = CAST_ROWS * 512 // src.shape[1]

                def cast_rows(c, carry, src=src, dst=dst, rows_per=rows_per):
                    rows = pl.ds(pl.multiple_of(c * rows_per, rows_per), rows_per)
                    dst[rows, :] = src[rows, :].astype(BF16)
                    return carry
                lax.fori_loop(0, src.shape[0] // rows_per, cast_rows, 0)

            @pl.when(tn_ref[i] >= 0)
            def _():
                for copy in weight_copies(tn_ref[i]):
                    copy.start()

        half = x_ref.shape[1]
        xp = x_ref[...]
        lo = lax.bitcast_convert_type(xp << 16, F32).astype(BF16)
        hi = lax.bitcast_convert_type(xp & jnp.uint32(0xFFFF0000), F32).astype(BF16)
        h1 = _dot(lo, w1b[0:half, :]) + _dot(hi, w1b[half:, :])
        h3 = _dot(lo, w3b[0:half, :]) + _dot(hi, w3b[half:, :])
        hh = (h1 * jax.nn.sigmoid(h1) * h3).astype(BF16)
        y_ref[...] = _dot(hh, w2b[...])

    @pl.when(jnp.logical_not(valid))
    def _():
        y_ref[...] = jnp.zeros_like(y_ref)


def _moe_experts(layer, x_sorted, tiles, w1, w3, w2):
    tile_expert, tile_first, tile_valid, tile_index, tile_next = tiles
    n_tiles = tile_expert.shape[0]
    half = x_sorted.shape[1]
    d = 2 * half
    grid_spec = pltpu.PrefetchScalarGridSpec(
        num_scalar_prefetch=5,
        grid=(n_tiles,),
        in_specs=[
            pl.BlockSpec((MOE_TM, half), lambda i, te, tf, tv, ti, tn: (ti[i], 0)),
            pl.BlockSpec(memory_space=pl.ANY),
            pl.BlockSpec(memory_space=pl.ANY),
            pl.BlockSpec(memory_space=pl.ANY),
        ],
        out_specs=pl.BlockSpec((MOE_TM, d), lambda i, te, tf, tv, ti, tn: (i, 0)),
        scratch_shapes=[
            pltpu.VMEM((d, D_EXPERT), F32),
            pltpu.VMEM((d, D_EXPERT), F32),
            pltpu.VMEM((D_EXPERT, d), F32),
            pltpu.VMEM((d, D_EXPERT), BF16),
            pltpu.VMEM((d, D_EXPERT), BF16),
            pltpu.VMEM((D_EXPERT, d), BF16),
            pltpu.SemaphoreType.DMA((3,)),
        ],
    )
    return pl.pallas_call(
        functools.partial(_moe_kernel, layer=layer),
        grid_spec=grid_spec,
        out_shape=jax.ShapeDtypeStruct((n_tiles * MOE_TM, d), F32),
        compiler_params=_cparams("arbitrary"),
        name="moe_experts",
    )(tile_expert, tile_first, tile_valid, tile_index, tile_next, x_sorted, w1, w3, w2)


def _combine_kernel(d0_ref, d1_ref, n0_ref, n1_ref, h_ref, route_ref, mod_ref, *refs, with_next):
    if with_next:
        modn_ref, gn_ref, y_hbm, o_ref, u_ref, ybuf, sems = refs
    else:
        y_hbm, o_ref, ybuf, sems = refs
    i = pl.program_id(0)
    n_steps = pl.num_programs(0)

    def row_copies(slot, g, k, d0, d1):
        rows = pl.ds(pl.multiple_of(g * 8, 8) + k, 1)
        return (pltpu.make_async_copy(y_hbm.at[pl.ds(d0, 1), :], ybuf.at[slot, 0, rows, :], sems.at[slot]),
                pltpu.make_async_copy(y_hbm.at[pl.ds(d1, 1), :], ybuf.at[slot, 1, rows, :], sems.at[slot]))

    def request(slot, i0_ref, i1_ref):
        def body(g, carry):
            for k in range(8):
                for copy in row_copies(slot, g, k, i0_ref[0, 0, g * 8 + k], i1_ref[0, 0, g * 8 + k]):
                    copy.start()
            return carry
        lax.fori_loop(0, ROW_TM // 8, body, 0)

    def await_rows(slot):
        def body(g, carry):
            for k in range(8):
                for copy in row_copies(slot, g, k, 0, 0):
                    copy.wait()
            return carry
        lax.fori_loop(0, ROW_TM // 8, body, 0)

    slot = i % 2

    @pl.when(i == 0)
    def _():
        request(0, d0_ref, d1_ref)

    @pl.when(i + 1 < n_steps)
    def _():
        request(1 - slot, n0_ref, n1_ref)

    await_rows(slot)
    p0 = route_ref[:, ROUTE_P:ROUTE_P + 1]
    p1 = route_ref[:, ROUTE_P + 1:ROUTE_P + 2]
    o_ref[...] = h_ref[...] + mod_ref[0, 5:6, :] * (p0 * ybuf[slot, 0] + p1 * ybuf[slot, 1])
    if with_next:
        _norm_modulate_rows((o_ref,), 0, gn_ref, modn_ref, 0, 1, u_ref)


def _combine(h1, route, mod, y_sorted, dest, next_norm, *, n_tok, n_lat, seq, batch):
    d = h1.shape[1]
    tm = ROW_TM
    n_steps = n_tok // tm
    d0, d1 = dest
    with_next = next_norm is not None

    def mod_idx(i):
        return (jnp.where(i * tm < n_lat, (i * tm) // seq, batch), 0, 0)

    row_spec = pl.BlockSpec((tm, d), lambda i: (i, 0))
    mod_spec = pl.BlockSpec((1, N_MOD, d), mod_idx)
    next_specs = [mod_spec, pl.BlockSpec((1, d), lambda i: (0, 0))] if with_next else []
    next_args = list(next_norm) if with_next else []
    out = pl.pallas_call(
        functools.partial(_combine_kernel, with_next=with_next),
        grid=(n_steps,),
        in_specs=[
            pl.BlockSpec((1, 1, tm), lambda i: (i, 0, 0), memory_space=pltpu.SMEM),
            pl.BlockSpec((1, 1, tm), lambda i: (i, 0, 0), memory_space=pltpu.SMEM),
            pl.BlockSpec((1, 1, tm), lambda i: (jnp.minimum(i + 1, n_steps - 1), 0, 0), memory_space=pltpu.SMEM),
            pl.BlockSpec((1, 1, tm), lambda i: (jnp.minimum(i + 1, n_steps - 1), 0, 0), memory_space=pltpu.SMEM),
            row_spec,
            pl.BlockSpec((tm, LANES), lambda i: (i, 0)),
            mod_spec,
        ] + next_specs + [pl.BlockSpec(memory_space=pl.ANY)],
        out_specs=[row_spec, row_spec] if with_next else row_spec,
        out_shape=([jax.ShapeDtypeStruct((n_tok, d), F32), jax.ShapeDtypeStruct((n_tok, d), BF16)] if with_next
                   else jax.ShapeDtypeStruct((n_tok, d), F32)),
        scratch_shapes=[pltpu.VMEM((2, 2, tm, d), F32), pltpu.SemaphoreType.DMA((2,))],
        compiler_params=_cparams("arbitrary"),
        name="moe_combine",
    )(d0, d1, d0, d1, h1, route, mod, *next_args, y_sorted)
    return out if with_next else (out, None)


def _rope_tables(seq, tm):
    n_rows = seq // GRID_W
    row = jnp.repeat(jnp.arange(n_rows), GRID_W).astype(F32)
    col = jnp.tile(jnp.arange(GRID_W), n_rows).astype(F32)
    half = HEAD_DIM // 2
    inv = 1.0 / (ROPE_THETA ** (jnp.arange(0, half, 2, dtype=F32) / half))
    ang_r = row[:, None] * inv
    ang_c = col[:, None] * inv
    ang = jnp.concatenate([ang_r, ang_c, ang_r, ang_c], axis=-1)
    cos, sin = jnp.cos(ang), jnp.sin(ang)
    sin_signed = jnp.where(jnp.arange(HEAD_DIM) < half, -sin, sin)
    ident = jnp.zeros((tm, HEAD_DIM), F32)
    return jnp.concatenate([cos, ident + 1.0]), jnp.concatenate([sin_signed, ident])


def _permute_heads(a):
    lead = a.shape[:-1]
    quarter = HEAD_DIM // 4
    assert HEAD_PERM[quarter] == 2 * quarter and HEAD_PERM[2 * quarter] == quarter
    q = a.reshape(*lead, a.shape[-1] // HEAD_DIM, 4, quarter)
    q = jnp.stack([q[..., 0, :], q[..., 2, :], q[..., 1, :], q[..., 3, :]], axis=-2)
    return q.reshape(*lead, a.shape[-1])


def _score_bound(gq, gk):
    return (1.02 * HEAD_DIM * jnp.max(jnp.abs(gq)) * jnp.max(jnp.abs(gk))).reshape(1).astype(F32)


def kernel(x, c, ctx, c_ctx, w_ada, b_ada, norm1_g, norm2_g, w_in, b_gate, q_norm_a, k_norm_a, diff_lambda,
           sub_norm_a, q_norm_c, k_norm_c, conv_w, conv_b, lru_w_a, lru_b_a, lru_w_x, lru_b_x, lru_lambda,
           w_branch_a, w_branch_b, w_branch_c, w_out, w_group, b_group, w_route, b_route, w1, w3, w2):
    batch, seq, d = x.shape
    ctx_len = ctx.shape[1]
    depth = w_ada.shape[0]
    n_lat = batch * seq
    n_ctx = batch * ctx_len
    n_all = n_lat + n_ctx
    tm = min(1024, seq, n_ctx)
    assert seq % tm == 0 and n_ctx % tm == 0 and n_lat % ctx_len == 0 and batch < MOD_ROWS
    assert seq % GRID_W == 0 and seq % ROW_TM == 0 and n_ctx % ROW_TM == 0
    tq = min(256, seq)
    tm_out = min(512, tm)

    cc = jnp.zeros((MOD_ROWS, d), F32).at[:batch].set(c).at[batch].set(c_ctx)
    mod_all = _ada_modulation(cc, w_ada, b_ada).reshape(depth, MOD_ROWS, N_MOD, d)
    rope = _rope_tables(seq, tm)
    h_src = (x.reshape(n_lat, d), ctx.reshape(n_ctx, d))
    scale = HEAD_DIM ** -0.5

    for l in range(depth):
        last = l == depth - 1
        lambda_init = 0.8 - 0.6 * math.exp(-0.3 * l)
        mod = mod_all[l]
        t_av, t_cq, t_gate = 2048 // PREP_TN, 5120 // PREP_TN, 6656 // PREP_TN
        w_qk = _prepare_weight(w_in, l, QK_COLS, lambda j: jnp.where(j < t_av, j, j + t_cq - t_av),
                               QK_CV // PREP_TN, "prep_w_qk")
        w_vb = _prepare_weight(w_in, l, VB_COLS, lambda j: j + t_av, 0, "prep_w_vb")
        w_gate = _prepare_weight(w_in, l, N_BRANCH * d, lambda j: j + t_gate, 0, "prep_w_gate")
        gq_a, gq_c = q_norm_a[l] * scale, q_norm_c[l] * scale
        gcol = jnp.concatenate([
            _permute_heads(jnp.concatenate([
                jnp.tile(gq_a, 2 * DIFF_HEADS), jnp.tile(k_norm_a[l], 2 * DIFF_HEADS),
                jnp.tile(gq_c, GQA_HEADS), jnp.tile(k_norm_c[l], GQA_KV_HEADS)])),
            jnp.ones((QK_COLS - QK_CV,), F32)]).reshape(1, QK_COLS)
        bound_a = _score_bound(gq_a, k_norm_a[l])
        bound_c = _score_bound(gq_c, k_norm_c[l])

        if l == 0:
            u = _norm_modulate_stream(h_src, mod, norm1_g[l].reshape(1, d), n_all=n_all, tm=tm,
                                      n_lat_tiles=n_lat // tm, tiles_per_seq=seq // tm, batch=batch)
        qk, vb = _in_projection(u, w_qk, w_vb, gcol, rope, tm=tm, n_lat_tiles=n_lat // tm,
                                tiles_per_seq=seq // tm)
        dims = dict(batch=batch, seq=seq, ctx=ctx_len)
        extra_a = (diff_lambda[l], sub_norm_a[l].reshape(1, DIFF_V_DIM))
        ya = _attention("diff", qk, vb, bound_a, extra_a, tq=tq, latent=True, lambda_init=lambda_init, **dims)
        yc = _attention("gqa", qk, vb, bound_c, None, tq=tq, latent=True, **dims)
        yb, yb_c = _rglru(vb, conv_w[l], conv_b[l].reshape(1, LRU_WIDTH), lru_w_a[l], lru_b_a[l],
                          lru_w_x[l], lru_b_x[l], lru_lambda[l], need_ctx=not last, **dims)
        if last:
            n_tok, y_ctx = n_lat, None
        else:
            n_tok = n_all
            ya_c = _attention("diff", qk, vb, bound_a, extra_a, tq=tq, latent=False, lambda_init=lambda_init,
                              **dims)
            yc_c = _attention("gqa", qk, vb, bound_c, None, tq=tq, latent=False, **dims)
            y_ctx = (ya_c, yb_c, yc_c)
        w_branch = (w_branch_a[l].astype(BF16), w_branch_b[l].astype(BF16), w_branch_c[l].astype(BF16))
        m = _merge(u, (ya, yb, yc), y_ctx, w_gate, b_gate[l], w_branch, n_tok=n_tok, n_lat=n_lat, tm=tm)
        w_router = jnp.zeros((d, LANES), F32).at[:, :N_GROUPS].set(w_group[l])
        w_router = w_router.at[:, N_GROUPS:N_GROUPS + N_EXPERTS].set(w_route[l]).astype(BF16)
        b_router = jnp.zeros((1, LANES), F32).at[0, :N_GROUPS].set(b_group[l])
        b_router = b_router.at[0, N_GROUPS:N_GROUPS + N_EXPERTS].set(b_route[l])
        h1, vp, route, counts = _out_projection(m, h_src, mod, norm2_g[l].reshape(1, d),
                                                w_out[l].astype(BF16), w_router, b_router, n_tok=n_tok,
                                                tm=tm_out, n_lat=n_lat, seq=seq, batch=batch)
        dest, tiles, tails = _dispatch(route, counts, n_tok, tm_out)
        x_sorted = _row_scatter(vp, dest, tails, tiles[0].shape[0] * MOE_TM)
        y_sorted = _moe_experts(l, x_sorted, tiles, w1, w3, w2)
        next_norm = None if last else (mod_all[l + 1], norm1_g[l + 1].reshape(1, d))
        h_src, u = _combine(h1, route, mod, y_sorted, dest, next_norm, n_tok=n_tok, n_lat=n_lat, seq=seq,
                            batch=batch)
    return h_src.reshape(batch, seq, d)
```

```python
import functools
import math

import jax
import jax.numpy as jnp
from jax import lax
from jax.experimental import pallas as pl
from jax.experimental.pallas import tpu as pltpu

F32 = jnp.float32
BF16 = jnp.bfloat16
I32 = jnp.int32
U32 = jnp.uint32

EPS = 1e-6
HEAD_DIM = 128
GRID_W = 64
ROPE_THETA = 10000.0
DIFF_HEADS = 4
DIFF_V_DIM = 2 * HEAD_DIM
LRU_WIDTH = 1024
LRU_BLOCK_W = 128
CONV_W = 4
LRU_C = 8.0
GQA_HEADS = 8
GQA_KV_HEADS = 2
N_BRANCH = 3
BRANCH_W = 1024
N_GROUPS = 4
EXPERTS_PER_GROUP = 8
N_EXPERTS = N_GROUPS * EXPERTS_PER_GROUP
D_EXPERT = 512
N_MOD = 6
MOD_ROWS = 16
LANES = 128

QK_AQ, QK_AK, QK_CQ, QK_CK, QK_CV = 0, 1024, 2048, 3072, 3328
VB_AV, VB_BX, VB_BY = 0, 1024, 2048
QK_COLS = 3584
VB_COLS = 3072
QK_TN = QK_COLS // 2
VB_TN = VB_COLS // 2
PROJ_CW = 256
PROJ_RC = 512
HEAD_PERM = tuple(range(0, 32)) + tuple(range(64, 96)) + tuple(range(32, 64)) + tuple(range(96, 128))

VMEM_LIMIT = 56 * 1024 * 1024


def _cparams(*sem):
    return pltpu.CompilerParams(dimension_semantics=sem, vmem_limit_bytes=VMEM_LIMIT)


def _dot(a, b):
    return jnp.dot(a, b, preferred_element_type=F32)


def _dot_nt(a, b):
    return lax.dot_general(a, b, (((1,), (1,)), ((), ())), preferred_element_type=F32)


def _lane_mean(x2):
    return _dot(x2.astype(BF16), jnp.full((LANES, LANES), 1.0 / LANES, BF16))


def _ada_kernel(c_ref, w_ref, b_ref, o_ref):
    c = c_ref[...]
    s = (c * jax.nn.sigmoid(c)).astype(BF16)
    o_ref[0] = _dot(s, w_ref[0].astype(BF16)) + b_ref[0]


def _ada_modulation(cc, w_ada, b_ada):
    depth, d, n = w_ada.shape
    tn = 1024
    return pl.pallas_call(
        _ada_kernel,
        grid=(depth, n // tn),
        in_specs=[
            pl.BlockSpec((MOD_ROWS, d), lambda l, j: (0, 0)),
            pl.BlockSpec((1, d, tn), lambda l, j: (l, 0, j)),
            pl.BlockSpec((1, 1, tn), lambda l, j: (l, 0, j)),
        ],
        out_specs=pl.BlockSpec((1, MOD_ROWS, tn), lambda l, j: (l, 0, j)),
        out_shape=jax.ShapeDtypeStruct((depth, MOD_ROWS, n), F32),
        compiler_params=_cparams("parallel", "parallel"),
        name="ada_modulation",
    )(cc, w_ada, b_ada.reshape(depth, 1, n))


PREP_TN = 256


def _wprep_kernel(w_ref, o_ref, *, n_perm_tiles):
    def plain():
        o_ref[...] = w_ref[0].astype(BF16)

    def permuted():
        quarter = lax.broadcasted_iota(I32, (1, HEAD_DIM), 1) // (HEAD_DIM // 4)
        for c in range(PREP_TN // HEAD_DIM):
            sl = slice(c * HEAD_DIM, (c + 1) * HEAD_DIM)
            x = w_ref[0, :, sl]
            y = jnp.where(quarter == 1, pltpu.roll(x, 96, 1), jnp.where(quarter == 2, pltpu.roll(x, 32, 1), x))
            o_ref[:, sl] = y.astype(BF16)

    if n_perm_tiles == 0:
        plain()
        return
    is_perm = pl.program_id(0) < n_perm_tiles
    pl.when(is_perm)(permuted)
    pl.when(jnp.logical_not(is_perm))(plain)


def _prepare_weight(w_in, layer, n_cols, col_tile_of, n_perm_tiles, name):
    d = w_in.shape[1]
    return pl.pallas_call(
        functools.partial(_wprep_kernel, n_perm_tiles=n_perm_tiles),
        grid=(n_cols // PREP_TN,),
        in_specs=[pl.BlockSpec((1, d, PREP_TN), lambda j: (layer, 0, col_tile_of(j)))],
        out_specs=pl.BlockSpec((d, PREP_TN), lambda j: (0, j)),
        out_shape=jax.ShapeDtypeStruct((d, n_cols), BF16),
        compiler_params=_cparams("parallel"),
        name=name,
    )(w_in)


NORM_ROWS = 32


def _stream_specs(h_src, tm, n_lat, n_tiles):
    nl = n_lat // tm
    if isinstance(h_src, tuple) and n_tiles > nl:
        d = h_src[0].shape[1]
        return list(h_src), [pl.BlockSpec((tm, d), lambda i: (jnp.minimum(i, nl - 1), 0)),
                             pl.BlockSpec((tm, d), lambda i: (jnp.maximum(i - nl, 0), 0))]
    arr = h_src[0] if isinstance(h_src, tuple) else h_src
    return [arr], [pl.BlockSpec((tm, arr.shape[1]), lambda i: (i, 0))]


def _read_stream(h_refs, rows, n_lat_tiles):
    if len(h_refs) == 1:
        return h_refs[0][rows, :]
    return jnp.where(pl.program_id(0) < n_lat_tiles, h_refs[0][rows, :], h_refs[1][rows, :])


def _norm_modulate_rows(h_refs, n_lat_tiles, g_ref, mod_ref, shift_row, scale_row, out_ref):
    g = g_ref[...]
    sc = 1.0 + mod_ref[0, scale_row:scale_row + 1, :]
    sh = mod_ref[0, shift_row:shift_row + 1, :]

    def body(i, carry):
        rows = pl.ds(pl.multiple_of(i * NORM_ROWS, NORM_ROWS), NORM_ROWS)
        x = _read_stream(h_refs, rows, n_lat_tiles)
        ms = jnp.mean(x * x, axis=-1, keepdims=True)
        y = (x * lax.rsqrt(ms + EPS) * g) * sc + sh
        out_ref[rows, :] = y.astype(out_ref.dtype)
        return carry
    lax.fori_loop(0, out_ref.shape[0] // NORM_ROWS, body, 0)


def _norm_kernel(*refs, n_lat_tiles):
    mod_ref, g_ref, u_ref = refs[-3:]
    _norm_modulate_rows(refs[:-3], n_lat_tiles, g_ref, mod_ref, 0, 1, u_ref)


def _project_tile(u_ref, w_ref, epilogue):
    tm, tn = u_ref.shape[0], w_ref.shape[1]
    rc = min(PROJ_RC, tm)
    pending = []
    for c in range(tn // PROJ_CW):
        cols = slice(c * PROJ_CW, (c + 1) * PROJ_CW)
        accs = []
        for r in range(tm // rc):
            rows = slice(r * rc, (r + 1) * rc)
            accs.append((rows, cols, _dot(u_ref[rows, :], w_ref[:, cols])))
        for item in pending:
            epilogue(*item)
        pending = accs
    for item in pending:
        epilogue(*item)


def _inproj_qk_kernel(u_ref, w_ref, gc_ref, cos_ref, sin_ref, qk_ref):
    last_tile = pl.program_id(1) == pl.num_programs(1) - 1
    tn = w_ref.shape[1]

    def epilogue(rows, cols, acc):
        plain = jnp.logical_and(last_tile, cols.stop == tn)
        for c in range(PROJ_CW // HEAD_DIM):
            sl = slice(cols.start + c * HEAD_DIM, cols.start + (c + 1) * HEAD_DIM)
            x = acc[:, c * HEAD_DIM:(c + 1) * HEAD_DIM]
            y = x * lax.rsqrt(_lane_mean(x * x) + EPS) * gc_ref[:, sl]
            y = y * cos_ref[rows, :] + pltpu.roll(y, HEAD_DIM // 2, 1) * sin_ref[rows, :]
            qk_ref[rows, sl] = jnp.where(plain, x, y).astype(BF16)

    _project_tile(u_ref, w_ref, epilogue)


def _inproj_vb_kernel(u_ref, w_ref, vb_ref):
    def epilogue(rows, cols, acc):
        vb_ref[rows, cols] = acc.astype(BF16)

    _project_tile(u_ref, w_ref, epilogue)


def _norm_modulate_stream(h_src, mod, g1, *, n_all, tm, n_lat_tiles, tiles_per_seq, batch):
    d = g1.shape[1]

    def mod_idx(i):
        return (jnp.where(i < n_lat_tiles, i // tiles_per_seq, batch), 0, 0)

    h_args, h_specs = _stream_specs(h_src, tm, n_lat_tiles * tm, n_all // tm)
    return pl.pallas_call(
        functools.partial(_norm_kernel, n_lat_tiles=n_lat_tiles),
        grid=(n_all // tm,),
        in_specs=h_specs + [
            pl.BlockSpec((1, N_MOD, d), mod_idx),
            pl.BlockSpec((1, d), lambda i: (0, 0)),
        ],
        out_specs=pl.BlockSpec((tm, d), lambda i: (i, 0)),
        out_shape=jax.ShapeDtypeStruct((n_all, d), BF16),
        compiler_params=_cparams("parallel"),
        name="norm_modulate",
    )(*h_args, mod, g1)


def _in_projection(u, w_qk, w_vb, gcol, rope, *, tm, n_lat_tiles, tiles_per_seq):
    n_all, d = u.shape
    cos, sin = rope

    def rope_idx(i, j):
        return (jnp.where(i < n_lat_tiles, i % tiles_per_seq, tiles_per_seq), 0)

    qk = pl.pallas_call(
        _inproj_qk_kernel,
        grid=(n_all // tm, QK_COLS // QK_TN),
        in_specs=[
            pl.BlockSpec((tm, d), lambda i, j: (i, 0)),
            pl.BlockSpec((d, QK_TN), lambda i, j: (0, j)),
            pl.BlockSpec((1, QK_TN), lambda i, j: (0, j)),
            pl.BlockSpec((tm, HEAD_DIM), rope_idx),
            pl.BlockSpec((tm, HEAD_DIM), rope_idx),
        ],
        out_specs=pl.BlockSpec((tm, QK_TN), lambda i, j: (i, j)),
        out_shape=jax.ShapeDtypeStruct((n_all, QK_COLS), BF16),
        compiler_params=_cparams("parallel", "arbitrary"),
        name="in_projection_qk",
    )(u, w_qk, gcol, cos, sin)
    vb = pl.pallas_call(
        _inproj_vb_kernel,
        grid=(n_all // tm, VB_COLS // VB_TN),
        in_specs=[
            pl.BlockSpec((tm, d), lambda i, j: (i, 0)),
            pl.BlockSpec((d, VB_TN), lambda i, j: (0, j)),
        ],
        out_specs=pl.BlockSpec((tm, VB_TN), lambda i, j: (i, j)),
        out_shape=jax.ShapeDtypeStruct((n_all, VB_COLS), BF16),
        compiler_params=_cparams("parallel", "arbitrary"),
        name="in_projection_vb",
    )(u, w_vb)
    return qk, vb


ATT_KC = 256
SAFE_SHIFT = 40.0
GQA_STACK = 2


def _key_chunks(k_refs, v_refs):
    chunks = []
    for k_ref, v_ref in zip(k_refs, v_refs):
        n = k_ref.shape[0]
        for s in range(0, n, ATT_KC):
            chunks.append((k_ref, v_ref, s, min(ATT_KC, n - s)))
    return chunks


def _fill_shifts(bound_ref, m_scr, q_ref, chunks, heads):
    bound = bound_ref[0]

    @pl.when(bound <= SAFE_SHIFT)
    def _():
        m_scr[...] = jnp.full(m_scr.shape, bound, F32)

    @pl.when(bound > SAFE_SHIFT)
    def _():
        for idx, (qcol, kcol) in enumerate(heads):
            q = q_ref[:, qcol:qcol + HEAD_DIM]
            m = None
            for k_ref, _, s, n in chunks:
                part = jnp.max(_dot_nt(q, k_ref[s:s + n, kcol:kcol + HEAD_DIM]), axis=-1, keepdims=True)
                m = part if m is None else jnp.maximum(m, part)
            m_scr[idx] = jnp.broadcast_to(m, m_scr.shape[1:])


def _stream_softmax_pv(streams, chunks, vcols):
    o = None
    ls = [None] * len(streams)
    for k_ref, v_ref, s, n in chunks:
        es = []
        for idx, (q, shift, kcol) in enumerate(streams):
            sc = _dot_nt(q, k_ref[s:s + n, kcol:kcol + HEAD_DIM])
            tiles = [jnp.exp(sc[:, t * LANES:(t + 1) * LANES] - shift) for t in range(n // LANES)]
            for e_t in tiles:
                ls[idx] = e_t if ls[idx] is None else ls[idx] + e_t
            es.append(tiles[0] if len(tiles) == 1 else jnp.concatenate(tiles, axis=1))
        e = es[0] if len(es) == 1 else jnp.concatenate(es, axis=0)
        pv = _dot(e.astype(BF16), v_ref[s:s + n, vcols])
        o = pv if o is None else o + pv
    rows = streams[0][0].shape[0]
    return [(o[i * rows:(i + 1) * rows, :], jnp.sum(ls[i], axis=-1, keepdims=True)) for i in range(len(streams))]


def _diff_attn_kernel(bound_ref, dl_ref, sg_ref, q_ref, *refs, lambda_init, n_pieces):
    k_refs = refs[0:2 * n_pieces:2]
    v_refs = refs[1:2 * n_pieces:2]
    o_ref = refs[2 * n_pieces]
    m_scr = refs[2 * n_pieces + 1]
    chunks = _key_chunks(k_refs, v_refs)
    heads = [(s * HEAD_DIM, s * HEAD_DIM) for s in range(2 * DIFF_HEADS)]
    _fill_shifts(bound_ref, m_scr, q_ref, chunks, heads)
    dl = dl_ref[...]
    lam = (jnp.exp(jnp.sum(dl[0:1] * dl[1:2], axis=-1, keepdims=True))
           - jnp.exp(jnp.sum(dl[2:3] * dl[3:4], axis=-1, keepdims=True)) + lambda_init)
    for h in range(DIFF_HEADS):
        c1 = 2 * h * HEAD_DIM
        c2 = c1 + HEAD_DIM
        vs = slice(h * DIFF_V_DIM, (h + 1) * DIFF_V_DIM)
        (o1, l1), = _stream_softmax_pv([(q_ref[:, c1:c1 + HEAD_DIM], m_scr[2 * h], c1)], chunks, vs)
        (o2, l2), = _stream_softmax_pv([(q_ref[:, c2:c2 + HEAD_DIM], m_scr[2 * h + 1], c2)], chunks, vs)
        o = o1 * (1.0 / l1) - o2 * (lam / l2)
        ms = jnp.mean(o * o, axis=-1, keepdims=True)
        o = o * lax.rsqrt(ms + EPS) * sg_ref[...] * (1.0 - lambda_init)
        o_ref[:, vs] = o.astype(BF16)


def _gqa_kernel(bound_ref, q_ref, *refs, n_pieces):
    k_refs = refs[0:2 * n_pieces:2]
    v_refs = refs[1:2 * n_pieces:2]
    o_ref = refs[2 * n_pieces]
    m_scr = refs[2 * n_pieces + 1]
    chunks = _key_chunks(k_refs, v_refs)
    group = GQA_HEADS // GQA_KV_HEADS
    heads = [(h * HEAD_DIM, (h // group) * HEAD_DIM) for h in range(GQA_HEADS)]
    _fill_shifts(bound_ref, m_scr, q_ref, chunks, heads)
    tq = q_ref.shape[0]
    for h0 in range(0, GQA_HEADS, GQA_STACK):
        hs = range(h0, h0 + GQA_STACK)
        kcol = heads[h0][1]
        q = jnp.concatenate([q_ref[:, heads[h][0]:heads[h][0] + HEAD_DIM] for h in hs], axis=0)
        shift = jnp.concatenate([m_scr[h] for h in hs], axis=0)
        (o, l), = _stream_softmax_pv([(q, shift, kcol)], chunks, slice(kcol, kcol + HEAD_DIM))
        o = (o * (1.0 / l)).astype(BF16)
        for idx, h in enumerate(hs):
            o_ref[:, heads[h][0]:heads[h][0] + HEAD_DIM] = o[idx * tq:(idx + 1) * tq, :]


def _attention(kind, qk, vb, bound, extra, *, batch, seq, ctx, tq, latent, lambda_init=None):
    n_lat = batch * seq
    ctx_blk0 = n_lat // ctx
    if kind == "diff":
        qcol, kcol, vcol, kvw, v_arr = QK_AQ // 1024, QK_AK // 1024, VB_AV // 1024, 1024, vb
    else:
        qcol, kcol, vcol, kvw, v_arr = QK_CQ // 1024, QK_CK // 256, QK_CV // 256, 256, qk
    if latent:
        q_tiles = seq // tq
        q_spec = pl.BlockSpec((tq, 1024), lambda b, i: (b * q_tiles + i, qcol))
        o_spec = pl.BlockSpec((tq, 1024), lambda b, i: (b * q_tiles + i, 0))
        n_q = n_lat
    else:
        q_tiles = 1
        tq = ctx
        q_spec = pl.BlockSpec((ctx, 1024), lambda b, i: (ctx_blk0 + b, qcol))
        o_spec = pl.BlockSpec((ctx, 1024), lambda b, i: (b, 0))
        n_q = batch * ctx
    kv_specs = [
        pl.BlockSpec((ctx, kvw), lambda b, i: (ctx_blk0 + b, kcol)),
        pl.BlockSpec((ctx, kvw), lambda b, i: (ctx_blk0 + b, vcol)),
    ]
    kv_args = [qk, v_arr]
    if latent:
        kv_specs += [
            pl.BlockSpec((seq, kvw), lambda b, i: (b, kcol)),
            pl.BlockSpec((seq, kvw), lambda b, i: (b, vcol)),
        ]
        kv_args += [qk, v_arr]
    n_pieces = len(kv_args) // 2
    pre_specs = [pl.BlockSpec(memory_space=pltpu.SMEM)]
    pre_args = [bound]
    if kind == "diff":
        diff_lambda, sub_g = extra
        body = functools.partial(_diff_attn_kernel, lambda_init=lambda_init, n_pieces=n_pieces)
        pre_specs += [
            pl.BlockSpec((4, HEAD_DIM), lambda b, i: (0, 0)),
            pl.BlockSpec((1, DIFF_V_DIM), lambda b, i: (0, 0)),
        ]
        pre_args += [diff_lambda, sub_g]
        n_softmax = 2 * DIFF_HEADS
    else:
        body = functools.partial(_gqa_kernel, n_pieces=n_pieces)
        n_softmax = GQA_HEADS
    return pl.pallas_call(
        body,
        grid=(batch, q_tiles),
        in_specs=pre_specs + [q_spec] + kv_specs,
        out_specs=o_spec,
        out_shape=jax.ShapeDtypeStruct((n_q, 1024), BF16),
        scratch_shapes=[pltpu.VMEM((n_softmax, tq, LANES), F32)],
        compiler_params=_cparams("parallel", "arbitrary"),
        name=f"{kind}_attn_{'lat' if latent else 'ctx'}",
    )(*pre_args, qk, *kv_args)


LRU_CW = 512
LRU_PAD = 8
SQRT_GUARD = 1e-30


def _lru_kernel(bxc_ref, byc_ref, bxl_ref, byl_ref, cw_ref, cb_ref, wa_ref, ba_ref, wx_ref, bx_ref, lam_ref,
                *refs, ctx, seq, need_ctx):
    if need_ctx:
        yc_ref, yl_ref = refs[0], refs[1]
        scratch = refs[2:]
    else:
        yc_ref, yl_ref = None, refs[0]
        scratch = refs[1:]
    xpad, a_f, b_f, a_b, b_b = scratch
    a_scr = (a_f, a_b)
    b_scr = (b_f, b_b)
    lam = lam_ref[...]
    neg_sp = -LRU_C * jax.nn.softplus(-lam)

    def gates(x_ref, n):
        zeros = jnp.zeros((LRU_PAD, LRU_CW), F32)
        xpad[0:LRU_PAD, :] = zeros
        xpad[LRU_PAD:LRU_PAD + n, :] = x_ref[...].astype(F32)
        xpad[LRU_PAD + n:2 * LRU_PAD + n, :] = zeros
        xc = cb_ref[...] + cw_ref[0:1, :] * xpad[LRU_PAD - 2:LRU_PAD - 2 + n, :]
        for k in range(1, CONV_W):
            xc = xc + cw_ref[k:k + 1, :] * xpad[LRU_PAD - 2 + k:LRU_PAD - 2 + k + n, :]
        xcb = xc.astype(BF16)
        for d in range(2):
            for blk in range(LRU_CW // LRU_BLOCK_W):
                sl = slice(blk * LRU_BLOCK_W, (blk + 1) * LRU_BLOCK_W)
                xs = xcb[:, sl]
                r = 0.5 * jnp.tanh(_dot(xs, wa_ref[d, blk].astype(BF16)) + ba_ref[d:d + 1, sl]) + 0.5
                gi = 0.5 * jnp.tanh(_dot(xs, wx_ref[d, blk].astype(BF16)) + bx_ref[d:d + 1, sl]) + 0.5
                log_a = r * neg_sp[d:d + 1, sl]
                a = jnp.exp(log_a)
                one_m_a2 = 1.0 - a * a
                mult = one_m_a2 * lax.rsqrt(jnp.maximum(one_m_a2, SQRT_GUARD))
                a_scr[d][0:n, sl] = a
                b_scr[d][0:n, sl] = mult * gi * xc[:, sl]

    def scan(n, hf0, hb0):
        def body(t, carry):
            hf, hb = carry
            tb = n - 1 - t
            hf = a_f[pl.ds(t, 1), :] * hf + b_f[pl.ds(t, 1), :]
            b_f[pl.ds(t, 1), :] = hf
            hb = a_b[pl.ds(tb, 1), :] * hb + b_b[pl.ds(tb, 1), :]
            b_b[pl.ds(tb, 1), :] = hb
            return hf, hb
        return lax.fori_loop(0, n, body, (hf0, hb0), unroll=8)

    zero = jnp.zeros((1, LRU_CW), F32)
    gates(bxc_ref, ctx)
    hf, hb = scan(ctx, zero, zero)
    if need_ctx:
        yc_ref[...] = ((b_f[0:ctx, :] + b_b[0:ctx, :])
                       * jax.nn.gelu(byc_ref[...].astype(F32), approximate=True)).astype(BF16)
    gates(bxl_ref, seq)
    scan(seq, hf, hb)
    yl_ref[...] = ((b_f[0:seq, :] + b_b[0:seq, :])
                   * jax.nn.gelu(byl_ref[...].astype(F32), approximate=True)).astype(BF16)


def _rglru(vb, conv_w, conv_b, w_a, b_a, w_x, b_x, lam, *, batch, seq, ctx, need_ctx):
    n_lat = batch * seq
    ctx_blk0 = n_lat // ctx
    cbx, cby = VB_BX // LRU_CW, VB_BY // LRU_CW
    nblk = LRU_CW // LRU_BLOCK_W
    tmax = max(seq, ctx)
    out_specs = [pl.BlockSpec((seq, LRU_CW), lambda b, c: (b, c))]
    out_shape = [jax.ShapeDtypeStruct((n_lat, LRU_WIDTH), BF16)]
    if need_ctx:
        out_specs = [pl.BlockSpec((ctx, LRU_CW), lambda b, c: (b, c))] + out_specs
        out_shape = [jax.ShapeDtypeStruct((batch * ctx, LRU_WIDTH), BF16)] + out_shape
    outs = pl.pallas_call(
        functools.partial(_lru_kernel, ctx=ctx, seq=seq, need_ctx=need_ctx),
        grid=(batch, LRU_WIDTH // LRU_CW),
        in_specs=[
            pl.BlockSpec((ctx, LRU_CW), lambda b, c: (ctx_blk0 + b, cbx + c)),
            pl.BlockSpec((ctx, LRU_CW), lambda b, c: (ctx_blk0 + b, cby + c)),
            pl.BlockSpec((seq, LRU_CW), lambda b, c: (b, cbx + c)),
            pl.BlockSpec((seq, LRU_CW), lambda b, c: (b, cby + c)),
            pl.BlockSpec((CONV_W, LRU_CW), lambda b, c: (0, c)),
            pl.BlockSpec((1, LRU_CW), lambda b, c: (0, c)),
            pl.BlockSpec((2, nblk, LRU_BLOCK_W, LRU_BLOCK_W), lambda b, c: (0, c, 0, 0)),
            pl.BlockSpec((2, LRU_CW), lambda b, c: (0, c)),
            pl.BlockSpec((2, nblk, LRU_BLOCK_W, LRU_BLOCK_W), lambda b, c: (0, c, 0, 0)),
            pl.BlockSpec((2, LRU_CW), lambda b, c: (0, c)),
            pl.BlockSpec((2, LRU_CW), lambda b, c: (0, c)),
        ],
        out_specs=out_specs,
        out_shape=out_shape,
        scratch_shapes=[
            pltpu.VMEM((tmax + 2 * LRU_PAD, LRU_CW), F32),
            pltpu.VMEM((tmax, LRU_CW), F32),
            pltpu.VMEM((tmax, LRU_CW), F32),
            pltpu.VMEM((tmax, LRU_CW), F32),
            pltpu.VMEM((tmax, LRU_CW), F32),
        ],
        compiler_params=_cparams("parallel", "arbitrary"),
        name="rglru",
    )(vb, vb, vb, vb, conv_w, conv_b, w_a, b_a, w_x, b_x, lam)
    if need_ctx:
        return outs[1], outs[0]
    return outs[0], None


MERGE_TN = 512


def _merge_kernel(u_ref, *refs, n_lat_tiles, has_ctx):
    n_y = 6 if has_ctx else 3
    y_refs = refs[:n_y]
    wg_refs = refs[n_y:n_y + 3]
    bg_ref = refs[n_y + 3]
    wbr_refs = refs[n_y + 4:n_y + 7]
    m_ref = refs[n_y + 7]

    def body(ys):
        u = u_ref[...]
        m = None
        for k in range(N_BRANCH):
            g = jax.nn.sigmoid(_dot(u, wg_refs[k][...]) + bg_ref[k:k + 1, :])
            part = g * _dot(ys[k][...], wbr_refs[k][...])
            m = part if m is None else m + part
        m_ref[...] = m.astype(BF16)

    if not has_ctx:
        body(y_refs)
        return
    is_lat = pl.program_id(0) < n_lat_tiles

    @pl.when(is_lat)
    def _():
        body(y_refs[0:3])

    @pl.when(jnp.logical_not(is_lat))
    def _():
        body(y_refs[3:6])


def _merge(u, y_lat, y_ctx, w_gate, b_gate, w_branch, *, n_tok, n_lat, tm):
    d = u.shape[1]
    nj = d // MERGE_TN
    nl = n_lat // tm
    has_ctx = y_ctx is not None
    y_specs = [pl.BlockSpec((tm, BRANCH_W), lambda i, j: (jnp.minimum(i, nl - 1), 0))] * 3
    y_args = list(y_lat)
    if has_ctx:
        y_specs += [pl.BlockSpec((tm, BRANCH_W), lambda i, j: (jnp.maximum(i - nl, 0), 0),
                                 pipeline_mode=pl.Buffered(1))] * 3
        y_args += list(y_ctx)
    wbr_spec = pl.BlockSpec((BRANCH_W, MERGE_TN), lambda i, j: (0, j))
    return pl.pallas_call(
        functools.partial(_merge_kernel, n_lat_tiles=nl, has_ctx=has_ctx),
        grid=(n_tok // tm, nj),
        in_specs=[pl.BlockSpec((tm, d), lambda i, j: (i, 0))] + y_specs + [
            pl.BlockSpec((d, MERGE_TN), lambda i, j: (0, j)),
            pl.BlockSpec((d, MERGE_TN), lambda i, j: (0, nj + j)),
            pl.BlockSpec((d, MERGE_TN), lambda i, j: (0, 2 * nj + j)),
            pl.BlockSpec((N_BRANCH, MERGE_TN), lambda i, j: (0, j)),
            wbr_spec, wbr_spec, wbr_spec,
        ],
        out_specs=pl.BlockSpec((tm, MERGE_TN), lambda i, j: (i, j)),
        out_shape=jax.ShapeDtypeStruct((n_tok, d), BF16),
        compiler_params=_cparams("parallel", "arbitrary"),
        name="branch_merge",
    )(u, *y_args, w_gate, w_gate, w_gate, b_gate, *w_branch)


OUT_RC = 256
ROUTE_E, ROUTE_P, ROUTE_RANK = 0, 2, 4
COUNT_ROWS = 8


def _outproj_kernel(m_ref, *refs, n_lat_tiles):
    mod_ref, g_ref, wo_ref, wr_ref, br_ref, h1_ref, vp_ref, route_ref, cnt_ref, v_scr = refs[-10:]
    h_refs = refs[:-10]
    tm, d = h1_ref.shape
    rc = min(OUT_RC, tm)
    for r in range(tm // rc):
        rows = slice(r * rc, (r + 1) * rc)
        h1 = (_read_stream(h_refs, rows, n_lat_tiles)
              + mod_ref[0, 2:3, :] * _dot(m_ref[rows, :], wo_ref[...]))
        h1_ref[rows, :] = h1
        ms = jnp.mean(h1 * h1, axis=-1, keepdims=True)
        y = h1 * lax.rsqrt(ms + EPS) * g_ref[...]
        v_scr[rows, :] = (y * (1.0 + mod_ref[0, 4:5, :]) + mod_ref[0, 3:4, :]).astype(BF16)
    v = v_scr[...]
    half = d // 2
    lo = lax.bitcast_convert_type(v[:, :half].astype(F32), U32)
    hi = lax.bitcast_convert_type(v[:, half:].astype(F32), U32)
    vp_ref[...] = (lo >> 16) | (hi & jnp.uint32(0xFFFF0000))

    logits = _dot(v, wr_ref[...]) + br_ref[...]
    lane = lax.broadcasted_iota(I32, logits.shape, 1).astype(F32)
    neg = jnp.float32(-jnp.inf)
    big = jnp.float32(1e9)

    def masked_argmax(mask):
        val = jnp.max(jnp.where(mask, logits, neg), axis=-1, keepdims=True)
        idx = jnp.min(jnp.where(jnp.logical_and(mask, logits == val), lane, big), axis=-1, keepdims=True)
        return val, idx

    gmask = lane < N_GROUPS
    gmax, gidx = masked_argmax(gmask)
    gsum = jnp.sum(jnp.where(gmask, jnp.exp(logits - gmax), 0.0), axis=-1, keepdims=True)
    g_w = 1.0 / gsum
    lo_lane = N_GROUPS + EXPERTS_PER_GROUP * gidx
    emask = jnp.logical_and(lane >= lo_lane, lane < lo_lane + EXPERTS_PER_GROUP)
    v1, i1 = masked_argmax(emask)
    v2, i2 = masked_argmax(jnp.logical_and(emask, lane != i1))
    e21 = jnp.exp(v2 - v1)
    p1 = g_w / (1.0 + e21)
    p2 = g_w * e21 / (1.0 + e21)
    e1 = i1 - N_GROUPS
    e2 = i2 - N_GROUPS
    oh1 = jnp.where(lane == e1, 1.0, 0.0)
    oh2 = jnp.where(lane == e2, 1.0, 0.0)
    cnt = oh1 + oh2
    row = lax.broadcasted_iota(I32, (tm, tm), 0)
    col = lax.broadcasted_iota(I32, (tm, tm), 1)
    before = jnp.where(row > col, 1.0, 0.0).astype(BF16)
    prefix = _dot(before, cnt.astype(BF16))
    rank1 = jnp.sum(prefix * oh1, axis=-1, keepdims=True)
    rank2 = jnp.sum(prefix * oh2, axis=-1, keepdims=True)
    route = jnp.where(lane == 0, e1, jnp.where(lane == 1, e2, jnp.where(lane == 2, p1, jnp.where(
        lane == 3, p2, jnp.where(lane == 4, rank1, jnp.where(lane == 5, rank2, 0.0))))))
    route_ref[...] = route
    cnt_ref[0] = jnp.broadcast_to(jnp.sum(cnt, axis=0, keepdims=True), (COUNT_ROWS, LANES))


def _out_projection(m, h_src, mod, g2, w_out, w_router, b_router, *, n_tok, tm, n_lat, seq, batch):
    d = m.shape[1]
    n_tiles = n_tok // tm

    def mod_idx(i):
        return (jnp.where(i * tm < n_lat, (i * tm) // seq, batch), 0, 0)

    h_args, h_specs = _stream_specs(h_src, tm, n_lat, n_tiles)
    return pl.pallas_call(
        functools.partial(_outproj_kernel, n_lat_tiles=n_lat // tm),
        grid=(n_tiles,),
        in_specs=[pl.BlockSpec((tm, d), lambda i: (i, 0))] + h_specs + [
            pl.BlockSpec((1, N_MOD, d), mod_idx),
            pl.BlockSpec((1, d), lambda i: (0, 0)),
            pl.BlockSpec((d, d), lambda i: (0, 0), pipeline_mode=pl.Buffered(1)),
            pl.BlockSpec((d, LANES), lambda i: (0, 0)),
            pl.BlockSpec((1, LANES), lambda i: (0, 0)),
        ],
        out_specs=[
            pl.BlockSpec((tm, d), lambda i: (i, 0)),
            pl.BlockSpec((tm, d // 2), lambda i: (i, 0)),
            pl.BlockSpec((tm, LANES), lambda i: (i, 0)),
            pl.BlockSpec((1, COUNT_ROWS, LANES), lambda i: (i, 0, 0)),
        ],
        out_shape=[
            jax.ShapeDtypeStruct((n_tok, d), F32),
            jax.ShapeDtypeStruct((n_tok, d // 2), U32),
            jax.ShapeDtypeStruct((n_tok, LANES), F32),
            jax.ShapeDtypeStruct((n_tiles, COUNT_ROWS, LANES), F32),
        ],
        scratch_shapes=[pltpu.VMEM((tm, d), BF16)],
        compiler_params=_cparams("parallel"),
        name="out_projection_router",
    )(m, *h_args, mod, g2, w_out, w_router, b_router)


MOE_TM = 256
CAST_ROWS = 64
ROW_TM = 512


def _dispatch(route, counts, n_tok, tm_route):
    n_tiles = (2 * n_tok) // MOE_TM + N_EXPERTS
    counts = counts[:, 0, :N_EXPERTS].astype(I32)
    tile_base = jnp.cumsum(counts, axis=0) - counts
    total = jnp.sum(counts, axis=0)
    padded = ((total + MOE_TM - 1) // MOE_TM) * MOE_TM
    seg_end = jnp.cumsum(padded)
    base = (seg_end - padded)[None, :] + tile_base
    r3 = route.reshape(n_tok // tm_route, tm_route, LANES)
    experts = jnp.arange(N_EXPERTS, dtype=I32)
    dest = []
    for k in range(2):
        hit = r3[:, :, ROUTE_E + k].astype(I32)[:, :, None] == experts
        picked = jnp.sum(jnp.where(hit, base[:, None, :], 0), axis=-1)
        dest.append((picked + r3[:, :, ROUTE_RANK + k].astype(I32)).reshape(n_tok // ROW_TM, 1, ROW_TM))
    tile_start = jnp.arange(n_tiles, dtype=I32) * MOE_TM
    n_used = seg_end[-1] // MOE_TM
    tile_valid = (tile_start < seg_end[-1]).astype(I32)
    tile_index = jnp.minimum(jnp.arange(n_tiles, dtype=I32), n_used - 1)
    tile_expert = jnp.sum((seg_end[None, :] <= (tile_index * MOE_TM)[:, None]).astype(I32), axis=1)
    prev = jnp.concatenate([jnp.full((1,), -1, I32), tile_expert[:-1]])
    tile_first = (tile_expert != prev).astype(I32)
    later_first = jnp.logical_and(tile_first == 1, tile_valid == 1)
    pos = jnp.arange(n_tiles, dtype=I32)
    first_pos = jnp.where(later_first, pos, n_tiles)
    next_pos = jnp.min(jnp.where(pos[None, :] > pos[:, None], first_pos[None, :], n_tiles), axis=1)
    tile_next = jnp.where(next_pos < n_tiles, tile_expert[jnp.minimum(next_pos, n_tiles - 1)], -1).astype(I32)
    tails = jnp.maximum(seg_end - MOE_TM, 0).astype(I32)
    return dest, (tile_expert, tile_first, tile_valid, tile_index, tile_next), tails


def _row_scatter_kernel(tail_ref, d0_ref, d1_ref, v_ref, x_hbm, zbuf, sem_z, sem):
    @pl.when(pl.program_id(0) == 0)
    def _():
        zbuf[...] = jnp.zeros_like(zbuf)
        def zero_tile(start):
            start = pl.multiple_of(start, MOE_TM)
            return pltpu.make_async_copy(zbuf, x_hbm.at[pl.ds(start, MOE_TM), :], sem_z)
        for e in range(N_EXPERTS):
            zero_tile(tail_ref[e]).start()
        for e in range(N_EXPERTS):
            zero_tile(tail_ref[e]).wait()
        first_unused = tail_ref[N_EXPERTS - 1] // MOE_TM + 1
        n_tiles = x_hbm.shape[0] // MOE_TM

        def start_unused(t, carry):
            zero_tile(t * MOE_TM).start()
            return carry

        def wait_unused(t, carry):
            zero_tile(t * MOE_TM).wait()
            return carry
        lax.fori_loop(first_unused, n_tiles, start_unused, 0)
        lax.fori_loop(first_unused, n_tiles, wait_unused, 0)

    def row_copies(g, k, d0, d1):
        src = v_ref.at[pl.ds(pl.multiple_of(g * 8, 8) + k, 1), :]
        return (pltpu.make_async_copy(src, x_hbm.at[pl.ds(d0, 1), :], sem),
                pltpu.make_async_copy(src, x_hbm.at[pl.ds(d1, 1), :], sem))

    def issue(g, carry):
        for k in range(8):
            for copy in row_copies(g, k, d0_ref[0, 0, g * 8 + k], d1_ref[0, 0, g * 8 + k]):
                copy.start()
        return carry
    lax.fori_loop(0, ROW_TM // 8, issue, 0)

    def drain(g, carry):
        for k in range(8):
            for copy in row_copies(g, k, 0, 0):
                copy.wait()
        return carry
    lax.fori_loop(0, ROW_TM // 8, drain, 0)


def _row_scatter(vp, dest, tails, n_rows):
    n_tok, half = vp.shape
    n_steps = n_tok // ROW_TM
    d0, d1 = dest
    grid_spec = pltpu.PrefetchScalarGridSpec(
        num_scalar_prefetch=1,
        grid=(n_steps,),
        in_specs=[
            pl.BlockSpec((1, 1, ROW_TM), lambda i, t: (i, 0, 0), memory_space=pltpu.SMEM),
            pl.BlockSpec((1, 1, ROW_TM), lambda i, t: (i, 0, 0), memory_space=pltpu.SMEM),
            pl.BlockSpec((ROW_TM, half), lambda i, t: (i, 0)),
        ],
        out_specs=pl.BlockSpec(memory_space=pl.ANY),
        scratch_shapes=[
            pltpu.VMEM((MOE_TM, half), U32),
            pltpu.SemaphoreType.DMA(()),
            pltpu.SemaphoreType.DMA(()),
        ],
    )
    return pl.pallas_call(
        _row_scatter_kernel,
        grid_spec=grid_spec,
        out_shape=jax.ShapeDtypeStruct((n_rows, half), U32),
        compiler_params=_cparams("arbitrary"),
        name="moe_row_scatter",
    )(tails, d0, d1, vp)


def _moe_kernel(te_ref, tf_ref, tv_ref, ti_ref, tn_ref, x_ref, w1_hbm, w3_hbm, w2_hbm, y_ref,
                s1, s3, s2, w1b, w3b, w2b, sems, *, layer):
    i = pl.program_id(0)
    valid = tv_ref[i] == 1

    def weight_copies(e):
        return (pltpu.make_async_copy(w1_hbm.at[layer, e], s1, sems.at[0]),
                pltpu.make_async_copy(w3_hbm.at[layer, e], s3, sems.at[1]),
                pltpu.make_async_copy(w2_hbm.at[layer, e], s2, sems.at[2]))

    @pl.when(i == 0)
    def _():
        for copy in weight_copies(te_ref[0]):
            copy.start()

    @pl.when(valid)
    def _():
        @pl.when(tf_ref[i] == 1)
        def _():
            for copy in weight_copies(te_ref[i]):
                copy.wait()
            for src, dst in ((s1, w1b), (s3, w3b), (s2, w2b)):
                rows_per = CAST_ROWS * 512 // src.shape[1]

                def cast_rows(c, carry, src=src, dst=dst, rows_per=rows_per):
                    rows = pl.ds(pl.multiple_of(c * rows_per, rows_per), rows_per)
                    dst[rows, :] = src[rows, :].astype(BF16)
                    return carry
                lax.fori_loop(0, src.shape[0] // rows_per, cast_rows, 0)

            @pl.when(tn_ref[i] >= 0)
            def _():
                for copy in weight_copies(tn_ref[i]):
                    copy.start()

        half = x_ref.shape[1]
        xp = x_ref[...]
        lo = lax.bitcast_convert_type(xp << 16, F32).astype(BF16)
        hi = lax.bitcast_convert_type(xp & jnp.uint32(0xFFFF0000), F32).astype(BF16)
        h1 = _dot(lo, w1b[0:half, :]) + _dot(hi, w1b[half:, :])
        h3 = _dot(lo, w3b[0:half, :]) + _dot(hi, w3b[half:, :])
        hh = (h1 * jax.nn.sigmoid(h1) * h3).astype(BF16)
        y_ref[...] = _dot(hh, w2b[...])

    @pl.when(jnp.logical_not(valid))
    def _():
        y_ref[...] = jnp.zeros_like(y_ref)


def _moe_experts(layer, x_sorted, tiles, w1, w3, w2):
    tile_expert, tile_first, tile_valid, tile_index, tile_next = tiles
    n_tiles = tile_expert.shape[0]
    half = x_sorted.shape[1]
    d = 2 * half
    grid_spec = pltpu.PrefetchScalarGridSpec(
        num_scalar_prefetch=5,
        grid=(n_tiles,),
        in_specs=[
            pl.BlockSpec((MOE_TM, half), lambda i, te, tf, tv, ti, tn: (ti[i], 0)),
            pl.BlockSpec(memory_space=pl.ANY),
            pl.BlockSpec(memory_space=pl.ANY),
            pl.BlockSpec(memory_space=pl.ANY),
        ],
        out_specs=pl.BlockSpec((MOE_TM, d), lambda i, te, tf, tv, ti, tn: (i, 0)),
        scratch_shapes=[
            pltpu.VMEM((d, D_EXPERT), F32),
            pltpu.VMEM((d, D_EXPERT), F32),
            pltpu.VMEM((D_EXPERT, d), F32),
            pltpu.VMEM((d, D_EXPERT), BF16),
            pltpu.VMEM((d, D_EXPERT), BF16),
            pltpu.VMEM((D_EXPERT, d), BF16),
            pltpu.SemaphoreType.DMA((3,)),
        ],
    )
    return pl.pallas_call(
        functools.partial(_moe_kernel, layer=layer),
        grid_spec=grid_spec,
        out_shape=jax.ShapeDtypeStruct((n_tiles * MOE_TM, d), F32),
        compiler_params=_cparams("arbitrary"),
        name="moe_experts",
    )(tile_expert, tile_first, tile_valid, tile_index, tile_next, x_sorted, w1, w3, w2)


def _combine_kernel(d0_ref, d1_ref, n0_ref, n1_ref, h_ref, route_ref, mod_ref, *refs, with_next):
    if with_next:
        modn_ref, gn_ref, y_hbm, o_ref, u_ref, ybuf, sems = refs
    else:
        y_hbm, o_ref, ybuf, sems = refs
    i = pl.program_id(0)
    n_steps = pl.num_programs(0)

    def row_copies(slot, g, k, d0, d1):
        rows = pl.ds(pl.multiple_of(g * 8, 8) + k, 1)
        return (pltpu.make_async_copy(y_hbm.at[pl.ds(d0, 1), :], ybuf.at[slot, 0, rows, :], sems.at[slot]),
                pltpu.make_async_copy(y_hbm.at[pl.ds(d1, 1), :], ybuf.at[slot, 1, rows, :], sems.at[slot]))

    def request(slot, i0_ref, i1_ref):
        def body(g, carry):
            for k in range(8):
                for copy in row_copies(slot, g, k, i0_ref[0, 0, g * 8 + k], i1_ref[0, 0, g * 8 + k]):
                    copy.start()
            return carry
        lax.fori_loop(0, ROW_TM // 8, body, 0)

    def await_rows(slot):
        def body(g, carry):
            for k in range(8):
                for copy in row_copies(slot, g, k, 0, 0):
                    copy.wait()
            return carry
        lax.fori_loop(0, ROW_TM // 8, body, 0)

    slot = i % 2

    @pl.when(i == 0)
    def _():
        request(0, d0_ref, d1_ref)

    for s in range(2):
        @pl.when(jnp.logical_and(i + 1 < n_steps, slot == s))
        def _(s=s):
            request(1 - s, n0_ref, n1_ref)

    for s in range(2):
        @pl.when(slot == s)
        def _(s=s):
            await_rows(s)
    p0 = route_ref[:, ROUTE_P:ROUTE_P + 1]
    p1 = route_ref[:, ROUTE_P + 1:ROUTE_P + 2]
    o_ref[...] = h_ref[...] + mod_ref[0, 5:6, :] * (p0 * ybuf[slot, 0] + p1 * ybuf[slot, 1])
    if with_next:
        _norm_modulate_rows((o_ref,), 0, gn_ref, modn_ref, 0, 1, u_ref)


def _combine(h1, route, mod, y_sorted, dest, next_norm, *, n_tok, n_lat, seq, batch):
    d = h1.shape[1]
    tm = ROW_TM
    n_steps = n_tok // tm
    d0, d1 = dest
    with_next = next_norm is not None

    def mod_idx(i):
        return (jnp.where(i * tm < n_lat, (i * tm) // seq, batch), 0, 0)

    row_spec = pl.BlockSpec((tm, d), lambda i: (i, 0))
    mod_spec = pl.BlockSpec((1, N_MOD, d), mod_idx)
    next_specs = [mod_spec, pl.BlockSpec((1, d), lambda i: (0, 0))] if with_next else []
    next_args = list(next_norm) if with_next else []
    out = pl.pallas_call(
        functools.partial(_combine_kernel, with_next=with_next),
        grid=(n_steps,),
        in_specs=[
            pl.BlockSpec((1, 1, tm), lambda i: (i, 0, 0), memory_space=pltpu.SMEM),
            pl.BlockSpec((1, 1, tm), lambda i: (i, 0, 0), memory_space=pltpu.SMEM),
            pl.BlockSpec((1, 1, tm), lambda i: (jnp.minimum(i + 1, n_steps - 1), 0, 0), memory_space=pltpu.SMEM),
            pl.BlockSpec((1, 1, tm), lambda i: (jnp.minimum(i + 1, n_steps - 1), 0, 0), memory_space=pltpu.SMEM),
            row_spec,
            pl.BlockSpec((tm, LANES), lambda i: (i, 0)),
            mod_spec,
        ] + next_specs + [pl.BlockSpec(memory_space=pl.ANY)],
        out_specs=[row_spec, row_spec] if with_next else row_spec,
        out_shape=([jax.ShapeDtypeStruct((n_tok, d), F32), jax.ShapeDtypeStruct((n_tok, d), BF16)] if with_next
                   else jax.ShapeDtypeStruct((n_tok, d), F32)),
        scratch_shapes=[pltpu.VMEM((2, 2, tm, d), F32), pltpu.SemaphoreType.DMA((2,))],
        compiler_params=_cparams("arbitrary"),
        name="moe_combine",
    )(d0, d1, d0, d1, h1, route, mod, *next_args, y_sorted)
    return out if with_next else (out, None)


def _rope_tables(seq, tm):
    n_rows = seq // GRID_W
    row = jnp.repeat(jnp.arange(n_rows), GRID_W).astype(F32)
    col = jnp.tile(jnp.arange(GRID_W), n_rows).astype(F32)
    half = HEAD_DIM // 2
    inv = 1.0 / (ROPE_THETA ** (jnp.arange(0, half, 2, dtype=F32) / half))
    ang_r = row[:, None] * inv
    ang_c = col[:, None] * inv
    ang = jnp.concatenate([ang_r, ang_c, ang_r, ang_c], axis=-1)
    cos, sin = jnp.cos(ang), jnp.sin(ang)
    sin_signed = jnp.where(jnp.arange(HEAD_DIM) < half, -sin, sin)
    ident = jnp.zeros((tm, HEAD_DIM), F32)
    return jnp.concatenate([cos, ident + 1.0]), jnp.concatenate([sin_signed, ident])


def _permute_heads(a):
    lead = a.shape[:-1]
    quarter = HEAD_DIM // 4
    assert HEAD_PERM[quarter] == 2 * quarter and HEAD_PERM[2 * quarter] == quarter
    q = a.reshape(*lead, a.shape[-1] // HEAD_DIM, 4, quarter)
    q = jnp.stack([q[..., 0, :], q[..., 2, :], q[..., 1, :], q[..., 3, :]], axis=-2)
    return q.reshape(*lead, a.shape[-1])


def _score_bound(gq, gk):
    return (1.02 * HEAD_DIM * jnp.max(jnp.abs(gq)) * jnp.max(jnp.abs(gk))).reshape(1).astype(F32)


def kernel(x, c, ctx, c_ctx, w_ada, b_ada, norm1_g, norm2_g, w_in, b_gate, q_norm_a, k_norm_a, diff_lambda,
           sub_norm_a, q_norm_c, k_norm_c, conv_w, conv_b, lru_w_a, lru_b_a, lru_w_x, lru_b_x, lru_lambda,
           w_branch_a, w_branch_b, w_branch_c, w_out, w_group, b_group, w_route, b_route, w1, w3, w2):
    batch, seq, d = x.shape
    ctx_len = ctx.shape[1]
    depth = w_ada.shape[0]
    n_lat = batch * seq
    n_ctx = batch * ctx_len
    n_all = n_lat + n_ctx
    tm = min(1024, seq, n_ctx)
    assert seq % tm == 0 and n_ctx % tm == 0 and n_lat % ctx_len == 0 and batch < MOD_ROWS
    assert seq % GRID_W == 0 and seq % ROW_TM == 0 and n_ctx % ROW_TM == 0
    tq = min(256, seq)
    tm_out = min(512, tm)

    cc = jnp.zeros((MOD_ROWS, d), F32).at[:batch].set(c).at[batch].set(c_ctx)
    mod_all = _ada_modulation(cc, w_ada, b_ada).reshape(depth, MOD_ROWS, N_MOD, d)
    rope = _rope_tables(seq, tm)
    h_src = (x.reshape(n_lat, d), ctx.reshape(n_ctx, d))
    scale = HEAD_DIM ** -0.5

    for l in range(depth):
        last = l == depth - 1
        lambda_init = 0.8 - 0.6 * math.exp(-0.3 * l)
        mod = mod_all[l]
        t_av, t_cq, t_gate = 2048 // PREP_TN, 5120 // PREP_TN, 6656 // PREP_TN
        w_qk = _prepare_weight(w_in, l, QK_COLS, lambda j: jnp.where(j < t_av, j, j + t_cq - t_av),
                               QK_CV // PREP_TN, "prep_w_qk")
        w_vb = _prepare_weight(w_in, l, VB_COLS, lambda j: j + t_av, 0, "prep_w_vb")
        w_gate = _prepare_weight(w_in, l, N_BRANCH * d, lambda j: j + t_gate, 0, "prep_w_gate")
        gq_a, gq_c = q_norm_a[l] * scale, q_norm_c[l] * scale
        gcol = jnp.concatenate([
            _permute_heads(jnp.concatenate([
                jnp.tile(gq_a, 2 * DIFF_HEADS), jnp.tile(k_norm_a[l], 2 * DIFF_HEADS),
                jnp.tile(gq_c, GQA_HEADS), jnp.tile(k_norm_c[l], GQA_KV_HEADS)])),
            jnp.ones((QK_COLS - QK_CV,), F32)]).reshape(1, QK_COLS)
        bound_a = _score_bound(gq_a, k_norm_a[l])
        bound_c = _score_bound(gq_c, k_norm_c[l])

        if l == 0:
            u = _norm_modulate_stream(h_src, mod, norm1_g[l].reshape(1, d), n_all=n_all, tm=tm,
                                      n_lat_tiles=n_lat // tm, tiles_per_seq=seq // tm, batch=batch)
        qk, vb = _in_projection(u, w_qk, w_vb, gcol, rope, tm=tm, n_lat_tiles=n_lat // tm,
                                tiles_per_seq=seq // tm)
        dims = dict(batch=batch, seq=seq, ctx=ctx_len)
        extra_a = (diff_lambda[l], sub_norm_a[l].reshape(1, DIFF_V_DIM))
        ya = _attention("diff", qk, vb, bound_a, extra_a, tq=tq, latent=True, lambda_init=lambda_init, **dims)
        yc = _attention("gqa", qk, vb, bound_c, None, tq=tq, latent=True, **dims)
        yb, yb_c = _rglru(vb, conv_w[l], conv_b[l].reshape(1, LRU_WIDTH), 0.5 * lru_w_a[l], 0.5 * lru_b_a[l],
                          0.5 * lru_w_x[l], 0.5 * lru_b_x[l], lru_lambda[l], need_ctx=not last, **dims)
        if last:
            n_tok, y_ctx = n_lat, None
        else:
            n_tok = n_all
            ya_c = _attention("diff", qk, vb, bound_a, extra_a, tq=tq, latent=False, lambda_init=lambda_init,
                              **dims)
            yc_c = _attention("gqa", qk, vb, bound_c, None, tq=tq, latent=False, **dims)
            y_ctx = (ya_c, yb_c, yc_c)
        w_branch = (w_branch_a[l].astype(BF16), w_branch_b[l].astype(BF16), w_branch_c[l].astype(BF16))
        m = _merge(u, (ya, yb, yc), y_ctx, w_gate, b_gate[l], w_branch, n_tok=n_tok, n_lat=n_lat, tm=tm)
        w_router = jnp.zeros((d, LANES), F32).at[:, :N_GROUPS].set(w_group[l])
        w_router = w_router.at[:, N_GROUPS:N_GROUPS + N_EXPERTS].set(w_route[l]).astype(BF16)
        b_router = jnp.zeros((1, LANES), F32).at[0, :N_GROUPS].set(b_group[l])
        b_router = b_router.at[0, N_GROUPS:N_GROUPS + N_EXPERTS].set(b_route[l])
        h1, vp, route, counts = _out_projection(m, h_src, mod, norm2_g[l].reshape(1, d),
                                                w_out[l].astype(BF16), w_router, b_router, n_tok=n_tok,
                                                tm=tm_out, n_lat=n_lat, seq=seq, batch=batch)
        dest, tiles, tails = _dispatch(route, counts, n_tok, tm_out)
        x_sorted = _row_scatter(vp, dest, tails, tiles[0].shape[0] * MOE_TM)
        y_sorted = _moe_experts(l, x_sorted, tiles, w1, w3, w2)
        next_norm = None if last else (mod_all[l + 1], norm1_g[l + 1].reshape(1, d))
        h_src, u = _combine(h1, route, mod, y_sorted, dest, next_norm, n_tok=n_tok, n_lat=n_lat, seq=seq,
                            batch=batch)
    return h_src.reshape(batch, seq, d)
```

```python
import functools
import math

import jax
import jax.numpy as jnp
from jax import lax
from jax.experimental import pallas as pl
from jax.experimental.pallas import tpu as pltpu

F32 = jnp.float32
BF16 = jnp.bfloat16
I32 = jnp.int32
U32 = jnp.uint32

EPS = 1e-6
HEAD_DIM = 128
GRID_W = 64
ROPE_THETA = 10000.0
DIFF_HEADS = 4
DIFF_V_DIM = 2 * HEAD_DIM
LRU_WIDTH = 1024
LRU_BLOCK_W = 128
CONV_W = 4
LRU_C = 8.0
GQA_HEADS = 8
GQA_KV_HEADS = 2
N_BRANCH = 3
BRANCH_W = 1024
N_GROUPS = 4
EXPERTS_PER_GROUP = 8
N_EXPERTS = N_GROUPS * EXPERTS_PER_GROUP
D_EXPERT = 512
N_MOD = 6
MOD_ROWS = 16
LANES = 128

QK_AQ, QK_AK, QK_CQ, QK_CK, QK_CV = 0, 1024, 2048, 3072, 3328
VB_AV, VB_BX, VB_BY = 0, 1024, 2048
QK_COLS = 3584
VB_COLS = 3072
QK_TN = QK_COLS // 2
VB_TN = VB_COLS // 2
PROJ_CW = 256
PROJ_RC = 512
HEAD_PERM = tuple(range(0, 32)) + tuple(range(64, 96)) + tuple(range(32, 64)) + tuple(range(96, 128))

VMEM_LIMIT = 56 * 1024 * 1024


def _cparams(*sem):
    return pltpu.CompilerParams(dimension_semantics=sem, vmem_limit_bytes=VMEM_LIMIT)


def _dot(a, b):
    return jnp.dot(a, b, preferred_element_type=F32)


def _dot_nt(a, b):
    return lax.dot_general(a, b, (((1,), (1,)), ((), ())), preferred_element_type=F32)


def _lane_mean(x2):
    return _dot(x2.astype(BF16), jnp.full((LANES, LANES), 1.0 / LANES, BF16))


def _ada_kernel(c_ref, w_ref, b_ref, o_ref):
    c = c_ref[...]
    s = (c * jax.nn.sigmoid(c)).astype(BF16)
    o_ref[0] = _dot(s, w_ref[0].astype(BF16)) + b_ref[0]


def _ada_modulation(cc, w_ada, b_ada):
    depth, d, n = w_ada.shape
    tn = 1024
    return pl.pallas_call(
        _ada_kernel,
        grid=(depth, n // tn),
        in_specs=[
            pl.BlockSpec((MOD_ROWS, d), lambda l, j: (0, 0)),
            pl.BlockSpec((1, d, tn), lambda l, j: (l, 0, j)),
            pl.BlockSpec((1, 1, tn), lambda l, j: (l, 0, j)),
        ],
        out_specs=pl.BlockSpec((1, MOD_ROWS, tn), lambda l, j: (l, 0, j)),
        out_shape=jax.ShapeDtypeStruct((depth, MOD_ROWS, n), F32),
        compiler_params=_cparams("parallel", "parallel"),
        name="ada_modulation",
    )(cc, w_ada, b_ada.reshape(depth, 1, n))


PREP_TN = 256


def _wprep_kernel(w_ref, o_ref, *, n_perm_tiles):
    def plain():
        o_ref[...] = w_ref[0].astype(BF16)

    def permuted():
        quarter = lax.broadcasted_iota(I32, (1, HEAD_DIM), 1) // (HEAD_DIM // 4)
        for c in range(PREP_TN // HEAD_DIM):
            sl = slice(c * HEAD_DIM, (c + 1) * HEAD_DIM)
            x = w_ref[0, :, sl]
            y = jnp.where(quarter == 1, pltpu.roll(x, 96, 1), jnp.where(quarter == 2, pltpu.roll(x, 32, 1), x))
            o_ref[:, sl] = y.astype(BF16)

    if n_perm_tiles == 0:
        plain()
        return
    is_perm = pl.program_id(0) < n_perm_tiles
    pl.when(is_perm)(permuted)
    pl.when(jnp.logical_not(is_perm))(plain)


def _prepare_weight(w_in, layer, n_cols, col_tile_of, n_perm_tiles, name):
    d = w_in.shape[1]
    return pl.pallas_call(
        functools.partial(_wprep_kernel, n_perm_tiles=n_perm_tiles),
        grid=(n_cols // PREP_TN,),
        in_specs=[pl.BlockSpec((1, d, PREP_TN), lambda j: (layer, 0, col_tile_of(j)))],
        out_specs=pl.BlockSpec((d, PREP_TN), lambda j: (0, j)),
        out_shape=jax.ShapeDtypeStruct((d, n_cols), BF16),
        compiler_params=_cparams("parallel"),
        name=name,
    )(w_in)


NORM_ROWS = 32
NORM_UNROLL = 4


def _stream_specs(h_src, tm, n_lat, n_tiles):
    nl = n_lat // tm
    if isinstance(h_src, tuple) and n_tiles > nl:
        d = h_src[0].shape[1]
        return list(h_src), [pl.BlockSpec((tm, d), lambda i: (jnp.minimum(i, nl - 1), 0)),
                             pl.BlockSpec((tm, d), lambda i: (jnp.maximum(i - nl, 0), 0))]
    arr = h_src[0] if isinstance(h_src, tuple) else h_src
    return [arr], [pl.BlockSpec((tm, arr.shape[1]), lambda i: (i, 0))]


def _read_stream(h_refs, rows, n_lat_tiles):
    if len(h_refs) == 1:
        return h_refs[0][rows, :]
    return jnp.where(pl.program_id(0) < n_lat_tiles, h_refs[0][rows, :], h_refs[1][rows, :])


def _norm_modulate_rows(h_refs, n_lat_tiles, g_ref, mod_ref, shift_row, scale_row, out_ref):
    g = g_ref[...]
    sc = 1.0 + mod_ref[0, scale_row:scale_row + 1, :]
    sh = mod_ref[0, shift_row:shift_row + 1, :]

    def body(i, carry):
        rows = pl.ds(pl.multiple_of(i * NORM_ROWS, NORM_ROWS), NORM_ROWS)
        x = _read_stream(h_refs, rows, n_lat_tiles)
        ms = jnp.mean(x * x, axis=-1, keepdims=True)
        y = (x * lax.rsqrt(ms + EPS) * g) * sc + sh
        out_ref[rows, :] = y.astype(out_ref.dtype)
        return carry
    lax.fori_loop(0, out_ref.shape[0] // NORM_ROWS, body, 0, unroll=NORM_UNROLL)


def _norm_kernel(*refs, n_lat_tiles):
    mod_ref, g_ref, u_ref = refs[-3:]
    _norm_modulate_rows(refs[:-3], n_lat_tiles, g_ref, mod_ref, 0, 1, u_ref)


def _project_tile(u_ref, w_ref, epilogue):
    tm, tn = u_ref.shape[0], w_ref.shape[1]
    rc = min(PROJ_RC, tm)
    pending = []
    for c in range(tn // PROJ_CW):
        cols = slice(c * PROJ_CW, (c + 1) * PROJ_CW)
        accs = []
        for r in range(tm // rc):
            rows = slice(r * rc, (r + 1) * rc)
            accs.append((rows, cols, _dot(u_ref[rows, :], w_ref[:, cols])))
        for item in pending:
            epilogue(*item)
        pending = accs
    for item in pending:
        epilogue(*item)


def _inproj_qk_kernel(u_ref, w_ref, gc_ref, cos_ref, sin_ref, qk_ref):
    last_tile = pl.program_id(1) == pl.num_programs(1) - 1
    tn = w_ref.shape[1]

    def epilogue(rows, cols, acc):
        plain = jnp.logical_and(last_tile, cols.stop == tn)
        for c in range(PROJ_CW // HEAD_DIM):
            sl = slice(cols.start + c * HEAD_DIM, cols.start + (c + 1) * HEAD_DIM)
            x = acc[:, c * HEAD_DIM:(c + 1) * HEAD_DIM]
            y = x * lax.rsqrt(_lane_mean(x * x) + EPS) * gc_ref[:, sl]
            y = y * cos_ref[rows, :] + pltpu.roll(y, HEAD_DIM // 2, 1) * sin_ref[rows, :]
            qk_ref[rows, sl] = jnp.where(plain, x, y).astype(BF16)

    _project_tile(u_ref, w_ref, epilogue)


def _inproj_vb_kernel(u_ref, w_ref, vb_ref):
    def epilogue(rows, cols, acc):
        vb_ref[rows, cols] = acc.astype(BF16)

    _project_tile(u_ref, w_ref, epilogue)


def _norm_modulate_stream(h_src, mod, g1, *, n_all, tm, n_lat_tiles, tiles_per_seq, batch):
    d = g1.shape[1]

    def mod_idx(i):
        return (jnp.where(i < n_lat_tiles, i // tiles_per_seq, batch), 0, 0)

    h_args, h_specs = _stream_specs(h_src, tm, n_lat_tiles * tm, n_all // tm)
    return pl.pallas_call(
        functools.partial(_norm_kernel, n_lat_tiles=n_lat_tiles),
        grid=(n_all // tm,),
        in_specs=h_specs + [
            pl.BlockSpec((1, N_MOD, d), mod_idx),
            pl.BlockSpec((1, d), lambda i: (0, 0)),
        ],
        out_specs=pl.BlockSpec((tm, d), lambda i: (i, 0)),
        out_shape=jax.ShapeDtypeStruct((n_all, d), BF16),
        compiler_params=_cparams("parallel"),
        name="norm_modulate",
    )(*h_args, mod, g1)


def _in_projection(u, w_qk, w_vb, gcol, rope, *, tm, n_lat_tiles, tiles_per_seq):
    n_all, d = u.shape
    cos, sin = rope

    def rope_idx(i, j):
        return (jnp.where(i < n_lat_tiles, i % tiles_per_seq, tiles_per_seq), 0)

    qk = pl.pallas_call(
        _inproj_qk_kernel,
        grid=(n_all // tm, QK_COLS // QK_TN),
        in_specs=[
            pl.BlockSpec((tm, d), lambda i, j: (i, 0)),
            pl.BlockSpec((d, QK_TN), lambda i, j: (0, j)),
            pl.BlockSpec((1, QK_TN), lambda i, j: (0, j)),
            pl.BlockSpec((tm, HEAD_DIM), rope_idx),
            pl.BlockSpec((tm, HEAD_DIM), rope_idx),
        ],
        out_specs=pl.BlockSpec((tm, QK_TN), lambda i, j: (i, j)),
        out_shape=jax.ShapeDtypeStruct((n_all, QK_COLS), BF16),
        compiler_params=_cparams("parallel", "arbitrary"),
        name="in_projection_qk",
    )(u, w_qk, gcol, cos, sin)
    vb = pl.pallas_call(
        _inproj_vb_kernel,
        grid=(n_all // tm, VB_COLS // VB_TN),
        in_specs=[
            pl.BlockSpec((tm, d), lambda i, j: (i, 0)),
            pl.BlockSpec((d, VB_TN), lambda i, j: (0, j)),
        ],
        out_specs=pl.BlockSpec((tm, VB_TN), lambda i, j: (i, j)),
        out_shape=jax.ShapeDtypeStruct((n_all, VB_COLS), BF16),
        compiler_params=_cparams("parallel", "arbitrary"),
        name="in_projection_vb",
    )(u, w_vb)
    return qk, vb


ATT_KC = 256
SAFE_SHIFT = 40.0
GQA_STACK = 1


def _key_chunks(k_refs, v_refs):
    chunks = []
    for k_ref, v_ref in zip(k_refs, v_refs):
        n = k_ref.shape[0]
        for s in range(0, n, ATT_KC):
            chunks.append((k_ref, v_ref, s, min(ATT_KC, n - s)))
    return chunks


def _fill_shifts(bound_ref, m_scr, q_ref, chunks, heads):
    bound = bound_ref[0]

    @pl.when(bound <= SAFE_SHIFT)
    def _():
        m_scr[...] = jnp.full(m_scr.shape, bound, F32)

    @pl.when(bound > SAFE_SHIFT)
    def _():
        for idx, (qcol, kcol) in enumerate(heads):
            q = q_ref[:, qcol:qcol + HEAD_DIM]
            m = None
            for k_ref, _, s, n in chunks:
                part = jnp.max(_dot_nt(q, k_ref[s:s + n, kcol:kcol + HEAD_DIM]), axis=-1, keepdims=True)
                m = part if m is None else jnp.maximum(m, part)
            m_scr[idx] = jnp.broadcast_to(m, m_scr.shape[1:])


def _stream_softmax_pv(streams, chunks, vcols):
    o = None
    ls = [None] * len(streams)
    for k_ref, v_ref, s, n in chunks:
        es = []
        for idx, (q, shift, kcol) in enumerate(streams):
            sc = _dot_nt(q, k_ref[s:s + n, kcol:kcol + HEAD_DIM])
            tiles = [jnp.exp(sc[:, t * LANES:(t + 1) * LANES] - shift) for t in range(n // LANES)]
            for e_t in tiles:
                ls[idx] = e_t if ls[idx] is None else ls[idx] + e_t
            es.append(tiles[0] if len(tiles) == 1 else jnp.concatenate(tiles, axis=1))
        e = es[0] if len(es) == 1 else jnp.concatenate(es, axis=0)
        pv = _dot(e.astype(BF16), v_ref[s:s + n, vcols])
        o = pv if o is None else o + pv
    rows = streams[0][0].shape[0]
    return [(o[i * rows:(i + 1) * rows, :], jnp.sum(ls[i], axis=-1, keepdims=True)) for i in range(len(streams))]


def _diff_attn_kernel(bound_ref, dl_ref, sg_ref, q_ref, *refs, lambda_init, n_pieces):
    k_refs = refs[0:2 * n_pieces:2]
    v_refs = refs[1:2 * n_pieces:2]
    o_ref = refs[2 * n_pieces]
    m_scr = refs[2 * n_pieces + 1]
    chunks = _key_chunks(k_refs, v_refs)
    heads = [(s * HEAD_DIM, s * HEAD_DIM) for s in range(2 * DIFF_HEADS)]
    _fill_shifts(bound_ref, m_scr, q_ref, chunks, heads)
    dl = dl_ref[...]
    lam = (jnp.exp(jnp.sum(dl[0:1] * dl[1:2], axis=-1, keepdims=True))
           - jnp.exp(jnp.sum(dl[2:3] * dl[3:4], axis=-1, keepdims=True)) + lambda_init)
    for h in range(DIFF_HEADS):
        c1 = 2 * h * HEAD_DIM
        c2 = c1 + HEAD_DIM
        vs = slice(h * DIFF_V_DIM, (h + 1) * DIFF_V_DIM)
        (o1, l1), = _stream_softmax_pv([(q_ref[:, c1:c1 + HEAD_DIM], m_scr[2 * h], c1)], chunks, vs)
        (o2, l2), = _stream_softmax_pv([(q_ref[:, c2:c2 + HEAD_DIM], m_scr[2 * h + 1], c2)], chunks, vs)
        o = o1 * (1.0 / l1) - o2 * (lam / l2)
        ms = jnp.mean(o * o, axis=-1, keepdims=True)
        o = o * lax.rsqrt(ms + EPS) * sg_ref[...] * (1.0 - lambda_init)
        o_ref[:, vs] = o.astype(BF16)


def _gqa_kernel(bound_ref, q_ref, *refs, n_pieces):
    k_refs = refs[0:2 * n_pieces:2]
    v_refs = refs[1:2 * n_pieces:2]
    o_ref = refs[2 * n_pieces]
    m_scr = refs[2 * n_pieces + 1]
    chunks = _key_chunks(k_refs, v_refs)
    group = GQA_HEADS // GQA_KV_HEADS
    heads = [(h * HEAD_DIM, (h // group) * HEAD_DIM) for h in range(GQA_HEADS)]
    _fill_shifts(bound_ref, m_scr, q_ref, chunks, heads)
    tq = q_ref.shape[0]
    for h0 in range(0, GQA_HEADS, GQA_STACK):
        hs = range(h0, h0 + GQA_STACK)
        kcol = heads[h0][1]
        q = jnp.concatenate([q_ref[:, heads[h][0]:heads[h][0] + HEAD_DIM] for h in hs], axis=0)
        shift = jnp.concatenate([m_scr[h] for h in hs], axis=0)
        (o, l), = _stream_softmax_pv([(q, shift, kcol)], chunks, slice(kcol, kcol + HEAD_DIM))
        o = (o * (1.0 / l)).astype(BF16)
        for idx, h in enumerate(hs):
            o_ref[:, heads[h][0]:heads[h][0] + HEAD_DIM] = o[idx * tq:(idx + 1) * tq, :]


def _attention(kind, qk, vb, bound, extra, *, batch, seq, ctx, tq, latent, lambda_init=None):
    n_lat = batch * seq
    ctx_blk0 = n_lat // ctx
    if kind == "diff":
        qcol, kcol, vcol, kvw, v_arr = QK_AQ // 1024, QK_AK // 1024, VB_AV // 1024, 1024, vb
    else:
        qcol, kcol, vcol, kvw, v_arr = QK_CQ // 1024, QK_CK // 256, QK_CV // 256, 256, qk
    if latent:
        q_tiles = seq // tq
        q_spec = pl.BlockSpec((tq, 1024), lambda b, i: (b * q_tiles + i, qcol))
        o_spec = pl.BlockSpec((tq, 1024), lambda b, i: (b * q_tiles + i, 0))
        n_q = n_lat
    else:
        q_tiles = 1
        tq = ctx
        q_spec = pl.BlockSpec((ctx, 1024), lambda b, i: (ctx_blk0 + b, qcol))
        o_spec = pl.BlockSpec((ctx, 1024), lambda b, i: (b, 0))
        n_q = batch * ctx
    kv_specs = [
        pl.BlockSpec((ctx, kvw), lambda b, i: (ctx_blk0 + b, kcol)),
        pl.BlockSpec((ctx, kvw), lambda b, i: (ctx_blk0 + b, vcol)),
    ]
    kv_args = [qk, v_arr]
    if latent:
        kv_specs += [
            pl.BlockSpec((seq, kvw), lambda b, i: (b, kcol)),
            pl.BlockSpec((seq, kvw), lambda b, i: (b, vcol)),
        ]
        kv_args += [qk, v_arr]
    n_pieces = len(kv_args) // 2
    pre_specs = [pl.BlockSpec(memory_space=pltpu.SMEM)]
    pre_args = [bound]
    if kind == "diff":
        diff_lambda, sub_g = extra
        body = functools.partial(_diff_attn_kernel, lambda_init=lambda_init, n_pieces=n_pieces)
        pre_specs += [
            pl.BlockSpec((4, HEAD_DIM), lambda b, i: (0, 0)),
            pl.BlockSpec((1, DIFF_V_DIM), lambda b, i: (0, 0)),
        ]
        pre_args += [diff_lambda, sub_g]
        n_softmax = 2 * DIFF_HEADS
    else:
        body = functools.partial(_gqa_kernel, n_pieces=n_pieces)
        n_softmax = GQA_HEADS
    return pl.pallas_call(
        body,
        grid=(batch, q_tiles),
        in_specs=pre_specs + [q_spec] + kv_specs,
        out_specs=o_spec,
        out_shape=jax.ShapeDtypeStruct((n_q, 1024), BF16),
        scratch_shapes=[pltpu.VMEM((n_softmax, tq, LANES), F32)],
        compiler_params=_cparams("parallel", "arbitrary"),
        name=f"{kind}_attn_{'lat' if latent else 'ctx'}",
    )(*pre_args, qk, *kv_args)


LRU_CW = 512
LRU_PAD = 8
SQRT_GUARD = 1e-30


def _lru_kernel(bxc_ref, byc_ref, bxl_ref, byl_ref, cw_ref, cb_ref, wa_ref, ba_ref, wx_ref, bx_ref, lam_ref,
                *refs, ctx, seq, need_ctx):
    if need_ctx:
        yc_ref, yl_ref = refs[0], refs[1]
        scratch = refs[2:]
    else:
        yc_ref, yl_ref = None, refs[0]
        scratch = refs[1:]
    xpad, a_f, b_f, a_b, b_b = scratch
    a_scr = (a_f, a_b)
    b_scr = (b_f, b_b)
    lam = lam_ref[...]
    neg_sp = -LRU_C * jax.nn.softplus(-lam)

    def gates(x_ref, n):
        zeros = jnp.zeros((LRU_PAD, LRU_CW), F32)
        xpad[0:LRU_PAD, :] = zeros
        xpad[LRU_PAD:LRU_PAD + n, :] = x_ref[...].astype(F32)
        xpad[LRU_PAD + n:2 * LRU_PAD + n, :] = zeros
        xc = cb_ref[...] + cw_ref[0:1, :] * xpad[LRU_PAD - 2:LRU_PAD - 2 + n, :]
        for k in range(1, CONV_W):
            xc = xc + cw_ref[k:k + 1, :] * xpad[LRU_PAD - 2 + k:LRU_PAD - 2 + k + n, :]
        xcb = xc.astype(BF16)
        for d in range(2):
            for blk in range(LRU_CW // LRU_BLOCK_W):
                sl = slice(blk * LRU_BLOCK_W, (blk + 1) * LRU_BLOCK_W)
                xs = xcb[:, sl]
                r = 0.5 * jnp.tanh(_dot(xs, wa_ref[d, blk].astype(BF16)) + ba_ref[d:d + 1, sl]) + 0.5
                gi = 0.5 * jnp.tanh(_dot(xs, wx_ref[d, blk].astype(BF16)) + bx_ref[d:d + 1, sl]) + 0.5
                log_a = r * neg_sp[d:d + 1, sl]
                a = jnp.exp(log_a)
                one_m_a2 = 1.0 - a * a
                mult = one_m_a2 * lax.rsqrt(jnp.maximum(one_m_a2, SQRT_GUARD))
                a_scr[d][0:n, sl] = a
                b_scr[d][0:n, sl] = mult * gi * xc[:, sl]

    def scan(n, hf0, hb0):
        def body(t, carry):
            hf, hb = carry
            tb = n - 1 - t
            hf = a_f[pl.ds(t, 1), :] * hf + b_f[pl.ds(t, 1), :]
            b_f[pl.ds(t, 1), :] = hf
            hb = a_b[pl.ds(tb, 1), :] * hb + b_b[pl.ds(tb, 1), :]
            b_b[pl.ds(tb, 1), :] = hb
            return hf, hb
        return lax.fori_loop(0, n, body, (hf0, hb0), unroll=8)

    zero = jnp.zeros((1, LRU_CW), F32)
    gates(bxc_ref, ctx)
    hf, hb = scan(ctx, zero, zero)
    if need_ctx:
        yc_ref[...] = ((b_f[0:ctx, :] + b_b[0:ctx, :])
                       * jax.nn.gelu(byc_ref[...].astype(F32), approximate=True)).astype(BF16)
    gates(bxl_ref, seq)
    scan(seq, hf, hb)
    yl_ref[...] = ((b_f[0:seq, :] + b_b[0:seq, :])
                   * jax.nn.gelu(byl_ref[...].astype(F32), approximate=True)).astype(BF16)


def _rglru(vb, conv_w, conv_b, w_a, b_a, w_x, b_x, lam, *, batch, seq, ctx, need_ctx):
    n_lat = batch * seq
    ctx_blk0 = n_lat // ctx
    cbx, cby = VB_BX // LRU_CW, VB_BY // LRU_CW
    nblk = LRU_CW // LRU_BLOCK_W
    tmax = max(seq, ctx)
    out_specs = [pl.BlockSpec((seq, LRU_CW), lambda b, c: (b, c))]
    out_shape = [jax.ShapeDtypeStruct((n_lat, LRU_WIDTH), BF16)]
    if need_ctx:
        out_specs = [pl.BlockSpec((ctx, LRU_CW), lambda b, c: (b, c))] + out_specs
        out_shape = [jax.ShapeDtypeStruct((batch * ctx, LRU_WIDTH), BF16)] + out_shape
    outs = pl.pallas_call(
        functools.partial(_lru_kernel, ctx=ctx, seq=seq, need_ctx=need_ctx),
        grid=(batch, LRU_WIDTH // LRU_CW),
        in_specs=[
            pl.BlockSpec((ctx, LRU_CW), lambda b, c: (ctx_blk0 + b, cbx + c)),
            pl.BlockSpec((ctx, LRU_CW), lambda b, c: (ctx_blk0 + b, cby + c)),
            pl.BlockSpec((seq, LRU_CW), lambda b, c: (b, cbx + c)),
            pl.BlockSpec((seq, LRU_CW), lambda b, c: (b, cby + c)),
            pl.BlockSpec((CONV_W, LRU_CW), lambda b, c: (0, c)),
            pl.BlockSpec((1, LRU_CW), lambda b, c: (0, c)),
            pl.BlockSpec((2, nblk, LRU_BLOCK_W, LRU_BLOCK_W), lambda b, c: (0, c, 0, 0)),
            pl.BlockSpec((2, LRU_CW), lambda b, c: (0, c)),
            pl.BlockSpec((2, nblk, LRU_BLOCK_W, LRU_BLOCK_W), lambda b, c: (0, c, 0, 0)),
            pl.BlockSpec((2, LRU_CW), lambda b, c: (0, c)),
            pl.BlockSpec((2, LRU_CW), lambda b, c: (0, c)),
        ],
        out_specs=out_specs,
        out_shape=out_shape,
        scratch_shapes=[
            pltpu.VMEM((tmax + 2 * LRU_PAD, LRU_CW), F32),
            pltpu.VMEM((tmax, LRU_CW), F32),
            pltpu.VMEM((tmax, LRU_CW), F32),
            pltpu.VMEM((tmax, LRU_CW), F32),
            pltpu.VMEM((tmax, LRU_CW), F32),
        ],
        compiler_params=_cparams("parallel", "arbitrary"),
        name="rglru",
    )(vb, vb, vb, vb, conv_w, conv_b, w_a, b_a, w_x, b_x, lam)
    if need_ctx:
        return outs[1], outs[0]
    return outs[0], None


MERGE_TN = 512


def _merge_kernel(u_ref, *refs, n_lat_tiles, has_ctx):
    n_y = 6 if has_ctx else 3
    y_refs = refs[:n_y]
    wg_refs = refs[n_y:n_y + 3]
    bg_ref = refs[n_y + 3]
    wbr_refs = refs[n_y + 4:n_y + 7]
    m_ref = refs[n_y + 7]

    def body(ys):
        u = u_ref[...]
        m = None
        for k in range(N_BRANCH):
            g = jax.nn.sigmoid(_dot(u, wg_refs[k][...]) + bg_ref[k:k + 1, :])
            part = g * _dot(ys[k][...], wbr_refs[k][...])
            m = part if m is None else m + part
        m_ref[...] = m.astype(BF16)

    if not has_ctx:
        body(y_refs)
        return
    is_lat = pl.program_id(0) < n_lat_tiles

    @pl.when(is_lat)
    def _():
        body(y_refs[0:3])

    @pl.when(jnp.logical_not(is_lat))
    def _():
        body(y_refs[3:6])


def _merge(u, y_lat, y_ctx, w_gate, b_gate, w_branch, *, n_tok, n_lat, tm):
    d = u.shape[1]
    nj = d // MERGE_TN
    nl = n_lat // tm
    has_ctx = y_ctx is not None
    y_specs = [pl.BlockSpec((tm, BRANCH_W), lambda i, j: (jnp.minimum(i, nl - 1), 0))] * 3
    y_args = list(y_lat)
    if has_ctx:
        y_specs += [pl.BlockSpec((tm, BRANCH_W), lambda i, j: (jnp.maximum(i - nl, 0), 0),
                                 pipeline_mode=pl.Buffered(1))] * 3
        y_args += list(y_ctx)
    wbr_spec = pl.BlockSpec((BRANCH_W, MERGE_TN), lambda i, j: (0, j))
    return pl.pallas_call(
        functools.partial(_merge_kernel, n_lat_tiles=nl, has_ctx=has_ctx),
        grid=(n_tok // tm, nj),
        in_specs=[pl.BlockSpec((tm, d), lambda i, j: (i, 0))] + y_specs + [
            pl.BlockSpec((d, MERGE_TN), lambda i, j: (0, j)),
            pl.BlockSpec((d, MERGE_TN), lambda i, j: (0, nj + j)),
            pl.BlockSpec((d, MERGE_TN), lambda i, j: (0, 2 * nj + j)),
            pl.BlockSpec((N_BRANCH, MERGE_TN), lambda i, j: (0, j)),
            wbr_spec, wbr_spec, wbr_spec,
        ],
        out_specs=pl.BlockSpec((tm, MERGE_TN), lambda i, j: (i, j)),
        out_shape=jax.ShapeDtypeStruct((n_tok, d), BF16),
        compiler_params=_cparams("parallel", "arbitrary"),
        name="branch_merge",
    )(u, *y_args, w_gate, w_gate, w_gate, b_gate, *w_branch)


OUT_RC = 256
ROUTE_E, ROUTE_P, ROUTE_RANK = 0, 2, 4
COUNT_ROWS = 8


def _outproj_kernel(m_ref, *refs, n_lat_tiles):
    mod_ref, g_ref, wo_ref, wr_ref, br_ref, h1_ref, vp_ref, route_ref, cnt_ref, v_scr = refs[-10:]
    h_refs = refs[:-10]
    tm, d = h1_ref.shape
    rc = min(OUT_RC, tm)
    for r in range(tm // rc):
        rows = slice(r * rc, (r + 1) * rc)
        h1 = (_read_stream(h_refs, rows, n_lat_tiles)
              + mod_ref[0, 2:3, :] * _dot(m_ref[rows, :], wo_ref[...]))
        h1_ref[rows, :] = h1
        ms = jnp.mean(h1 * h1, axis=-1, keepdims=True)
        y = h1 * lax.rsqrt(ms + EPS) * g_ref[...]
        v_scr[rows, :] = (y * (1.0 + mod_ref[0, 4:5, :]) + mod_ref[0, 3:4, :]).astype(BF16)
    v = v_scr[...]
    half = d // 2
    lo = lax.bitcast_convert_type(v[:, :half].astype(F32), U32)
    hi = lax.bitcast_convert_type(v[:, half:].astype(F32), U32)
    vp_ref[...] = (lo >> 16) | (hi & jnp.uint32(0xFFFF0000))

    logits = _dot(v, wr_ref[...]) + br_ref[...]
    lane = lax.broadcasted_iota(I32, logits.shape, 1).astype(F32)
    neg = jnp.float32(-jnp.inf)
    big = jnp.float32(1e9)

    def masked_argmax(mask):
        val = jnp.max(jnp.where(mask, logits, neg), axis=-1, keepdims=True)
        idx = jnp.min(jnp.where(jnp.logical_and(mask, logits == val), lane, big), axis=-1, keepdims=True)
        return val, idx

    gmask = lane < N_GROUPS
    gmax, gidx = masked_argmax(gmask)
    gsum = jnp.sum(jnp.where(gmask, jnp.exp(logits - gmax), 0.0), axis=-1, keepdims=True)
    g_w = 1.0 / gsum
    lo_lane = N_GROUPS + EXPERTS_PER_GROUP * gidx
    emask = jnp.logical_and(lane >= lo_lane, lane < lo_lane + EXPERTS_PER_GROUP)
    v1, i1 = masked_argmax(emask)
    v2, i2 = masked_argmax(jnp.logical_and(emask, lane != i1))
    e21 = jnp.exp(v2 - v1)
    p1 = g_w / (1.0 + e21)
    p2 = g_w * e21 / (1.0 + e21)
    e1 = i1 - N_GROUPS
    e2 = i2 - N_GROUPS
    oh1 = jnp.where(lane == e1, 1.0, 0.0)
    oh2 = jnp.where(lane == e2, 1.0, 0.0)
    cnt = oh1 + oh2
    row = lax.broadcasted_iota(I32, (tm, tm), 0)
    col = lax.broadcasted_iota(I32, (tm, tm), 1)
    before = jnp.where(row > col, 1.0, 0.0).astype(BF16)
    prefix = _dot(before, cnt.astype(BF16))
    rank1 = jnp.sum(prefix * oh1, axis=-1, keepdims=True)
    rank2 = jnp.sum(prefix * oh2, axis=-1, keepdims=True)
    route = jnp.where(lane == 0, e1, jnp.where(lane == 1, e2, jnp.where(lane == 2, p1, jnp.where(
        lane == 3, p2, jnp.where(lane == 4, rank1, jnp.where(lane == 5, rank2, 0.0))))))
    route_ref[...] = route
    cnt_ref[0] = jnp.broadcast_to(jnp.sum(cnt, axis=0, keepdims=True), (COUNT_ROWS, LANES))


def _out_projection(m, h_src, mod, g2, w_out, w_router, b_router, *, n_tok, tm, n_lat, seq, batch):
    d = m.shape[1]
    n_tiles = n_tok // tm

    def mod_idx(i):
        return (jnp.where(i * tm < n_lat, (i * tm) // seq, batch), 0, 0)

    h_args, h_specs = _stream_specs(h_src, tm, n_lat, n_tiles)
    return pl.pallas_call(
        functools.partial(_outproj_kernel, n_lat_tiles=n_lat // tm),
        grid=(n_tiles,),
        in_specs=[pl.BlockSpec((tm, d), lambda i: (i, 0))] + h_specs + [
            pl.BlockSpec((1, N_MOD, d), mod_idx),
            pl.BlockSpec((1, d), lambda i: (0, 0)),
            pl.BlockSpec((d, d), lambda i: (0, 0), pipeline_mode=pl.Buffered(1)),
            pl.BlockSpec((d, LANES), lambda i: (0, 0)),
            pl.BlockSpec((1, LANES), lambda i: (0, 0)),
        ],
        out_specs=[
            pl.BlockSpec((tm, d), lambda i: (i, 0)),
            pl.BlockSpec((tm, d // 2), lambda i: (i, 0)),
            pl.BlockSpec((tm, LANES), lambda i: (i, 0)),
            pl.BlockSpec((1, COUNT_ROWS, LANES), lambda i: (i, 0, 0)),
        ],
        out_shape=[
            jax.ShapeDtypeStruct((n_tok, d), F32),
            jax.ShapeDtypeStruct((n_tok, d // 2), U32),
            jax.ShapeDtypeStruct((n_tok, LANES), F32),
            jax.ShapeDtypeStruct((n_tiles, COUNT_ROWS, LANES), F32),
        ],
        scratch_shapes=[pltpu.VMEM((tm, d), BF16)],
        compiler_params=_cparams("parallel"),
        name="out_projection_router",
    )(m, *h_args, mod, g2, w_out, w_router, b_router)


MOE_TM = 256
CAST_ROWS = 64
ROW_TM = 512


def _dispatch(route, counts, n_tok, tm_route):
    n_tiles = (2 * n_tok) // MOE_TM + N_EXPERTS
    counts = counts[:, 0, :N_EXPERTS].astype(I32)
    tile_base = jnp.cumsum(counts, axis=0) - counts
    total = jnp.sum(counts, axis=0)
    padded = ((total + MOE_TM - 1) // MOE_TM) * MOE_TM
    seg_end = jnp.cumsum(padded)
    base = (seg_end - padded)[None, :] + tile_base
    r3 = route.reshape(n_tok // tm_route, tm_route, LANES)
    experts = jnp.arange(N_EXPERTS, dtype=I32)
    dest = []
    for k in range(2):
        hit = r3[:, :, ROUTE_E + k].astype(I32)[:, :, None] == experts
        picked = jnp.sum(jnp.where(hit, base[:, None, :], 0), axis=-1)
        dest.append((picked + r3[:, :, ROUTE_RANK + k].astype(I32)).reshape(n_tok // ROW_TM, 1, ROW_TM))
    tile_start = jnp.arange(n_tiles, dtype=I32) * MOE_TM
    n_used = seg_end[-1] // MOE_TM
    tile_valid = (tile_start < seg_end[-1]).astype(I32)
    tile_index = jnp.minimum(jnp.arange(n_tiles, dtype=I32), n_used - 1)
    tile_expert = jnp.sum((seg_end[None, :] <= (tile_index * MOE_TM)[:, None]).astype(I32), axis=1)
    prev = jnp.concatenate([jnp.full((1,), -1, I32), tile_expert[:-1]])
    tile_first = (tile_expert != prev).astype(I32)
    later_first = jnp.logical_and(tile_first == 1, tile_valid == 1)
    pos = jnp.arange(n_tiles, dtype=I32)
    first_pos = jnp.where(later_first, pos, n_tiles)
    next_pos = jnp.min(jnp.where(pos[None, :] > pos[:, None], first_pos[None, :], n_tiles), axis=1)
    tile_next = jnp.where(next_pos < n_tiles, tile_expert[jnp.minimum(next_pos, n_tiles - 1)], -1).astype(I32)
    tails = jnp.maximum(seg_end - MOE_TM, 0).astype(I32)
    return dest, (tile_expert, tile_first, tile_valid, tile_index, tile_next), tails


def _row_scatter_kernel(tail_ref, d0_ref, d1_ref, v_ref, x_hbm, zbuf, sem_z, sem):
    @pl.when(pl.program_id(0) == 0)
    def _():
        zbuf[...] = jnp.zeros_like(zbuf)
        def zero_tile(start):
            start = pl.multiple_of(start, MOE_TM)
            return pltpu.make_async_copy(zbuf, x_hbm.at[pl.ds(start, MOE_TM), :], sem_z)
        for e in range(N_EXPERTS):
            zero_tile(tail_ref[e]).start()
        for e in range(N_EXPERTS):
            zero_tile(tail_ref[e]).wait()
        first_unused = tail_ref[N_EXPERTS - 1] // MOE_TM + 1
        n_tiles = x_hbm.shape[0] // MOE_TM

        def start_unused(t, carry):
            zero_tile(t * MOE_TM).start()
            return carry

        def wait_unused(t, carry):
            zero_tile(t * MOE_TM).wait()
            return carry
        lax.fori_loop(first_unused, n_tiles, start_unused, 0)
        lax.fori_loop(first_unused, n_tiles, wait_unused, 0)

    def row_copies(g, k, d0, d1):
        src = v_ref.at[pl.ds(pl.multiple_of(g * 8, 8) + k, 1), :]
        return (pltpu.make_async_copy(src, x_hbm.at[pl.ds(d0, 1), :], sem),
                pltpu.make_async_copy(src, x_hbm.at[pl.ds(d1, 1), :], sem))

    def issue(g, carry):
        for k in range(8):
            for copy in row_copies(g, k, d0_ref[0, 0, g * 8 + k], d1_ref[0, 0, g * 8 + k]):
                copy.start()
        return carry
    lax.fori_loop(0, ROW_TM // 8, issue, 0)

    def drain(g, carry):
        for k in range(8):
            for copy in row_copies(g, k, 0, 0):
                copy.wait()
        return carry
    lax.fori_loop(0, ROW_TM // 8, drain, 0)


def _row_scatter(vp, dest, tails, n_rows):
    n_tok, half = vp.shape
    n_steps = n_tok // ROW_TM
    d0, d1 = dest
    grid_spec = pltpu.PrefetchScalarGridSpec(
        num_scalar_prefetch=1,
        grid=(n_steps,),
        in_specs=[
            pl.BlockSpec((1, 1, ROW_TM), lambda i, t: (i, 0, 0), memory_space=pltpu.SMEM),
            pl.BlockSpec((1, 1, ROW_TM), lambda i, t: (i, 0, 0), memory_space=pltpu.SMEM),
            pl.BlockSpec((ROW_TM, half), lambda i, t: (i, 0)),
        ],
        out_specs=pl.BlockSpec(memory_space=pl.ANY),
        scratch_shapes=[
            pltpu.VMEM((MOE_TM, half), U32),
            pltpu.SemaphoreType.DMA(()),
            pltpu.SemaphoreType.DMA(()),
        ],
    )
    return pl.pallas_call(
        _row_scatter_kernel,
        grid_spec=grid_spec,
        out_shape=jax.ShapeDtypeStruct((n_rows, half), U32),
        compiler_params=_cparams("arbitrary"),
        name="moe_row_scatter",
    )(tails, d0, d1, vp)


def _moe_kernel(te_ref, tf_ref, tv_ref, ti_ref, tn_ref, x_ref, w1_hbm, w3_hbm, w2_hbm, y_ref,
                s1, s3, s2, w1b, w3b, w2b, sems, *, layer):
    i = pl.program_id(0)
    valid = tv_ref[i] == 1

    def weight_copies(e):
        return (pltpu.make_async_copy(w1_hbm.at[layer, e], s1, sems.at[0]),
                pltpu.make_async_copy(w3_hbm.at[layer, e], s3, sems.at[1]),
                pltpu.make_async_copy(w2_hbm.at[layer, e], s2, sems.at[2]))

    @pl.when(i == 0)
    def _():
        for copy in weight_copies(te_ref[0]):
            copy.start()

    @pl.when(valid)
    def _():
        @pl.when(tf_ref[i] == 1)
        def _():
            for copy in weight_copies(te_ref[i]):
                copy.wait()
            for src, dst in ((s1, w1b), (s3, w3b), (s2, w2b)):
                rows_per = CAST_ROWS * 512 // src.shape[1]

                def cast_rows(c, carry, src=src, dst=dst, rows_per=rows_per):
                    rows = pl.ds(pl.multiple_of(c * rows_per, rows_per), rows_per)
                    dst[rows, :] = src[rows, :].astype(BF16)
                    return carry
                lax.fori_loop(0, src.shape[0] // rows_per, cast_rows, 0)

            @pl.when(tn_ref[i] >= 0)
            def _():
                for copy in weight_copies(tn_ref[i]):
                    copy.start()

        half = x_ref.shape[1]
        xp = x_ref[...]
        lo = lax.bitcast_convert_type(xp << 16, F32).astype(BF16)
        hi = lax.bitcast_convert_type(xp & jnp.uint32(0xFFFF0000), F32).astype(BF16)
        h1 = _dot(lo, w1b[0:half, :]) + _dot(hi, w1b[half:, :])
        h3 = _dot(lo, w3b[0:half, :]) + _dot(hi, w3b[half:, :])
        hh = (h1 * jax.nn.sigmoid(h1) * h3).astype(BF16)
        y_ref[...] = _dot(hh, w2b[...])

    @pl.when(jnp.logical_not(valid))
    def _():
        y_ref[...] = jnp.zeros_like(y_ref)


def _moe_experts(layer, x_sorted, tiles, w1, w3, w2):
    tile_expert, tile_first, tile_valid, tile_index, tile_next = tiles
    n_tiles = tile_expert.shape[0]
    half = x_sorted.shape[1]
    d = 2 * half
    grid_spec = pltpu.PrefetchScalarGridSpec(
        num_scalar_prefetch=5,
        grid=(n_tiles,),
        in_specs=[
            pl.BlockSpec((MOE_TM, half), lambda i, te, tf, tv, ti, tn: (ti[i], 0)),
            pl.BlockSpec(memory_space=pl.ANY),
            pl.BlockSpec(memory_space=pl.ANY),
            pl.BlockSpec(memory_space=pl.ANY),
        ],
        out_specs=pl.BlockSpec((MOE_TM, d), lambda i, te, tf, tv, ti, tn: (i, 0)),
        scratch_shapes=[
            pltpu.VMEM((d, D_EXPERT), F32),
            pltpu.VMEM((d, D_EXPERT), F32),
            pltpu.VMEM((D_EXPERT, d), F32),
            pltpu.VMEM((d, D_EXPERT), BF16),
            pltpu.VMEM((d, D_EXPERT), BF16),
            pltpu.VMEM((D_EXPERT, d), BF16),
            pltpu.SemaphoreType.DMA((3,)),
        ],
    )
    return pl.pallas_call(
        functools.partial(_moe_kernel, layer=layer),
        grid_spec=grid_spec,
        out_shape=jax.ShapeDtypeStruct((n_tiles * MOE_TM, d), F32),
        compiler_params=_cparams("arbitrary"),
        name="moe_experts",
    )(tile_expert, tile_first, tile_valid, tile_index, tile_next, x_sorted, w1, w3, w2)


def _combine_kernel(d0_ref, d1_ref, n0_ref, n1_ref, h_ref, route_ref, mod_ref, *refs, with_next):
    if with_next:
        modn_ref, gn_ref, y_hbm, o_ref, u_ref, ybuf, sems = refs
    else:
        y_hbm, o_ref, ybuf, sems = refs
    i = pl.program_id(0)
    n_steps = pl.num_programs(0)

    def row_copies(slot, g, k, d0, d1):
        rows = pl.ds(pl.multiple_of(g * 8, 8) + k, 1)
        return (pltpu.make_async_copy(y_hbm.at[pl.ds(d0, 1), :], ybuf.at[slot, 0, rows, :], sems.at[slot]),
                pltpu.make_async_copy(y_hbm.at[pl.ds(d1, 1), :], ybuf.at[slot, 1, rows, :], sems.at[slot]))

    def request(slot, i0_ref, i1_ref):
        def body(g, carry):
            for k in range(8):
                for copy in row_copies(slot, g, k, i0_ref[0, 0, g * 8 + k], i1_ref[0, 0, g * 8 + k]):
                    copy.start()
            return carry
        lax.fori_loop(0, ROW_TM // 8, body, 0)

    def await_rows(slot):
        def body(g, carry):
            for k in range(8):
                for copy in row_copies(slot, g, k, 0, 0):
                    copy.wait()
            return carry
        lax.fori_loop(0, ROW_TM // 8, body, 0)

    slot = i % 2

    @pl.when(i == 0)
    def _():
        request(0, d0_ref, d1_ref)

    for s in range(2):
        @pl.when(jnp.logical_and(i + 1 < n_steps, slot == s))
        def _(s=s):
            request(1 - s, n0_ref, n1_ref)

    for s in range(2):
        @pl.when(slot == s)
        def _(s=s):
            await_rows(s)
    p0 = route_ref[:, ROUTE_P:ROUTE_P + 1]
    p1 = route_ref[:, ROUTE_P + 1:ROUTE_P + 2]
    o_ref[...] = h_ref[...] + mod_ref[0, 5:6, :] * (p0 * ybuf[slot, 0] + p1 * ybuf[slot, 1])
    if with_next:
        _norm_modulate_rows((o_ref,), 0, gn_ref, modn_ref, 0, 1, u_ref)


def _combine(h1, route, mod, y_sorted, dest, next_norm, *, n_tok, n_lat, seq, batch):
    d = h1.shape[1]
    tm = ROW_TM
    n_steps = n_tok // tm
    d0, d1 = dest
    with_next = next_norm is not None

    def mod_idx(i):
        return (jnp.where(i * tm < n_lat, (i * tm) // seq, batch), 0, 0)

    row_spec = pl.BlockSpec((tm, d), lambda i: (i, 0))
    mod_spec = pl.BlockSpec((1, N_MOD, d), mod_idx)
    next_specs = [mod_spec, pl.BlockSpec((1, d), lambda i: (0, 0))] if with_next else []
    next_args = list(next_norm) if with_next else []
    out = pl.pallas_call(
        functools.partial(_combine_kernel, with_next=with_next),
        grid=(n_steps,),
        in_specs=[
            pl.BlockSpec((1, 1, tm), lambda i: (i, 0, 0), memory_space=pltpu.SMEM),
            pl.BlockSpec((1, 1, tm), lambda i: (i, 0, 0), memory_space=pltpu.SMEM),
            pl.BlockSpec((1, 1, tm), lambda i: (jnp.minimum(i + 1, n_steps - 1), 0, 0), memory_space=pltpu.SMEM),
            pl.BlockSpec((1, 1, tm), lambda i: (jnp.minimum(i + 1, n_steps - 1), 0, 0), memory_space=pltpu.SMEM),
            row_spec,
            pl.BlockSpec((tm, LANES), lambda i: (i, 0)),
            mod_spec,
        ] + next_specs + [pl.BlockSpec(memory_space=pl.ANY)],
        out_specs=[row_spec, row_spec] if with_next else row_spec,
        out_shape=([jax.ShapeDtypeStruct((n_tok, d), F32), jax.ShapeDtypeStruct((n_tok, d), BF16)] if with_next
                   else jax.ShapeDtypeStruct((n_tok, d), F32)),
        scratch_shapes=[pltpu.VMEM((2, 2, tm, d), F32), pltpu.SemaphoreType.DMA((2,))],
        compiler_params=_cparams("arbitrary"),
        name="moe_combine",
    )(d0, d1, d0, d1, h1, route, mod, *next_args, y_sorted)
    return out if with_next else (out, None)


def _rope_tables(seq, tm):
    n_rows = seq // GRID_W
    row = jnp.repeat(jnp.arange(n_rows), GRID_W).astype(F32)
    col = jnp.tile(jnp.arange(GRID_W), n_rows).astype(F32)
    half = HEAD_DIM // 2
    inv = 1.0 / (ROPE_THETA ** (jnp.arange(0, half, 2, dtype=F32) / half))
    ang_r = row[:, None] * inv
    ang_c = col[:, None] * inv
    ang = jnp.concatenate([ang_r, ang_c, ang_r, ang_c], axis=-1)
    cos, sin = jnp.cos(ang), jnp.sin(ang)
    sin_signed = jnp.where(jnp.arange(HEAD_DIM) < half, -sin, sin)
    ident = jnp.zeros((tm, HEAD_DIM), F32)
    return jnp.concatenate([cos, ident + 1.0]), jnp.concatenate([sin_signed, ident])


def _permute_heads(a):
    lead = a.shape[:-1]
    quarter = HEAD_DIM // 4
    assert HEAD_PERM[quarter] == 2 * quarter and HEAD_PERM[2 * quarter] == quarter
    q = a.reshape(*lead, a.shape[-1] // HEAD_DIM, 4, quarter)
    q = jnp.stack([q[..., 0, :], q[..., 2, :], q[..., 1, :], q[..., 3, :]], axis=-2)
    return q.reshape(*lead, a.shape[-1])


def _score_bound(gq, gk):
    return (1.02 * HEAD_DIM * jnp.max(jnp.abs(gq)) * jnp.max(jnp.abs(gk))).reshape(1).astype(F32)


def kernel(x, c, ctx, c_ctx, w_ada, b_ada, norm1_g, norm2_g, w_in, b_gate, q_norm_a, k_norm_a, diff_lambda,
           sub_norm_a, q_norm_c, k_norm_c, conv_w, conv_b, lru_w_a, lru_b_a, lru_w_x, lru_b_x, lru_lambda,
           w_branch_a, w_branch_b, w_branch_c, w_out, w_group, b_group, w_route, b_route, w1, w3, w2):
    batch, seq, d = x.shape
    ctx_len = ctx.shape[1]
    depth = w_ada.shape[0]
    n_lat = batch * seq
    n_ctx = batch * ctx_len
    n_all = n_lat + n_ctx
    tm = min(1024, seq, n_ctx)
    assert seq % tm == 0 and n_ctx % tm == 0 and n_lat % ctx_len == 0 and batch < MOD_ROWS
    assert seq % GRID_W == 0 and seq % ROW_TM == 0 and n_ctx % ROW_TM == 0
    tq = min(512, seq)
    tm_out = min(512, tm)

    cc = jnp.zeros((MOD_ROWS, d), F32).at[:batch].set(c).at[batch].set(c_ctx)
    mod_all = _ada_modulation(cc, w_ada, b_ada).reshape(depth, MOD_ROWS, N_MOD, d)
    rope = _rope_tables(seq, tm)
    h_src = (x.reshape(n_lat, d), ctx.reshape(n_ctx, d))
    scale = HEAD_DIM ** -0.5

    for l in range(depth):
        last = l == depth - 1
        lambda_init = 0.8 - 0.6 * math.exp(-0.3 * l)
        mod = mod_all[l]
        t_av, t_cq, t_gate = 2048 // PREP_TN, 5120 // PREP_TN, 6656 // PREP_TN
        w_qk = _prepare_weight(w_in, l, QK_COLS, lambda j: jnp.where(j < t_av, j, j + t_cq - t_av),
                               QK_CV // PREP_TN, "prep_w_qk")
        w_vb = _prepare_weight(w_in, l, VB_COLS, lambda j: j + t_av, 0, "prep_w_vb")
        w_gate = _prepare_weight(w_in, l, N_BRANCH * d, lambda j: j + t_gate, 0, "prep_w_gate")
        gq_a, gq_c = q_norm_a[l] * scale, q_norm_c[l] * scale
        gcol = jnp.concatenate([
            _permute_heads(jnp.concatenate([
                jnp.tile(gq_a, 2 * DIFF_HEADS), jnp.tile(k_norm_a[l], 2 * DIFF_HEADS),
                jnp.tile(gq_c, GQA_HEADS), jnp.tile(k_norm_c[l], GQA_KV_HEADS)])),
            jnp.ones((QK_COLS - QK_CV,), F32)]).reshape(1, QK_COLS)
        bound_a = _score_bound(gq_a, k_norm_a[l])
        bound_c = _score_bound(gq_c, k_norm_c[l])

        if l == 0:
            u = _norm_modulate_stream(h_src, mod, norm1_g[l].reshape(1, d), n_all=n_all, tm=tm,
                                      n_lat_tiles=n_lat // tm, tiles_per_seq=seq // tm, batch=batch)
        qk, vb = _in_projection(u, w_qk, w_vb, gcol, rope, tm=tm, n_lat_tiles=n_lat // tm,
                                tiles_per_seq=seq // tm)
        dims = dict(batch=batch, seq=seq, ctx=ctx_len)
        extra_a = (diff_lambda[l], sub_norm_a[l].reshape(1, DIFF_V_DIM))
        ya = _attention("diff", qk, vb, bound_a, extra_a, tq=tq, latent=True, lambda_init=lambda_init, **dims)
        yc = _attention("gqa", qk, vb, bound_c, None, tq=tq, latent=True, **dims)
        yb, yb_c = _rglru(vb, conv_w[l], conv_b[l].reshape(1, LRU_WIDTH), 0.5 * lru_w_a[l], 0.5 * lru_b_a[l],
                          0.5 * lru_w_x[l], 0.5 * lru_b_x[l], lru_lambda[l], need_ctx=not last, **dims)
        if last:
            n_tok, y_ctx = n_lat, None
        else:
            n_tok = n_all
            ya_c = _attention("diff", qk, vb, bound_a, extra_a, tq=tq, latent=False, lambda_init=lambda_init,
                              **dims)
            yc_c = _attention("gqa", qk, vb, bound_c, None, tq=tq, latent=False, **dims)
            y_ctx = (ya_c, yb_c, yc_c)
        w_branch = (w_branch_a[l].astype(BF16), w_branch_b[l].astype(BF16), w_branch_c[l].astype(BF16))
        m = _merge(u, (ya, yb, yc), y_ctx, w_gate, b_gate[l], w_branch, n_tok=n_tok, n_lat=n_lat, tm=tm)
        w_router = jnp.zeros((d, LANES), F32).at[:, :N_GROUPS].set(w_group[l])
        w_router = w_router.at[:, N_GROUPS:N_GROUPS + N_EXPERTS].set(w_route[l]).astype(BF16)
        b_router = jnp.zeros((1, LANES), F32).at[0, :N_GROUPS].set(b_group[l])
        b_router = b_router.at[0, N_GROUPS:N_GROUPS + N_EXPERTS].set(b_route[l])
        h1, vp, route, counts = _out_projection(m, h_src, mod, norm2_g[l].reshape(1, d),
                                                w_out[l].astype(BF16), w_router, b_router, n_tok=n_tok,
                                                tm=tm_out, n_lat=n_lat, seq=seq, batch=batch)
        dest, tiles, tails = _dispatch(route, counts, n_tok, tm_out)
        x_sorted = _row_scatter(vp, dest, tails, tiles[0].shape[0] * MOE_TM)
        y_sorted = _moe_experts(l, x_sorted, tiles, w1, w3, w2)
        next_norm = None if last else (mod_all[l + 1], norm1_g[l + 1].reshape(1, d))
        h_src, u = _combine(h1, route, mod, y_sorted, dest, next_norm, n_tok=n_tok, n_lat=n_lat, seq=seq,
                            batch=batch)
    return h_src.reshape(batch, seq, d)
```

```python
import functools
import math

import jax
import jax.numpy as jnp
from jax import lax
from jax.experimental import pallas as pl
from jax.experimental.pallas import tpu as pltpu

F32 = jnp.float32
BF16 = jnp.bfloat16
I32 = jnp.int32

EPS = 1e-6
HEAD_DIM = 128
GRID_W = 64
ROPE_THETA = 10000.0
DIFF_HEADS = 4
DIFF_V_DIM = 2 * HEAD_DIM
LRU_WIDTH = 1024
LRU_BLOCK_W = 128
CONV_W = 4
LRU_C = 8.0
GQA_HEADS = 8
GQA_KV_HEADS = 2
N_BRANCH = 3
BRANCH_W = 1024
N_GROUPS = 4
EXPERTS_PER_GROUP = 8
N_EXPERTS = N_GROUPS * EXPERTS_PER_GROUP
D_EXPERT = 512
N_MOD = 6
MOD_ROWS = 16
LANES = 128

QK_AQ, QK_AK, QK_CQ, QK_CK, QK_CV = 0, 1024, 2048, 3072, 3328
VB_AV, VB_BX, VB_BY = 0, 1024, 2048
QK_COLS = 3584
VB_COLS = 3072
QK_TN = QK_COLS // 2
VB_TN = VB_COLS // 2
PROJ_CW = 256
PROJ_RC = 512
HEAD_PERM = tuple(range(0, 32)) + tuple(range(64, 96)) + tuple(range(32, 64)) + tuple(range(96, 128))

VMEM_LIMIT = 56 * 1024 * 1024


def _cparams(*sem):
    return pltpu.CompilerParams(dimension_semantics=sem, vmem_limit_bytes=VMEM_LIMIT)


def _dot(a, b):
    return jnp.dot(a, b, preferred_element_type=F32)


def _dot_nt(a, b):
    return lax.dot_general(a, b, (((1,), (1,)), ((), ())), preferred_element_type=F32)


def _lane_mean(x2):
    return _dot(x2.astype(BF16), jnp.full((LANES, LANES), 1.0 / LANES, BF16))


def _ada_kernel(c_ref, w_ref, b_ref, o_ref):
    c = c_ref[...]
    s = (c * jax.nn.sigmoid(c)).astype(BF16)
    o_ref[0] = _dot(s, w_ref[0].astype(BF16)) + b_ref[0]


def _ada_modulation(cc, w_ada, b_ada):
    depth, d, n = w_ada.shape
    tn = 1024
    return pl.pallas_call(
        _ada_kernel,
        grid=(depth, n // tn),
        in_specs=[
            pl.BlockSpec((MOD_ROWS, d), lambda l, j: (0, 0)),
            pl.BlockSpec((1, d, tn), lambda l, j: (l, 0, j)),
            pl.BlockSpec((1, 1, tn), lambda l, j: (l, 0, j)),
        ],
        out_specs=pl.BlockSpec((1, MOD_ROWS, tn), lambda l, j: (l, 0, j)),
        out_shape=jax.ShapeDtypeStruct((depth, MOD_ROWS, n), F32),
        compiler_params=_cparams("parallel", "parallel"),
        name="ada_modulation",
    )(cc, w_ada, b_ada.reshape(depth, 1, n))


PREP_TN = 256


def _wprep_kernel(w_ref, o_ref, *, n_perm_tiles):
    def plain():
        o_ref[...] = w_ref[0].astype(BF16)

    def permuted():
        quarter = lax.broadcasted_iota(I32, (1, HEAD_DIM), 1) // (HEAD_DIM // 4)
        for c in range(PREP_TN // HEAD_DIM):
            sl = slice(c * HEAD_DIM, (c + 1) * HEAD_DIM)
            x = w_ref[0, :, sl]
            y = jnp.where(quarter == 1, pltpu.roll(x, 96, 1), jnp.where(quarter == 2, pltpu.roll(x, 32, 1), x))
            o_ref[:, sl] = y.astype(BF16)

    if n_perm_tiles == 0:
        plain()
        return
    is_perm = pl.program_id(0) < n_perm_tiles
    pl.when(is_perm)(permuted)
    pl.when(jnp.logical_not(is_perm))(plain)


def _prepare_weight(w_in, layer, n_cols, col_tile_of, n_perm_tiles, name):
    d = w_in.shape[1]
    return pl.pallas_call(
        functools.partial(_wprep_kernel, n_perm_tiles=n_perm_tiles),
        grid=(n_cols // PREP_TN,),
        in_specs=[pl.BlockSpec((1, d, PREP_TN), lambda j: (layer, 0, col_tile_of(j)))],
        out_specs=pl.BlockSpec((d, PREP_TN), lambda j: (0, j)),
        out_shape=jax.ShapeDtypeStruct((d, n_cols), BF16),
        compiler_params=_cparams("parallel"),
        name=name,
    )(w_in)


NORM_ROWS = 32
NORM_UNROLL = 4


def _stream_specs(h_src, tm, n_lat, n_tiles):
    nl = n_lat // tm
    if isinstance(h_src, tuple) and n_tiles > nl:
        d = h_src[0].shape[1]
        return list(h_src), [pl.BlockSpec((tm, d), lambda i: (jnp.minimum(i, nl - 1), 0)),
                             pl.BlockSpec((tm, d), lambda i: (jnp.maximum(i - nl, 0), 0))]
    arr = h_src[0] if isinstance(h_src, tuple) else h_src
    return [arr], [pl.BlockSpec((tm, arr.shape[1]), lambda i: (i, 0))]


def _read_stream(h_refs, rows, n_lat_tiles):
    if len(h_refs) == 1:
        return h_refs[0][rows, :]
    return jnp.where(pl.program_id(0) < n_lat_tiles, h_refs[0][rows, :], h_refs[1][rows, :])


def _norm_modulate_rows(h_refs, n_lat_tiles, g_ref, mod_ref, shift_row, scale_row, out_ref):
    g = g_ref[...]
    sc = 1.0 + mod_ref[0, scale_row:scale_row + 1, :]
    sh = mod_ref[0, shift_row:shift_row + 1, :]

    def body(i, carry):
        rows = pl.ds(pl.multiple_of(i * NORM_ROWS, NORM_ROWS), NORM_ROWS)
        x = _read_stream(h_refs, rows, n_lat_tiles)
        ms = jnp.mean(x * x, axis=-1, keepdims=True)
        y = (x * lax.rsqrt(ms + EPS) * g) * sc + sh
        out_ref[rows, :] = y.astype(out_ref.dtype)
        return carry
    lax.fori_loop(0, out_ref.shape[0] // NORM_ROWS, body, 0, unroll=NORM_UNROLL)


def _norm_kernel(*refs, n_lat_tiles):
    mod_ref, g_ref, u_ref = refs[-3:]
    _norm_modulate_rows(refs[:-3], n_lat_tiles, g_ref, mod_ref, 0, 1, u_ref)


def _project_tile(u_ref, w_ref, epilogue):
    tm, tn = u_ref.shape[0], w_ref.shape[1]
    rc = min(PROJ_RC, tm)
    pending = []
    for c in range(tn // PROJ_CW):
        cols = slice(c * PROJ_CW, (c + 1) * PROJ_CW)
        accs = []
        for r in range(tm // rc):
            rows = slice(r * rc, (r + 1) * rc)
            accs.append((rows, cols, _dot(u_ref[rows, :], w_ref[:, cols])))
        for item in pending:
            epilogue(*item)
        pending = accs
    for item in pending:
        epilogue(*item)


def _inproj_qk_kernel(u_ref, w_ref, gc_ref, cos_ref, sin_ref, qk_ref):
    last_tile = pl.program_id(1) == pl.num_programs(1) - 1
    tn = w_ref.shape[1]

    def epilogue(rows, cols, acc):
        plain = jnp.logical_and(last_tile, cols.stop == tn)
        for c in range(PROJ_CW // HEAD_DIM):
            sl = slice(cols.start + c * HEAD_DIM, cols.start + (c + 1) * HEAD_DIM)
            x = acc[:, c * HEAD_DIM:(c + 1) * HEAD_DIM]
            y = x * lax.rsqrt(_lane_mean(x * x) + EPS) * gc_ref[:, sl]
            y = y * cos_ref[rows, :] + pltpu.roll(y, HEAD_DIM // 2, 1) * sin_ref[rows, :]
            qk_ref[rows, sl] = jnp.where(plain, x, y).astype(BF16)

    _project_tile(u_ref, w_ref, epilogue)


def _inproj_vb_kernel(u_ref, w_ref, vb_ref):
    def epilogue(rows, cols, acc):
        vb_ref[rows, cols] = acc.astype(BF16)

    _project_tile(u_ref, w_ref, epilogue)


def _norm_modulate_stream(h_src, mod, g1, *, n_all, tm, n_lat_tiles, tiles_per_seq, batch):
    d = g1.shape[1]

    def mod_idx(i):
        return (jnp.where(i < n_lat_tiles, i // tiles_per_seq, batch), 0, 0)

    h_args, h_specs = _stream_specs(h_src, tm, n_lat_tiles * tm, n_all // tm)
    return pl.pallas_call(
        functools.partial(_norm_kernel, n_lat_tiles=n_lat_tiles),
        grid=(n_all // tm,),
        in_specs=h_specs + [
            pl.BlockSpec((1, N_MOD, d), mod_idx),
            pl.BlockSpec((1, d), lambda i: (0, 0)),
        ],
        out_specs=pl.BlockSpec((tm, d), lambda i: (i, 0)),
        out_shape=jax.ShapeDtypeStruct((n_all, d), BF16),
        compiler_params=_cparams("parallel"),
        name="norm_modulate",
    )(*h_args, mod, g1)


def _in_projection(u, w_qk, w_vb, gcol, rope, *, tm, n_lat_tiles, tiles_per_seq):
    n_all, d = u.shape
    cos, sin = rope

    def rope_idx(i, j):
        return (jnp.where(i < n_lat_tiles, i % tiles_per_seq, tiles_per_seq), 0)

    qk = pl.pallas_call(
        _inproj_qk_kernel,
        grid=(n_all // tm, QK_COLS // QK_TN),
        in_specs=[
            pl.BlockSpec((tm, d), lambda i, j: (i, 0)),
            pl.BlockSpec((d, QK_TN), lambda i, j: (0, j)),
            pl.BlockSpec((1, QK_TN), lambda i, j: (0, j)),
            pl.BlockSpec((tm, HEAD_DIM), rope_idx),
            pl.BlockSpec((tm, HEAD_DIM), rope_idx),
        ],
        out_specs=pl.BlockSpec((tm, QK_TN), lambda i, j: (i, j)),
        out_shape=jax.ShapeDtypeStruct((n_all, QK_COLS), BF16),
        compiler_params=_cparams("parallel", "arbitrary"),
        name="in_projection_qk",
    )(u, w_qk, gcol, cos, sin)
    vb = pl.pallas_call(
        _inproj_vb_kernel,
        grid=(n_all // tm, VB_COLS // VB_TN),
        in_specs=[
            pl.BlockSpec((tm, d), lambda i, j: (i, 0)),
            pl.BlockSpec((d, VB_TN), lambda i, j: (0, j)),
        ],
        out_specs=pl.BlockSpec((tm, VB_TN), lambda i, j: (i, j)),
        out_shape=jax.ShapeDtypeStruct((n_all, VB_COLS), BF16),
        compiler_params=_cparams("parallel", "arbitrary"),
        name="in_projection_vb",
    )(u, w_vb)
    return qk, vb


ATT_KC = 256
SAFE_SHIFT = 40.0
GQA_STACK = 1


def _key_chunks(k_refs, v_refs):
    chunks = []
    for k_ref, v_ref in zip(k_refs, v_refs):
        n = k_ref.shape[0]
        for s in range(0, n, ATT_KC):
            chunks.append((k_ref, v_ref, s, min(ATT_KC, n - s)))
    return chunks


def _fill_shifts(bound_ref, m_scr, q_ref, chunks, heads):
    bound = bound_ref[0]

    @pl.when(bound <= SAFE_SHIFT)
    def _():
        m_scr[...] = jnp.full(m_scr.shape, bound, F32)

    @pl.when(bound > SAFE_SHIFT)
    def _():
        for idx, (qcol, kcol) in enumerate(heads):
            q = q_ref[:, qcol:qcol + HEAD_DIM]
            m = None
            for k_ref, _, s, n in chunks:
                part = jnp.max(_dot_nt(q, k_ref[s:s + n, kcol:kcol + HEAD_DIM]), axis=-1, keepdims=True)
                m = part if m is None else jnp.maximum(m, part)
            m_scr[idx] = jnp.broadcast_to(m, m_scr.shape[1:])


def _stream_softmax_pv(streams, chunks, vcols):
    o = None
    ls = [None] * len(streams)
    for k_ref, v_ref, s, n in chunks:
        es = []
        for idx, (q, shift, kcol) in enumerate(streams):
            sc = _dot_nt(q, k_ref[s:s + n, kcol:kcol + HEAD_DIM])
            tiles = [jnp.exp(sc[:, t * LANES:(t + 1) * LANES] - shift) for t in range(n // LANES)]
            for e_t in tiles:
                ls[idx] = e_t if ls[idx] is None else ls[idx] + e_t
            es.append(tiles[0] if len(tiles) == 1 else jnp.concatenate(tiles, axis=1))
        e = es[0] if len(es) == 1 else jnp.concatenate(es, axis=0)
        pv = _dot(e.astype(BF16), v_ref[s:s + n, vcols])
        o = pv if o is None else o + pv
    rows = streams[0][0].shape[0]
    return [(o[i * rows:(i + 1) * rows, :], jnp.sum(ls[i], axis=-1, keepdims=True)) for i in range(len(streams))]


def _diff_attn_kernel(bound_ref, dl_ref, sg_ref, q_ref, *refs, lambda_init, n_pieces):
    k_refs = refs[0:2 * n_pieces:2]
    v_refs = refs[1:2 * n_pieces:2]
    o_ref = refs[2 * n_pieces]
    m_scr = refs[2 * n_pieces + 1]
    chunks = _key_chunks(k_refs, v_refs)
    heads = [(s * HEAD_DIM, s * HEAD_DIM) for s in range(2 * DIFF_HEADS)]
    _fill_shifts(bound_ref, m_scr, q_ref, chunks, heads)
    dl = dl_ref[...]
    lam = (jnp.exp(jnp.sum(dl[0:1] * dl[1:2], axis=-1, keepdims=True))
           - jnp.exp(jnp.sum(dl[2:3] * dl[3:4], axis=-1, keepdims=True)) + lambda_init)
    for h in range(DIFF_HEADS):
        c1 = 2 * h * HEAD_DIM
        c2 = c1 + HEAD_DIM
        vs = slice(h * DIFF_V_DIM, (h + 1) * DIFF_V_DIM)
        (o1, l1), = _stream_softmax_pv([(q_ref[:, c1:c1 + HEAD_DIM], m_scr[2 * h], c1)], chunks, vs)
        (o2, l2), = _stream_softmax_pv([(q_ref[:, c2:c2 + HEAD_DIM], m_scr[2 * h + 1], c2)], chunks, vs)
        o = o1 * (1.0 / l1) - o2 * (lam / l2)
        ms = jnp.mean(o * o, axis=-1, keepdims=True)
        o = o * lax.rsqrt(ms + EPS) * sg_ref[...] * (1.0 - lambda_init)
        o_ref[:, vs] = o.astype(BF16)


def _gqa_kernel(bound_ref, q_ref, *refs, n_pieces):
    k_refs = refs[0:2 * n_pieces:2]
    v_refs = refs[1:2 * n_pieces:2]
    o_ref = refs[2 * n_pieces]
    m_scr = refs[2 * n_pieces + 1]
    chunks = _key_chunks(k_refs, v_refs)
    group = GQA_HEADS // GQA_KV_HEADS
    heads = [(h * HEAD_DIM, (h // group) * HEAD_DIM) for h in range(GQA_HEADS)]
    _fill_shifts(bound_ref, m_scr, q_ref, chunks, heads)
    tq = q_ref.shape[0]
    for h0 in range(0, GQA_HEADS, GQA_STACK):
        hs = range(h0, h0 + GQA_STACK)
        kcol = heads[h0][1]
        q = jnp.concatenate([q_ref[:, heads[h][0]:heads[h][0] + HEAD_DIM] for h in hs], axis=0)
        shift = jnp.concatenate([m_scr[h] for h in hs], axis=0)
        (o, l), = _stream_softmax_pv([(q, shift, kcol)], chunks, slice(kcol, kcol + HEAD_DIM))
        o = (o * (1.0 / l)).astype(BF16)
        for idx, h in enumerate(hs):
            o_ref[:, heads[h][0]:heads[h][0] + HEAD_DIM] = o[idx * tq:(idx + 1) * tq, :]


def _attention(kind, qk, vb, bound, extra, *, batch, seq, ctx, tq, latent, lambda_init=None):
    n_lat = batch * seq
    ctx_blk0 = n_lat // ctx
    if kind == "diff":
        qcol, kcol, vcol, kvw, v_arr = QK_AQ // 1024, QK_AK // 1024, VB_AV // 1024, 1024, vb
    else:
        qcol, kcol, vcol, kvw, v_arr = QK_CQ // 1024, QK_CK // 256, QK_CV // 256, 256, qk
    if latent:
        q_tiles = seq // tq
        q_spec = pl.BlockSpec((tq, 1024), lambda b, i: (b * q_tiles + i, qcol))
        o_spec = pl.BlockSpec((tq, 1024), lambda b, i: (b * q_tiles + i, 0))
        n_q = n_lat
    else:
        q_tiles = 1
        tq = ctx
        q_spec = pl.BlockSpec((ctx, 1024), lambda b, i: (ctx_blk0 + b, qcol))
        o_spec = pl.BlockSpec((ctx, 1024), lambda b, i: (b, 0))
        n_q = batch * ctx
    kv_specs = [
        pl.BlockSpec((ctx, kvw), lambda b, i: (ctx_blk0 + b, kcol)),
        pl.BlockSpec((ctx, kvw), lambda b, i: (ctx_blk0 + b, vcol)),
    ]
    kv_args = [qk, v_arr]
    if latent:
        kv_specs += [
            pl.BlockSpec((seq, kvw), lambda b, i: (b, kcol)),
            pl.BlockSpec((seq, kvw), lambda b, i: (b, vcol)),
        ]
        kv_args += [qk, v_arr]
    n_pieces = len(kv_args) // 2
    pre_specs = [pl.BlockSpec(memory_space=pltpu.SMEM)]
    pre_args = [bound]
    if kind == "diff":
        diff_lambda, sub_g = extra
        body = functools.partial(_diff_attn_kernel, lambda_init=lambda_init, n_pieces=n_pieces)
        pre_specs += [
            pl.BlockSpec((4, HEAD_DIM), lambda b, i: (0, 0)),
            pl.BlockSpec((1, DIFF_V_DIM), lambda b, i: (0, 0)),
        ]
        pre_args += [diff_lambda, sub_g]
        n_softmax = 2 * DIFF_HEADS
    else:
        body = functools.partial(_gqa_kernel, n_pieces=n_pieces)
        n_softmax = GQA_HEADS
    return pl.pallas_call(
        body,
        grid=(batch, q_tiles),
        in_specs=pre_specs + [q_spec] + kv_specs,
        out_specs=o_spec,
        out_shape=jax.ShapeDtypeStruct((n_q, 1024), BF16),
        scratch_shapes=[pltpu.VMEM((n_softmax, tq, LANES), F32)],
        compiler_params=_cparams("parallel", "arbitrary"),
        name=f"{kind}_attn_{'lat' if latent else 'ctx'}",
    )(*pre_args, qk, *kv_args)


LRU_CW = 512
LRU_PAD = 8
SQRT_GUARD = 1e-30


def _lru_kernel(bxc_ref, byc_ref, bxl_ref, byl_ref, cw_ref, cb_ref, wa_ref, ba_ref, wx_ref, bx_ref, lam_ref,
                *refs, ctx, seq, need_ctx):
    if need_ctx:
        yc_ref, yl_ref = refs[0], refs[1]
        scratch = refs[2:]
    else:
        yc_ref, yl_ref = None, refs[0]
        scratch = refs[1:]
    xpad, a_f, b_f, a_b, b_b = scratch
    a_scr = (a_f, a_b)
    b_scr = (b_f, b_b)
    lam = lam_ref[...]
    neg_sp = -LRU_C * jax.nn.softplus(-lam)

    def gates(x_ref, n):
        zeros = jnp.zeros((LRU_PAD, LRU_CW), F32)
        xpad[0:LRU_PAD, :] = zeros
        xpad[LRU_PAD:LRU_PAD + n, :] = x_ref[...].astype(F32)
        xpad[LRU_PAD + n:2 * LRU_PAD + n, :] = zeros
        xc = cb_ref[...] + cw_ref[0:1, :] * xpad[LRU_PAD - 2:LRU_PAD - 2 + n, :]
        for k in range(1, CONV_W):
            xc = xc + cw_ref[k:k + 1, :] * xpad[LRU_PAD - 2 + k:LRU_PAD - 2 + k + n, :]
        xcb = xc.astype(BF16)
        for d in range(2):
            for blk in range(LRU_CW // LRU_BLOCK_W):
                sl = slice(blk * LRU_BLOCK_W, (blk + 1) * LRU_BLOCK_W)
                xs = xcb[:, sl]
                r = 0.5 * jnp.tanh(_dot(xs, wa_ref[d, blk].astype(BF16)) + ba_ref[d:d + 1, sl]) + 0.5
                gi = 0.5 * jnp.tanh(_dot(xs, wx_ref[d, blk].astype(BF16)) + bx_ref[d:d + 1, sl]) + 0.5
                log_a = r * neg_sp[d:d + 1, sl]
                a = jnp.exp(log_a)
                one_m_a2 = 1.0 - a * a
                mult = one_m_a2 * lax.rsqrt(jnp.maximum(one_m_a2, SQRT_GUARD))
                a_scr[d][0:n, sl] = a
                b_scr[d][0:n, sl] = mult * gi * xc[:, sl]

    def scan(n, hf0, hb0):
        def body(t, carry):
            hf, hb = carry
            tb = n - 1 - t
            hf = a_f[pl.ds(t, 1), :] * hf + b_f[pl.ds(t, 1), :]
            b_f[pl.ds(t, 1), :] = hf
            hb = a_b[pl.ds(tb, 1), :] * hb + b_b[pl.ds(tb, 1), :]
            b_b[pl.ds(tb, 1), :] = hb
            return hf, hb
        return lax.fori_loop(0, n, body, (hf0, hb0), unroll=8)

    zero = jnp.zeros((1, LRU_CW), F32)
    gates(bxc_ref, ctx)
    hf, hb = scan(ctx, zero, zero)
    if need_ctx:
        yc_ref[...] = ((b_f[0:ctx, :] + b_b[0:ctx, :])
                       * jax.nn.gelu(byc_ref[...].astype(F32), approximate=True)).astype(BF16)
    gates(bxl_ref, seq)
    scan(seq, hf, hb)
    yl_ref[...] = ((b_f[0:seq, :] + b_b[0:seq, :])
                   * jax.nn.gelu(byl_ref[...].astype(F32), approximate=True)).astype(BF16)


def _rglru(vb, conv_w, conv_b, w_a, b_a, w_x, b_x, lam, *, batch, seq, ctx, need_ctx):
    n_lat = batch * seq
    ctx_blk0 = n_lat // ctx
    cbx, cby = VB_BX // LRU_CW, VB_BY // LRU_CW
    nblk = LRU_CW // LRU_BLOCK_W
    tmax = max(seq, ctx)
    out_specs = [pl.BlockSpec((seq, LRU_CW), lambda b, c: (b, c))]
    out_shape = [jax.ShapeDtypeStruct((n_lat, LRU_WIDTH), BF16)]
    if need_ctx:
        out_specs = [pl.BlockSpec((ctx, LRU_CW), lambda b, c: (b, c))] + out_specs
        out_shape = [jax.ShapeDtypeStruct((batch * ctx, LRU_WIDTH), BF16)] + out_shape
    outs = pl.pallas_call(
        functools.partial(_lru_kernel, ctx=ctx, seq=seq, need_ctx=need_ctx),
        grid=(batch, LRU_WIDTH // LRU_CW),
        in_specs=[
            pl.BlockSpec((ctx, LRU_CW), lambda b, c: (ctx_blk0 + b, cbx + c)),
            pl.BlockSpec((ctx, LRU_CW), lambda b, c: (ctx_blk0 + b, cby + c)),
            pl.BlockSpec((seq, LRU_CW), lambda b, c: (b, cbx + c)),
            pl.BlockSpec((seq, LRU_CW), lambda b, c: (b, cby + c)),
            pl.BlockSpec((CONV_W, LRU_CW), lambda b, c: (0, c)),
            pl.BlockSpec((1, LRU_CW), lambda b, c: (0, c)),
            pl.BlockSpec((2, nblk, LRU_BLOCK_W, LRU_BLOCK_W), lambda b, c: (0, c, 0, 0)),
            pl.BlockSpec((2, LRU_CW), lambda b, c: (0, c)),
            pl.BlockSpec((2, nblk, LRU_BLOCK_W, LRU_BLOCK_W), lambda b, c: (0, c, 0, 0)),
            pl.BlockSpec((2, LRU_CW), lambda b, c: (0, c)),
            pl.BlockSpec((2, LRU_CW), lambda b, c: (0, c)),
        ],
        out_specs=out_specs,
        out_shape=out_shape,
        scratch_shapes=[
            pltpu.VMEM((tmax + 2 * LRU_PAD, LRU_CW), F32),
            pltpu.VMEM((tmax, LRU_CW), F32),
            pltpu.VMEM((tmax, LRU_CW), F32),
            pltpu.VMEM((tmax, LRU_CW), F32),
            pltpu.VMEM((tmax, LRU_CW), F32),
        ],
        compiler_params=_cparams("parallel", "arbitrary"),
        name="rglru",
    )(vb, vb, vb, vb, conv_w, conv_b, w_a, b_a, w_x, b_x, lam)
    if need_ctx:
        return outs[1], outs[0]
    return outs[0], None


MERGE_TN = 512


def _merge_kernel(u_ref, *refs, n_lat_tiles, has_ctx):
    n_y = 6 if has_ctx else 3
    y_refs = refs[:n_y]
    wg_refs = refs[n_y:n_y + 3]
    bg_ref = refs[n_y + 3]
    wbr_refs = refs[n_y + 4:n_y + 7]
    m_ref = refs[n_y + 7]

    def body(ys):
        u = u_ref[...]
        m = None
        for k in range(N_BRANCH):
            g = jax.nn.sigmoid(_dot(u, wg_refs[k][...]) + bg_ref[k:k + 1, :])
            part = g * _dot(ys[k][...], wbr_refs[k][...])
            m = part if m is None else m + part
        m_ref[...] = m.astype(BF16)

    if not has_ctx:
        body(y_refs)
        return
    is_lat = pl.program_id(0) < n_lat_tiles

    @pl.when(is_lat)
    def _():
        body(y_refs[0:3])

    @pl.when(jnp.logical_not(is_lat))
    def _():
        body(y_refs[3:6])


def _merge(u, y_lat, y_ctx, w_gate, b_gate, w_branch, *, n_tok, n_lat, tm):
    d = u.shape[1]
    nj = d // MERGE_TN
    nl = n_lat // tm
    has_ctx = y_ctx is not None
    y_specs = [pl.BlockSpec((tm, BRANCH_W), lambda i, j: (jnp.minimum(i, nl - 1), 0))] * 3
    y_args = list(y_lat)
    if has_ctx:
        y_specs += [pl.BlockSpec((tm, BRANCH_W), lambda i, j: (jnp.maximum(i - nl, 0), 0),
                                 pipeline_mode=pl.Buffered(1))] * 3
        y_args += list(y_ctx)
    wbr_spec = pl.BlockSpec((BRANCH_W, MERGE_TN), lambda i, j: (0, j))
    return pl.pallas_call(
        functools.partial(_merge_kernel, n_lat_tiles=nl, has_ctx=has_ctx),
        grid=(n_tok // tm, nj),
        in_specs=[pl.BlockSpec((tm, d), lambda i, j: (i, 0))] + y_specs + [
            pl.BlockSpec((d, MERGE_TN), lambda i, j: (0, j)),
            pl.BlockSpec((d, MERGE_TN), lambda i, j: (0, nj + j)),
            pl.BlockSpec((d, MERGE_TN), lambda i, j: (0, 2 * nj + j)),
            pl.BlockSpec((N_BRANCH, MERGE_TN), lambda i, j: (0, j)),
            wbr_spec, wbr_spec, wbr_spec,
        ],
        out_specs=pl.BlockSpec((tm, MERGE_TN), lambda i, j: (i, j)),
        out_shape=jax.ShapeDtypeStruct((n_tok, d), BF16),
        compiler_params=_cparams("parallel", "arbitrary"),
        name="branch_merge",
    )(u, *y_args, w_gate, w_gate, w_gate, b_gate, *w_branch)


OUT_RC = 256
ROUTE_E, ROUTE_P, ROUTE_RANK = 0, 2, 4
COUNT_ROWS = 8


def _outproj_kernel(m_ref, *refs, n_lat_tiles):
    mod_ref, g_ref, wo_ref, wr_ref, br_ref, h1_ref, vp_ref, route_ref, cnt_ref, v_scr = refs[-10:]
    h_refs = refs[:-10]
    tm, d = h1_ref.shape
    rc = min(OUT_RC, tm)
    for r in range(tm // rc):
        rows = slice(r * rc, (r + 1) * rc)
        h1 = (_read_stream(h_refs, rows, n_lat_tiles)
              + mod_ref[0, 2:3, :] * _dot(m_ref[rows, :], wo_ref[...]))
        h1_ref[rows, :] = h1
        ms = jnp.mean(h1 * h1, axis=-1, keepdims=True)
        y = h1 * lax.rsqrt(ms + EPS) * g_ref[...]
        v_scr[rows, :] = (y * (1.0 + mod_ref[0, 4:5, :]) + mod_ref[0, 3:4, :]).astype(BF16)
    v = v_scr[...]
    vp_ref[...] = v.astype(F32)

    logits = _dot(v, wr_ref[...]) + br_ref[...]
    lane = lax.broadcasted_iota(I32, logits.shape, 1).astype(F32)
    neg = jnp.float32(-jnp.inf)
    big = jnp.float32(1e9)

    def masked_argmax(mask):
        val = jnp.max(jnp.where(mask, logits, neg), axis=-1, keepdims=True)
        idx = jnp.min(jnp.where(jnp.logical_and(mask, logits == val), lane, big), axis=-1, keepdims=True)
        return val, idx

    gmask = lane < N_GROUPS
    gmax, gidx = masked_argmax(gmask)
    gsum = jnp.sum(jnp.where(gmask, jnp.exp(logits - gmax), 0.0), axis=-1, keepdims=True)
    g_w = 1.0 / gsum
    lo_lane = N_GROUPS + EXPERTS_PER_GROUP * gidx
    emask = jnp.logical_and(lane >= lo_lane, lane < lo_lane + EXPERTS_PER_GROUP)
    v1, i1 = masked_argmax(emask)
    v2, i2 = masked_argmax(jnp.logical_and(emask, lane != i1))
    e21 = jnp.exp(v2 - v1)
    p1 = g_w / (1.0 + e21)
    p2 = g_w * e21 / (1.0 + e21)
    e1 = i1 - N_GROUPS
    e2 = i2 - N_GROUPS
    oh1 = jnp.where(lane == e1, 1.0, 0.0)
    oh2 = jnp.where(lane == e2, 1.0, 0.0)
    cnt = oh1 + oh2
    row = lax.broadcasted_iota(I32, (tm, tm), 0)
    col = lax.broadcasted_iota(I32, (tm, tm), 1)
    before = jnp.where(row > col, 1.0, 0.0).astype(BF16)
    prefix = _dot(before, cnt.astype(BF16))
    rank1 = jnp.sum(prefix * oh1, axis=-1, keepdims=True)
    rank2 = jnp.sum(prefix * oh2, axis=-1, keepdims=True)
    route = jnp.where(lane == 0, e1, jnp.where(lane == 1, e2, jnp.where(lane == 2, p1, jnp.where(
        lane == 3, p2, jnp.where(lane == 4, rank1, jnp.where(lane == 5, rank2, 0.0))))))
    route_ref[...] = route
    cnt_ref[0] = jnp.broadcast_to(jnp.sum(cnt, axis=0, keepdims=True), (COUNT_ROWS, LANES))


def _out_projection(m, h_src, mod, g2, w_out, w_router, b_router, *, n_tok, tm, n_lat, seq, batch):
    d = m.shape[1]
    n_tiles = n_tok // tm

    def mod_idx(i):
        return (jnp.where(i * tm < n_lat, (i * tm) // seq, batch), 0, 0)

    h_args, h_specs = _stream_specs(h_src, tm, n_lat, n_tiles)
    return pl.pallas_call(
        functools.partial(_outproj_kernel, n_lat_tiles=n_lat // tm),
        grid=(n_tiles,),
        in_specs=[pl.BlockSpec((tm, d), lambda i: (i, 0))] + h_specs + [
            pl.BlockSpec((1, N_MOD, d), mod_idx),
            pl.BlockSpec((1, d), lambda i: (0, 0)),
            pl.BlockSpec((d, d), lambda i: (0, 0), pipeline_mode=pl.Buffered(1)),
            pl.BlockSpec((d, LANES), lambda i: (0, 0)),
            pl.BlockSpec((1, LANES), lambda i: (0, 0)),
        ],
        out_specs=[
            pl.BlockSpec((tm, d), lambda i: (i, 0)),
            pl.BlockSpec((tm, d), lambda i: (i, 0)),
            pl.BlockSpec((tm, LANES), lambda i: (i, 0)),
            pl.BlockSpec((1, COUNT_ROWS, LANES), lambda i: (i, 0, 0)),
        ],
        out_shape=[
            jax.ShapeDtypeStruct((n_tok, d), F32),
            jax.ShapeDtypeStruct((n_tok, d), F32),
            jax.ShapeDtypeStruct((n_tok, LANES), F32),
            jax.ShapeDtypeStruct((n_tiles, COUNT_ROWS, LANES), F32),
        ],
        scratch_shapes=[pltpu.VMEM((tm, d), BF16)],
        compiler_params=_cparams("parallel"),
        name="out_projection_router",
    )(m, *h_args, mod, g2, w_out, w_router, b_router)


MOE_TM = 256
CAST_ROWS = 64
ROW_TM = 512


def _dispatch(route, counts, n_tok, tm_route):
    n_tiles = (2 * n_tok) // MOE_TM + N_EXPERTS
    counts = counts[:, 0, :N_EXPERTS].astype(I32)
    tile_base = jnp.cumsum(counts, axis=0) - counts
    total = jnp.sum(counts, axis=0)
    padded = ((total + MOE_TM - 1) // MOE_TM) * MOE_TM
    seg_end = jnp.cumsum(padded)
    base = (seg_end - padded)[None, :] + tile_base
    r3 = route.reshape(n_tok // tm_route, tm_route, LANES)
    experts = jnp.arange(N_EXPERTS, dtype=I32)
    dest = []
    for k in range(2):
        hit = r3[:, :, ROUTE_E + k].astype(I32)[:, :, None] == experts
        picked = jnp.sum(jnp.where(hit, base[:, None, :], 0), axis=-1)
        dest.append((picked + r3[:, :, ROUTE_RANK + k].astype(I32)).reshape(n_tok // ROW_TM, 1, ROW_TM))
    tile_start = jnp.arange(n_tiles, dtype=I32) * MOE_TM
    n_used = seg_end[-1] // MOE_TM
    tile_valid = (tile_start < seg_end[-1]).astype(I32)
    tile_index = jnp.minimum(jnp.arange(n_tiles, dtype=I32), n_used - 1)
    tile_expert = jnp.sum((seg_end[None, :] <= (tile_index * MOE_TM)[:, None]).astype(I32), axis=1)
    prev = jnp.concatenate([jnp.full((1,), -1, I32), tile_expert[:-1]])
    tile_first = (tile_expert != prev).astype(I32)
    later_first = jnp.logical_and(tile_first == 1, tile_valid == 1)
    pos = jnp.arange(n_tiles, dtype=I32)
    first_pos = jnp.where(later_first, pos, n_tiles)
    next_pos = jnp.min(jnp.where(pos[None, :] > pos[:, None], first_pos[None, :], n_tiles), axis=1)
    tile_next = jnp.where(next_pos < n_tiles, tile_expert[jnp.minimum(next_pos, n_tiles - 1)], -1).astype(I32)
    tails = jnp.maximum(seg_end - MOE_TM, 0).astype(I32)
    return dest, (tile_expert, tile_first, tile_valid, tile_index, tile_next), tails


def _row_scatter_kernel(tail_ref, d0_ref, d1_ref, v_ref, x_hbm, zbuf, sem_z, sem):
    @pl.when(pl.program_id(0) == 0)
    def _():
        zbuf[...] = jnp.zeros_like(zbuf)
        def zero_tile(start):
            start = pl.multiple_of(start, MOE_TM)
            return pltpu.make_async_copy(zbuf, x_hbm.at[pl.ds(start, MOE_TM), :], sem_z)
        for e in range(N_EXPERTS):
            zero_tile(tail_ref[e]).start()
        for e in range(N_EXPERTS):
            zero_tile(tail_ref[e]).wait()
        first_unused = tail_ref[N_EXPERTS - 1] // MOE_TM + 1
        n_tiles = x_hbm.shape[0] // MOE_TM

        def start_unused(t, carry):
            zero_tile(t * MOE_TM).start()
            return carry

        def wait_unused(t, carry):
            zero_tile(t * MOE_TM).wait()
            return carry
        lax.fori_loop(first_unused, n_tiles, start_unused, 0)
        lax.fori_loop(first_unused, n_tiles, wait_unused, 0)

    def row_copies(g, k, d0, d1):
        src = v_ref.at[pl.ds(pl.multiple_of(g * 8, 8) + k, 1), :]
        return (pltpu.make_async_copy(src, x_hbm.at[pl.ds(d0, 1), :], sem),
                pltpu.make_async_copy(src, x_hbm.at[pl.ds(d1, 1), :], sem))

    def issue(g, carry):
        for k in range(8):
            for copy in row_copies(g, k, d0_ref[0, 0, g * 8 + k], d1_ref[0, 0, g * 8 + k]):
                copy.start()
        return carry
    lax.fori_loop(0, ROW_TM // 8, issue, 0)

    def drain(g, carry):
        for k in range(8):
            for copy in row_copies(g, k, 0, 0):
                copy.wait()
        return carry
    lax.fori_loop(0, ROW_TM // 8, drain, 0)


def _row_scatter(vp, dest, tails, n_rows):
    n_tok, width = vp.shape
    n_steps = n_tok // ROW_TM
    d0, d1 = dest
    grid_spec = pltpu.PrefetchScalarGridSpec(
        num_scalar_prefetch=1,
        grid=(n_steps,),
        in_specs=[
            pl.BlockSpec((1, 1, ROW_TM), lambda i, t: (i, 0, 0), memory_space=pltpu.SMEM),
            pl.BlockSpec((1, 1, ROW_TM), lambda i, t: (i, 0, 0), memory_space=pltpu.SMEM),
            pl.BlockSpec((ROW_TM, width), lambda i, t: (i, 0)),
        ],
        out_specs=pl.BlockSpec(memory_space=pl.ANY),
        scratch_shapes=[
            pltpu.VMEM((MOE_TM, width), vp.dtype),
            pltpu.SemaphoreType.DMA(()),
            pltpu.SemaphoreType.DMA(()),
        ],
    )
    return pl.pallas_call(
        _row_scatter_kernel,
        grid_spec=grid_spec,
        out_shape=jax.ShapeDtypeStruct((n_rows, width), vp.dtype),
        compiler_params=_cparams("arbitrary"),
        name="moe_row_scatter",
    )(tails, d0, d1, vp)


def _moe_kernel(te_ref, tf_ref, tv_ref, ti_ref, tn_ref, x_ref, w1_hbm, w3_hbm, w2_hbm, y_ref,
                s1, s3, s2, w1b, w3b, w2b, sems, *, layer):
    i = pl.program_id(0)
    valid = tv_ref[i] == 1

    def weight_copies(e):
        return (pltpu.make_async_copy(w1_hbm.at[layer, e], s1, sems.at[0]),
                pltpu.make_async_copy(w3_hbm.at[layer, e], s3, sems.at[1]),
                pltpu.make_async_copy(w2_hbm.at[layer, e], s2, sems.at[2]))

    @pl.when(i == 0)
    def _():
        for copy in weight_copies(te_ref[0]):
            copy.start()

    @pl.when(valid)
    def _():
        @pl.when(tf_ref[i] == 1)
        def _():
            for copy in weight_copies(te_ref[i]):
                copy.wait()
            for src, dst in ((s1, w1b), (s3, w3b), (s2, w2b)):
                rows_per = CAST_ROWS * 512 // src.shape[1]

                def cast_rows(c, carry, src=src, dst=dst, rows_per=rows_per):
                    rows = pl.ds(pl.multiple_of(c * rows_per, rows_per), rows_per)
                    dst[rows, :] = src[rows, :].astype(BF16)
                    return carry
                lax.fori_loop(0, src.shape[0] // rows_per, cast_rows, 0)

            @pl.when(tn_ref[i] >= 0)
            def _():
                for copy in weight_copies(tn_ref[i]):
                    copy.start()

        x = x_ref[...].astype(BF16)
        h1 = _dot(x, w1b[...])
        h3 = _dot(x, w3b[...])
        hh = (h1 * jax.nn.sigmoid(h1) * h3).astype(BF16)
        y_ref[...] = _dot(hh, w2b[...])

    @pl.when(jnp.logical_not(valid))
    def _():
        y_ref[...] = jnp.zeros_like(y_ref)


def _moe_experts(layer, x_sorted, tiles, w1, w3, w2):
    tile_expert, tile_first, tile_valid, tile_index, tile_next = tiles
    n_tiles = tile_expert.shape[0]
    d = x_sorted.shape[1]
    grid_spec = pltpu.PrefetchScalarGridSpec(
        num_scalar_prefetch=5,
        grid=(n_tiles,),
        in_specs=[
            pl.BlockSpec((MOE_TM, d), lambda i, te, tf, tv, ti, tn: (ti[i], 0)),
            pl.BlockSpec(memory_space=pl.ANY),
            pl.BlockSpec(memory_space=pl.ANY),
            pl.BlockSpec(memory_space=pl.ANY),
        ],
        out_specs=pl.BlockSpec((MOE_TM, d), lambda i, te, tf, tv, ti, tn: (i, 0)),
        scratch_shapes=[
            pltpu.VMEM((d, D_EXPERT), F32),
            pltpu.VMEM((d, D_EXPERT), F32),
            pltpu.VMEM((D_EXPERT, d), F32),
            pltpu.VMEM((d, D_EXPERT), BF16),
            pltpu.VMEM((d, D_EXPERT), BF16),
            pltpu.VMEM((D_EXPERT, d), BF16),
            pltpu.SemaphoreType.DMA((3,)),
        ],
    )
    return pl.pallas_call(
        functools.partial(_moe_kernel, layer=layer),
        grid_spec=grid_spec,
        out_shape=jax.ShapeDtypeStruct((n_tiles * MOE_TM, d), F32),
        compiler_params=_cparams("arbitrary"),
        name="moe_experts",
    )(tile_expert, tile_first, tile_valid, tile_index, tile_next, x_sorted, w1, w3, w2)


def _combine_kernel(d0_ref, d1_ref, n0_ref, n1_ref, h_ref, route_ref, mod_ref, *refs, with_next):
    if with_next:
        modn_ref, gn_ref, y_hbm, o_ref, u_ref, ybuf, sems = refs
    else:
        y_hbm, o_ref, ybuf, sems = refs
    i = pl.program_id(0)
    n_steps = pl.num_programs(0)

    def row_copies(slot, g, k, d0, d1):
        rows = pl.ds(pl.multiple_of(g * 8, 8) + k, 1)
        return (pltpu.make_async_copy(y_hbm.at[pl.ds(d0, 1), :], ybuf.at[slot, 0, rows, :], sems.at[slot]),
                pltpu.make_async_copy(y_hbm.at[pl.ds(d1, 1), :], ybuf.at[slot, 1, rows, :], sems.at[slot]))

    def request(slot, i0_ref, i1_ref):
        def body(g, carry):
            for k in range(8):
                for copy in row_copies(slot, g, k, i0_ref[0, 0, g * 8 + k], i1_ref[0, 0, g * 8 + k]):
                    copy.start()
            return carry
        lax.fori_loop(0, ROW_TM // 8, body, 0)

    def await_rows(slot):
        def body(g, carry):
            for k in range(8):
                for copy in row_copies(slot, g, k, 0, 0):
                    copy.wait()
            return carry
        lax.fori_loop(0, ROW_TM // 8, body, 0)

    slot = i % 2

    @pl.when(i == 0)
    def _():
        request(0, d0_ref, d1_ref)

    for s in range(2):
        @pl.when(jnp.logical_and(i + 1 < n_steps, slot == s))
        def _(s=s):
            request(1 - s, n0_ref, n1_ref)

    for s in range(2):
        @pl.when(slot == s)
        def _(s=s):
            await_rows(s)
    p0 = route_ref[:, ROUTE_P:ROUTE_P + 1]
    p1 = route_ref[:, ROUTE_P + 1:ROUTE_P + 2]
    o_ref[...] = h_ref[...] + mod_ref[0, 5:6, :] * (p0 * ybuf[slot, 0] + p1 * ybuf[slot, 1])
    if with_next:
        _norm_modulate_rows((o_ref,), 0, gn_ref, modn_ref, 0, 1, u_ref)


def _combine(h1, route, mod, y_sorted, dest, next_norm, *, n_tok, n_lat, seq, batch):
    d = h1.shape[1]
    tm = ROW_TM
    n_steps = n_tok // tm
    d0, d1 = dest
    with_next = next_norm is not None

    def mod_idx(i):
        return (jnp.where(i * tm < n_lat, (i * tm) // seq, batch), 0, 0)

    row_spec = pl.BlockSpec((tm, d), lambda i: (i, 0))
    mod_spec = pl.BlockSpec((1, N_MOD, d), mod_idx)
    next_specs = [mod_spec, pl.BlockSpec((1, d), lambda i: (0, 0))] if with_next else []
    next_args = list(next_norm) if with_next else []
    out = pl.pallas_call(
        functools.partial(_combine_kernel, with_next=with_next),
        grid=(n_steps,),
        in_specs=[
            pl.BlockSpec((1, 1, tm), lambda i: (i, 0, 0), memory_space=pltpu.SMEM),
            pl.BlockSpec((1, 1, tm), lambda i: (i, 0, 0), memory_space=pltpu.SMEM),
            pl.BlockSpec((1, 1, tm), lambda i: (jnp.minimum(i + 1, n_steps - 1), 0, 0), memory_space=pltpu.SMEM),
            pl.BlockSpec((1, 1, tm), lambda i: (jnp.minimum(i + 1, n_steps - 1), 0, 0), memory_space=pltpu.SMEM),
            row_spec,
            pl.BlockSpec((tm, LANES), lambda i: (i, 0)),
            mod_spec,
        ] + next_specs + [pl.BlockSpec(memory_space=pl.ANY)],
        out_specs=[row_spec, row_spec] if with_next else row_spec,
        out_shape=([jax.ShapeDtypeStruct((n_tok, d), F32), jax.ShapeDtypeStruct((n_tok, d), BF16)] if with_next
                   else jax.ShapeDtypeStruct((n_tok, d), F32)),
        scratch_shapes=[pltpu.VMEM((2, 2, tm, d), F32), pltpu.SemaphoreType.DMA((2,))],
        compiler_params=_cparams("arbitrary"),
        name="moe_combine",
    )(d0, d1, d0, d1, h1, route, mod, *next_args, y_sorted)
    return out if with_next else (out, None)


def _rope_tables(seq, tm):
    n_rows = seq // GRID_W
    row = jnp.repeat(jnp.arange(n_rows), GRID_W).astype(F32)
    col = jnp.tile(jnp.arange(GRID_W), n_rows).astype(F32)
    half = HEAD_DIM // 2
    inv = 1.0 / (ROPE_THETA ** (jnp.arange(0, half, 2, dtype=F32) / half))
    ang_r = row[:, None] * inv
    ang_c = col[:, None] * inv
    ang = jnp.concatenate([ang_r, ang_c, ang_r, ang_c], axis=-1)
    cos, sin = jnp.cos(ang), jnp.sin(ang)
    sin_signed = jnp.where(jnp.arange(HEAD_DIM) < half, -sin, sin)
    ident = jnp.zeros((tm, HEAD_DIM), F32)
    return jnp.concatenate([cos, ident + 1.0]), jnp.concatenate([sin_signed, ident])


def _permute_heads(a):
    lead = a.shape[:-1]
    quarter = HEAD_DIM // 4
    assert HEAD_PERM[quarter] == 2 * quarter and HEAD_PERM[2 * quarter] == quarter
    q = a.reshape(*lead, a.shape[-1] // HEAD_DIM, 4, quarter)
    q = jnp.stack([q[..., 0, :], q[..., 2, :], q[..., 1, :], q[..., 3, :]], axis=-2)
    return q.reshape(*lead, a.shape[-1])


def _score_bound(gq, gk):
    return (1.02 * HEAD_DIM * jnp.max(jnp.abs(gq)) * jnp.max(jnp.abs(gk))).reshape(1).astype(F32)


def kernel(x, c, ctx, c_ctx, w_ada, b_ada, norm1_g, norm2_g, w_in, b_gate, q_norm_a, k_norm_a, diff_lambda,
           sub_norm_a, q_norm_c, k_norm_c, conv_w, conv_b, lru_w_a, lru_b_a, lru_w_x, lru_b_x, lru_lambda,
           w_branch_a, w_branch_b, w_branch_c, w_out, w_group, b_group, w_route, b_route, w1, w3, w2):
    batch, seq, d = x.shape
    ctx_len = ctx.shape[1]
    depth = w_ada.shape[0]
    n_lat = batch * seq
    n_ctx = batch * ctx_len
    n_all = n_lat + n_ctx
    tm = min(1024, seq, n_ctx)
    assert seq % tm == 0 and n_ctx % tm == 0 and n_lat % ctx_len == 0 and batch < MOD_ROWS
    assert seq % GRID_W == 0 and seq % ROW_TM == 0 and n_ctx % ROW_TM == 0
    tq = min(512, seq)
    tm_out = min(512, tm)

    cc = jnp.zeros((MOD_ROWS, d), F32).at[:batch].set(c).at[batch].set(c_ctx)
    mod_all = _ada_modulation(cc, w_ada, b_ada).reshape(depth, MOD_ROWS, N_MOD, d)
    rope = _rope_tables(seq, tm)
    h_src = (x.reshape(n_lat, d), ctx.reshape(n_ctx, d))
    scale = HEAD_DIM ** -0.5

    for l in range(depth):
        last = l == depth - 1
        lambda_init = 0.8 - 0.6 * math.exp(-0.3 * l)
        mod = mod_all[l]
        t_av, t_cq, t_gate = 2048 // PREP_TN, 5120 // PREP_TN, 6656 // PREP_TN
        w_qk = _prepare_weight(w_in, l, QK_COLS, lambda j: jnp.where(j < t_av, j, j + t_cq - t_av),
                               QK_CV // PREP_TN, "prep_w_qk")
        w_vb = _prepare_weight(w_in, l, VB_COLS, lambda j: j + t_av, 0, "prep_w_vb")
        w_gate = _prepare_weight(w_in, l, N_BRANCH * d, lambda j: j + t_gate, 0, "prep_w_gate")
        gq_a, gq_c = q_norm_a[l] * scale, q_norm_c[l] * scale
        gcol = jnp.concatenate([
            _permute_heads(jnp.concatenate([
                jnp.tile(gq_a, 2 * DIFF_HEADS), jnp.tile(k_norm_a[l], 2 * DIFF_HEADS),
                jnp.tile(gq_c, GQA_HEADS), jnp.tile(k_norm_c[l], GQA_KV_HEADS)])),
            jnp.ones((QK_COLS - QK_CV,), F32)]).reshape(1, QK_COLS)
        bound_a = _score_bound(gq_a, k_norm_a[l])
        bound_c = _score_bound(gq_c, k_norm_c[l])

        if l == 0:
            u = _norm_modulate_stream(h_src, mod, norm1_g[l].reshape(1, d), n_all=n_all, tm=tm,
                                      n_lat_tiles=n_lat // tm, tiles_per_seq=seq // tm, batch=batch)
        qk, vb = _in_projection(u, w_qk, w_vb, gcol, rope, tm=tm, n_lat_tiles=n_lat // tm,
                                tiles_per_seq=seq // tm)
        dims = dict(batch=batch, seq=seq, ctx=ctx_len)
        extra_a = (diff_lambda[l], sub_norm_a[l].reshape(1, DIFF_V_DIM))
        ya = _attention("diff", qk, vb, bound_a, extra_a, tq=tq, latent=True, lambda_init=lambda_init, **dims)
        yc = _attention("gqa", qk, vb, bound_c, None, tq=tq, latent=True, **dims)
        yb, yb_c = _rglru(vb, conv_w[l], conv_b[l].reshape(1, LRU_WIDTH), 0.5 * lru_w_a[l], 0.5 * lru_b_a[l],
                          0.5 * lru_w_x[l], 0.5 * lru_b_x[l], lru_lambda[l], need_ctx=not last, **dims)
        if last:
            n_tok, y_ctx = n_lat, None
        else:
            n_tok = n_all
            ya_c = _attention("diff", qk, vb, bound_a, extra_a, tq=tq, latent=False, lambda_init=lambda_init,
                              **dims)
            yc_c = _attention("gqa", qk, vb, bound_c, None, tq=tq, latent=False, **dims)
            y_ctx = (ya_c, yb_c, yc_c)
        w_branch = (w_branch_a[l].astype(BF16), w_branch_b[l].astype(BF16), w_branch_c[l].astype(BF16))
        m = _merge(u, (ya, yb, yc), y_ctx, w_gate, b_gate[l], w_branch, n_tok=n_tok, n_lat=n_lat, tm=tm)
        w_router = jnp.zeros((d, LANES), F32).at[:, :N_GROUPS].set(w_group[l])
        w_router = w_router.at[:, N_GROUPS:N_GROUPS + N_EXPERTS].set(w_route[l]).astype(BF16)
        b_router = jnp.zeros((1, LANES), F32).at[0, :N_GROUPS].set(b_group[l])
        b_router = b_router.at[0, N_GROUPS:N_GROUPS + N_EXPERTS].set(b_route[l])
        h1, vp, route, counts = _out_projection(m, h_src, mod, norm2_g[l].reshape(1, d),
                                                w_out[l].astype(BF16), w_router, b_router, n_tok=n_tok,
                                                tm=tm_out, n_lat=n_lat, seq=seq, batch=batch)
        dest, tiles, tails = _dispatch(route, counts, n_tok, tm_out)
        x_sorted = _row_scatter(vp, dest, tails, tiles[0].shape[0] * MOE_TM)
        y_sorted = _moe_experts(l, x_sorted, tiles, w1, w3, w2)
        next_norm = None if last else (mod_all[l + 1], norm1_g[l + 1].reshape(1, d))
        h_src, u = _combine(h1, route, mod, y_sorted, dest, next_norm, n_tok=n_tok, n_lat=n_lat, seq=seq,
                            batch=batch)
    return h_src.reshape(batch, seq, d)
```

```python
import functools
import math

import jax
import jax.numpy as jnp
from jax import lax
from jax.experimental import pallas as pl
from jax.experimental.pallas import tpu as pltpu

F32 = jnp.float32
BF16 = jnp.bfloat16
I32 = jnp.int32

EPS = 1e-6
HEAD_DIM = 128
GRID_W = 64
ROPE_THETA = 10000.0
DIFF_HEADS = 4
DIFF_V_DIM = 2 * HEAD_DIM
LRU_WIDTH = 1024
LRU_BLOCK_W = 128
CONV_W = 4
LRU_C = 8.0
GQA_HEADS = 8
GQA_KV_HEADS = 2
N_BRANCH = 3
BRANCH_W = 1024
N_GROUPS = 4
EXPERTS_PER_GROUP = 8
N_EXPERTS = N_GROUPS * EXPERTS_PER_GROUP
D_EXPERT = 512
N_MOD = 6
MOD_ROWS = 16
LANES = 128

QK_AQ, QK_AK, QK_CQ, QK_CK, QK_CV = 0, 1024, 2048, 3072, 3328
VB_AV, VB_BX, VB_BY = 0, 1024, 2048
QK_COLS = 3584
VB_COLS = 3072
QK_TN = QK_COLS // 2
VB_TN = VB_COLS // 2
PROJ_CW = 256
PROJ_RC = 512
HEAD_PERM = tuple(range(0, 32)) + tuple(range(64, 96)) + tuple(range(32, 64)) + tuple(range(96, 128))

VMEM_LIMIT = 56 * 1024 * 1024


def _cparams(*sem):
    return pltpu.CompilerParams(dimension_semantics=sem, vmem_limit_bytes=VMEM_LIMIT)


def _dot(a, b):
    return jnp.dot(a, b, preferred_element_type=F32)


def _dot_nt(a, b):
    return lax.dot_general(a, b, (((1,), (1,)), ((), ())), preferred_element_type=F32)


def _lane_mean(x2):
    return _dot(x2.astype(BF16), jnp.full((LANES, LANES), 1.0 / LANES, BF16))


def _ada_kernel(c_ref, w_ref, b_ref, o_ref):
    c = c_ref[...]
    s = (c * jax.nn.sigmoid(c)).astype(BF16)
    o_ref[0] = _dot(s, w_ref[0].astype(BF16)) + b_ref[0]


def _ada_modulation(cc, w_ada, b_ada):
    depth, d, n = w_ada.shape
    tn = 1024
    return pl.pallas_call(
        _ada_kernel,
        grid=(depth, n // tn),
        in_specs=[
            pl.BlockSpec((MOD_ROWS, d), lambda l, j: (0, 0)),
            pl.BlockSpec((1, d, tn), lambda l, j: (l, 0, j)),
            pl.BlockSpec((1, 1, tn), lambda l, j: (l, 0, j)),
        ],
        out_specs=pl.BlockSpec((1, MOD_ROWS, tn), lambda l, j: (l, 0, j)),
        out_shape=jax.ShapeDtypeStruct((depth, MOD_ROWS, n), F32),
        compiler_params=_cparams("parallel", "parallel"),
        name="ada_modulation",
    )(cc, w_ada, b_ada.reshape(depth, 1, n))


PREP_TN = 256


def _wprep_kernel(w_ref, o_ref, *, n_perm_tiles):
    def plain():
        o_ref[...] = w_ref[0].astype(BF16)

    def permuted():
        quarter = lax.broadcasted_iota(I32, (1, HEAD_DIM), 1) // (HEAD_DIM // 4)
        for c in range(PREP_TN // HEAD_DIM):
            sl = slice(c * HEAD_DIM, (c + 1) * HEAD_DIM)
            x = w_ref[0, :, sl]
            y = jnp.where(quarter == 1, pltpu.roll(x, 96, 1), jnp.where(quarter == 2, pltpu.roll(x, 32, 1), x))
            o_ref[:, sl] = y.astype(BF16)

    if n_perm_tiles == 0:
        plain()
        return
    is_perm = pl.program_id(0) < n_perm_tiles
    pl.when(is_perm)(permuted)
    pl.when(jnp.logical_not(is_perm))(plain)


def _prepare_weight(w_in, layer, n_cols, col_tile_of, n_perm_tiles, name):
    d = w_in.shape[1]
    return pl.pallas_call(
        functools.partial(_wprep_kernel, n_perm_tiles=n_perm_tiles),
        grid=(n_cols // PREP_TN,),
        in_specs=[pl.BlockSpec((1, d, PREP_TN), lambda j: (layer, 0, col_tile_of(j)))],
        out_specs=pl.BlockSpec((d, PREP_TN), lambda j: (0, j)),
        out_shape=jax.ShapeDtypeStruct((d, n_cols), BF16),
        compiler_params=_cparams("parallel"),
        name=name,
    )(w_in)


NORM_ROWS = 32
NORM_UNROLL = 4


def _stream_specs(h_src, tm, n_lat, n_tiles):
    nl = n_lat // tm
    if isinstance(h_src, tuple) and n_tiles > nl:
        d = h_src[0].shape[1]
        return list(h_src), [pl.BlockSpec((tm, d), lambda i: (jnp.minimum(i, nl - 1), 0)),
                             pl.BlockSpec((tm, d), lambda i: (jnp.maximum(i - nl, 0), 0))]
    arr = h_src[0] if isinstance(h_src, tuple) else h_src
    return [arr], [pl.BlockSpec((tm, arr.shape[1]), lambda i: (i, 0))]


def _read_stream(h_refs, rows, n_lat_tiles):
    if len(h_refs) == 1:
        return h_refs[0][rows, :]
    return jnp.where(pl.program_id(0) < n_lat_tiles, h_refs[0][rows, :], h_refs[1][rows, :])


def _norm_modulate_rows(h_refs, n_lat_tiles, g_ref, mod_ref, shift_row, scale_row, out_ref):
    g = g_ref[...]
    sc = 1.0 + mod_ref[0, scale_row:scale_row + 1, :]
    sh = mod_ref[0, shift_row:shift_row + 1, :]

    def body(i, carry):
        rows = pl.ds(pl.multiple_of(i * NORM_ROWS, NORM_ROWS), NORM_ROWS)
        x = _read_stream(h_refs, rows, n_lat_tiles)
        ms = jnp.mean(x * x, axis=-1, keepdims=True)
        y = (x * lax.rsqrt(ms + EPS) * g) * sc + sh
        out_ref[rows, :] = y.astype(out_ref.dtype)
        return carry
    lax.fori_loop(0, out_ref.shape[0] // NORM_ROWS, body, 0, unroll=NORM_UNROLL)


def _norm_kernel(*refs, n_lat_tiles):
    mod_ref, g_ref, u_ref = refs[-3:]
    _norm_modulate_rows(refs[:-3], n_lat_tiles, g_ref, mod_ref, 0, 1, u_ref)


def _project_tile(u_ref, w_ref, epilogue):
    tm, tn = u_ref.shape[0], w_ref.shape[1]
    rc = min(PROJ_RC, tm)
    pending = []
    for c in range(tn // PROJ_CW):
        cols = slice(c * PROJ_CW, (c + 1) * PROJ_CW)
        accs = []
        for r in range(tm // rc):
            rows = slice(r * rc, (r + 1) * rc)
            accs.append((rows, cols, _dot(u_ref[rows, :], w_ref[:, cols])))
        for item in pending:
            epilogue(*item)
        pending = accs
    for item in pending:
        epilogue(*item)


def _inproj_qk_kernel(u_ref, w_ref, gc_ref, cos_ref, sin_ref, qk_ref):
    last_tile = pl.program_id(1) == pl.num_programs(1) - 1
    tn = w_ref.shape[1]

    def epilogue(rows, cols, acc):
        plain = jnp.logical_and(last_tile, cols.stop == tn)
        for c in range(PROJ_CW // HEAD_DIM):
            sl = slice(cols.start + c * HEAD_DIM, cols.start + (c + 1) * HEAD_DIM)
            x = acc[:, c * HEAD_DIM:(c + 1) * HEAD_DIM]
            y = x * lax.rsqrt(_lane_mean(x * x) + EPS) * gc_ref[:, sl]
            y = y * cos_ref[rows, :] + pltpu.roll(y, HEAD_DIM // 2, 1) * sin_ref[rows, :]
            qk_ref[rows, sl] = jnp.where(plain, x, y).astype(BF16)

    _project_tile(u_ref, w_ref, epilogue)


def _inproj_vb_kernel(u_ref, w_ref, vb_ref):
    def epilogue(rows, cols, acc):
        vb_ref[rows, cols] = acc.astype(BF16)

    _project_tile(u_ref, w_ref, epilogue)


def _norm_modulate_stream(h_src, mod, g1, *, n_all, tm, n_lat_tiles, tiles_per_seq, batch):
    d = g1.shape[1]

    def mod_idx(i):
        return (jnp.where(i < n_lat_tiles, i // tiles_per_seq, batch), 0, 0)

    h_args, h_specs = _stream_specs(h_src, tm, n_lat_tiles * tm, n_all // tm)
    return pl.pallas_call(
        functools.partial(_norm_kernel, n_lat_tiles=n_lat_tiles),
        grid=(n_all // tm,),
        in_specs=h_specs + [
            pl.BlockSpec((1, N_MOD, d), mod_idx),
            pl.BlockSpec((1, d), lambda i: (0, 0)),
        ],
        out_specs=pl.BlockSpec((tm, d), lambda i: (i, 0)),
        out_shape=jax.ShapeDtypeStruct((n_all, d), BF16),
        compiler_params=_cparams("parallel"),
        name="norm_modulate",
    )(*h_args, mod, g1)


def _in_projection(u, w_qk, w_vb, gcol, rope, *, tm, n_lat_tiles, tiles_per_seq):
    n_all, d = u.shape
    cos, sin = rope

    def rope_idx(i, j):
        return (jnp.where(i < n_lat_tiles, i % tiles_per_seq, tiles_per_seq), 0)

    qk = pl.pallas_call(
        _inproj_qk_kernel,
        grid=(n_all // tm, QK_COLS // QK_TN),
        in_specs=[
            pl.BlockSpec((tm, d), lambda i, j: (i, 0)),
            pl.BlockSpec((d, QK_TN), lambda i, j: (0, j)),
            pl.BlockSpec((1, QK_TN), lambda i, j: (0, j)),
            pl.BlockSpec((tm, HEAD_DIM), rope_idx),
            pl.BlockSpec((tm, HEAD_DIM), rope_idx),
        ],
        out_specs=pl.BlockSpec((tm, QK_TN), lambda i, j: (i, j)),
        out_shape=jax.ShapeDtypeStruct((n_all, QK_COLS), BF16),
        compiler_params=_cparams("parallel", "arbitrary"),
        name="in_projection_qk",
    )(u, w_qk, gcol, cos, sin)
    vb = pl.pallas_call(
        _inproj_vb_kernel,
        grid=(n_all // tm, VB_COLS // VB_TN),
        in_specs=[
            pl.BlockSpec((tm, d), lambda i, j: (i, 0)),
            pl.BlockSpec((d, VB_TN), lambda i, j: (0, j)),
        ],
        out_specs=pl.BlockSpec((tm, VB_TN), lambda i, j: (i, j)),
        out_shape=jax.ShapeDtypeStruct((n_all, VB_COLS), BF16),
        compiler_params=_cparams("parallel", "arbitrary"),
        name="in_projection_vb",
    )(u, w_vb)
    return qk, vb


ATT_KC = 256
SAFE_SHIFT = 40.0
GQA_STACK = 1


def _key_chunks(k_refs, v_refs):
    chunks = []
    for k_ref, v_ref in zip(k_refs, v_refs):
        n = k_ref.shape[0]
        for s in range(0, n, ATT_KC):
            chunks.append((k_ref, v_ref, s, min(ATT_KC, n - s)))
    return chunks


def _fill_shifts(bound_ref, m_scr, q_ref, chunks, heads):
    bound = bound_ref[0]

    @pl.when(bound <= SAFE_SHIFT)
    def _():
        m_scr[...] = jnp.full(m_scr.shape, bound, F32)

    @pl.when(bound > SAFE_SHIFT)
    def _():
        for idx, (qcol, kcol) in enumerate(heads):
            q = q_ref[:, qcol:qcol + HEAD_DIM]
            m = None
            for k_ref, _, s, n in chunks:
                part = jnp.max(_dot_nt(q, k_ref[s:s + n, kcol:kcol + HEAD_DIM]), axis=-1, keepdims=True)
                m = part if m is None else jnp.maximum(m, part)
            m_scr[idx] = jnp.broadcast_to(m, m_scr.shape[1:])


def _stream_softmax_pv(streams, chunks, vcols):
    o = None
    ls = [None] * len(streams)
    for k_ref, v_ref, s, n in chunks:
        es = []
        for idx, (q, shift, kcol) in enumerate(streams):
            sc = _dot_nt(q, k_ref[s:s + n, kcol:kcol + HEAD_DIM])
            tiles = [jnp.exp(sc[:, t * LANES:(t + 1) * LANES] - shift) for t in range(n // LANES)]
            for e_t in tiles:
                ls[idx] = e_t if ls[idx] is None else ls[idx] + e_t
            es.append(tiles[0] if len(tiles) == 1 else jnp.concatenate(tiles, axis=1))
        e = es[0] if len(es) == 1 else jnp.concatenate(es, axis=0)
        pv = _dot(e.astype(BF16), v_ref[s:s + n, vcols])
        o = pv if o is None else o + pv
    rows = streams[0][0].shape[0]
    return [(o[i * rows:(i + 1) * rows, :], jnp.sum(ls[i], axis=-1, keepdims=True)) for i in range(len(streams))]


def _diff_attn_kernel(bound_ref, dl_ref, sg_ref, q_ref, *refs, lambda_init, n_pieces):
    k_refs = refs[0:2 * n_pieces:2]
    v_refs = refs[1:2 * n_pieces:2]
    o_ref = refs[2 * n_pieces]
    m_scr = refs[2 * n_pieces + 1]
    chunks = _key_chunks(k_refs, v_refs)
    heads = [(s * HEAD_DIM, s * HEAD_DIM) for s in range(2 * DIFF_HEADS)]
    _fill_shifts(bound_ref, m_scr, q_ref, chunks, heads)
    dl = dl_ref[...]
    lam = (jnp.exp(jnp.sum(dl[0:1] * dl[1:2], axis=-1, keepdims=True))
           - jnp.exp(jnp.sum(dl[2:3] * dl[3:4], axis=-1, keepdims=True)) + lambda_init)
    for h in range(DIFF_HEADS):
        c1 = 2 * h * HEAD_DIM
        c2 = c1 + HEAD_DIM
        vs = slice(h * DIFF_V_DIM, (h + 1) * DIFF_V_DIM)
        (o1, l1), = _stream_softmax_pv([(q_ref[:, c1:c1 + HEAD_DIM], m_scr[2 * h], c1)], chunks, vs)
        (o2, l2), = _stream_softmax_pv([(q_ref[:, c2:c2 + HEAD_DIM], m_scr[2 * h + 1], c2)], chunks, vs)
        o = o1 * (1.0 / l1) - o2 * (lam / l2)
        ms = jnp.mean(o * o, axis=-1, keepdims=True)
        o = o * lax.rsqrt(ms + EPS) * sg_ref[...] * (1.0 - lambda_init)
        o_ref[:, vs] = o.astype(BF16)


def _gqa_kernel(bound_ref, q_ref, *refs, n_pieces):
    k_refs = refs[0:2 * n_pieces:2]
    v_refs = refs[1:2 * n_pieces:2]
    o_ref = refs[2 * n_pieces]
    m_scr = refs[2 * n_pieces + 1]
    chunks = _key_chunks(k_refs, v_refs)
    group = GQA_HEADS // GQA_KV_HEADS
    heads = [(h * HEAD_DIM, (h // group) * HEAD_DIM) for h in range(GQA_HEADS)]
    _fill_shifts(bound_ref, m_scr, q_ref, chunks, heads)
    tq = q_ref.shape[0]
    for h0 in range(0, GQA_HEADS, GQA_STACK):
        hs = range(h0, h0 + GQA_STACK)
        kcol = heads[h0][1]
        q = jnp.concatenate([q_ref[:, heads[h][0]:heads[h][0] + HEAD_DIM] for h in hs], axis=0)
        shift = jnp.concatenate([m_scr[h] for h in hs], axis=0)
        (o, l), = _stream_softmax_pv([(q, shift, kcol)], chunks, slice(kcol, kcol + HEAD_DIM))
        o = (o * (1.0 / l)).astype(BF16)
        for idx, h in enumerate(hs):
            o_ref[:, heads[h][0]:heads[h][0] + HEAD_DIM] = o[idx * tq:(idx + 1) * tq, :]


def _attention(kind, qk, vb, bound, extra, *, batch, seq, ctx, tq, latent, lambda_init=None):
    n_lat = batch * seq
    ctx_blk0 = n_lat // ctx
    if kind == "diff":
        qcol, kcol, vcol, kvw, v_arr = QK_AQ // 1024, QK_AK // 1024, VB_AV // 1024, 1024, vb
    else:
        qcol, kcol, vcol, kvw, v_arr = QK_CQ // 1024, QK_CK // 256, QK_CV // 256, 256, qk
    if latent:
        q_tiles = seq // tq
        q_spec = pl.BlockSpec((tq, 1024), lambda b, i: (b * q_tiles + i, qcol))
        o_spec = pl.BlockSpec((tq, 1024), lambda b, i: (b * q_tiles + i, 0))
        n_q = n_lat
    else:
        q_tiles = 1
        tq = ctx
        q_spec = pl.BlockSpec((ctx, 1024), lambda b, i: (ctx_blk0 + b, qcol))
        o_spec = pl.BlockSpec((ctx, 1024), lambda b, i: (b, 0))
        n_q = batch * ctx
    kv_specs = [
        pl.BlockSpec((ctx, kvw), lambda b, i: (ctx_blk0 + b, kcol)),
        pl.BlockSpec((ctx, kvw), lambda b, i: (ctx_blk0 + b, vcol)),
    ]
    kv_args = [qk, v_arr]
    if latent:
        kv_specs += [
            pl.BlockSpec((seq, kvw), lambda b, i: (b, kcol)),
            pl.BlockSpec((seq, kvw), lambda b, i: (b, vcol)),
        ]
        kv_args += [qk, v_arr]
    n_pieces = len(kv_args) // 2
    pre_specs = [pl.BlockSpec(memory_space=pltpu.SMEM)]
    pre_args = [bound]
    if kind == "diff":
        diff_lambda, sub_g = extra
        body = functools.partial(_diff_attn_kernel, lambda_init=lambda_init, n_pieces=n_pieces)
        pre_specs += [
            pl.BlockSpec((4, HEAD_DIM), lambda b, i: (0, 0)),
            pl.BlockSpec((1, DIFF_V_DIM), lambda b, i: (0, 0)),
        ]
        pre_args += [diff_lambda, sub_g]
        n_softmax = 2 * DIFF_HEADS
    else:
        body = functools.partial(_gqa_kernel, n_pieces=n_pieces)
        n_softmax = GQA_HEADS
    return pl.pallas_call(
        body,
        grid=(batch, q_tiles),
        in_specs=pre_specs + [q_spec] + kv_specs,
        out_specs=o_spec,
        out_shape=jax.ShapeDtypeStruct((n_q, 1024), BF16),
        scratch_shapes=[pltpu.VMEM((n_softmax, tq, LANES), F32)],
        compiler_params=_cparams("parallel", "arbitrary"),
        name=f"{kind}_attn_{'lat' if latent else 'ctx'}",
    )(*pre_args, qk, *kv_args)


LRU_CW = 512
LRU_PAD = 8
SQRT_GUARD = 1e-30


def _lru_kernel(bxc_ref, byc_ref, bxl_ref, byl_ref, cw_ref, cb_ref, wa_ref, ba_ref, wx_ref, bx_ref, lam_ref,
                *refs, ctx, seq, need_ctx):
    if need_ctx:
        yc_ref, yl_ref = refs[0], refs[1]
        scratch = refs[2:]
    else:
        yc_ref, yl_ref = None, refs[0]
        scratch = refs[1:]
    xpad, a_f, b_f, a_b, b_b = scratch
    a_scr = (a_f, a_b)
    b_scr = (b_f, b_b)
    lam = lam_ref[...]
    neg_sp = -LRU_C * jax.nn.softplus(-lam)

    def gates(x_ref, n):
        zeros = jnp.zeros((LRU_PAD, LRU_CW), F32)
        xpad[0:LRU_PAD, :] = zeros
        xpad[LRU_PAD:LRU_PAD + n, :] = x_ref[...].astype(F32)
        xpad[LRU_PAD + n:2 * LRU_PAD + n, :] = zeros
        xc = cb_ref[...] + cw_ref[0:1, :] * xpad[LRU_PAD - 2:LRU_PAD - 2 + n, :]
        for k in range(1, CONV_W):
            xc = xc + cw_ref[k:k + 1, :] * xpad[LRU_PAD - 2 + k:LRU_PAD - 2 + k + n, :]
        xcb = xc.astype(BF16)
        for d in range(2):
            for blk in range(LRU_CW // LRU_BLOCK_W):
                sl = slice(blk * LRU_BLOCK_W, (blk + 1) * LRU_BLOCK_W)
                xs = xcb[:, sl]
                r = 0.5 * jnp.tanh(_dot(xs, wa_ref[d, blk].astype(BF16)) + ba_ref[d:d + 1, sl]) + 0.5
                gi = 0.5 * jnp.tanh(_dot(xs, wx_ref[d, blk].astype(BF16)) + bx_ref[d:d + 1, sl]) + 0.5
                log_a = r * neg_sp[d:d + 1, sl]
                a = jnp.exp(log_a)
                one_m_a2 = 1.0 - a * a
                mult = one_m_a2 * lax.rsqrt(jnp.maximum(one_m_a2, SQRT_GUARD))
                a_scr[d][0:n, sl] = a
                b_scr[d][0:n, sl] = mult * gi * xc[:, sl]

    def scan(n, hf0, hb0):
        def body(t, carry):
            hf, hb = carry
            tb = n - 1 - t
            hf = a_f[pl.ds(t, 1), :] * hf + b_f[pl.ds(t, 1), :]
            b_f[pl.ds(t, 1), :] = hf
            hb = a_b[pl.ds(tb, 1), :] * hb + b_b[pl.ds(tb, 1), :]
            b_b[pl.ds(tb, 1), :] = hb
            return hf, hb
        return lax.fori_loop(0, n, body, (hf0, hb0), unroll=8)

    zero = jnp.zeros((1, LRU_CW), F32)
    gates(bxc_ref, ctx)
    hf, hb = scan(ctx, zero, zero)
    if need_ctx:
        yc_ref[...] = ((b_f[0:ctx, :] + b_b[0:ctx, :])
                       * jax.nn.gelu(byc_ref[...].astype(F32), approximate=True)).astype(BF16)
    gates(bxl_ref, seq)
    scan(seq, hf, hb)
    yl_ref[...] = ((b_f[0:seq, :] + b_b[0:seq, :])
                   * jax.nn.gelu(byl_ref[...].astype(F32), approximate=True)).astype(BF16)


def _rglru(vb, conv_w, conv_b, w_a, b_a, w_x, b_x, lam, *, batch, seq, ctx, need_ctx):
    n_lat = batch * seq
    ctx_blk0 = n_lat // ctx
    cbx, cby = VB_BX // LRU_CW, VB_BY // LRU_CW
    nblk = LRU_CW // LRU_BLOCK_W
    tmax = max(seq, ctx)
    out_specs = [pl.BlockSpec((seq, LRU_CW), lambda b, c: (b, c))]
    out_shape = [jax.ShapeDtypeStruct((n_lat, LRU_WIDTH), BF16)]
    if need_ctx:
        out_specs = [pl.BlockSpec((ctx, LRU_CW), lambda b, c: (b, c))] + out_specs
        out_shape = [jax.ShapeDtypeStruct((batch * ctx, LRU_WIDTH), BF16)] + out_shape
    outs = pl.pallas_call(
        functools.partial(_lru_kernel, ctx=ctx, seq=seq, need_ctx=need_ctx),
        grid=(batch, LRU_WIDTH // LRU_CW),
        in_specs=[
            pl.BlockSpec((ctx, LRU_CW), lambda b, c: (ctx_blk0 + b, cbx + c)),
            pl.BlockSpec((ctx, LRU_CW), lambda b, c: (ctx_blk0 + b, cby + c)),
            pl.BlockSpec((seq, LRU_CW), lambda b, c: (b, cbx + c)),
            pl.BlockSpec((seq, LRU_CW), lambda b, c: (b, cby + c)),
            pl.BlockSpec((CONV_W, LRU_CW), lambda b, c: (0, c)),
            pl.BlockSpec((1, LRU_CW), lambda b, c: (0, c)),
            pl.BlockSpec((2, nblk, LRU_BLOCK_W, LRU_BLOCK_W), lambda b, c: (0, c, 0, 0)),
            pl.BlockSpec((2, LRU_CW), lambda b, c: (0, c)),
            pl.BlockSpec((2, nblk, LRU_BLOCK_W, LRU_BLOCK_W), lambda b, c: (0, c, 0, 0)),
            pl.BlockSpec((2, LRU_CW), lambda b, c: (0, c)),
            pl.BlockSpec((2, LRU_CW), lambda b, c: (0, c)),
        ],
        out_specs=out_specs,
        out_shape=out_shape,
        scratch_shapes=[
            pltpu.VMEM((tmax + 2 * LRU_PAD, LRU_CW), F32),
            pltpu.VMEM((tmax, LRU_CW), F32),
            pltpu.VMEM((tmax, LRU_CW), F32),
            pltpu.VMEM((tmax, LRU_CW), F32),
            pltpu.VMEM((tmax, LRU_CW), F32),
        ],
        compiler_params=_cparams("parallel", "arbitrary"),
        name="rglru",
    )(vb, vb, vb, vb, conv_w, conv_b, w_a, b_a, w_x, b_x, lam)
    if need_ctx:
        return outs[1], outs[0]
    return outs[0], None


MERGE_TN = 512


def _merge_kernel(u_ref, *refs, n_lat_tiles, has_ctx):
    n_y = 6 if has_ctx else 3
    y_refs = refs[:n_y]
    wg_refs = refs[n_y:n_y + 3]
    bg_ref = refs[n_y + 3]
    wbr_refs = refs[n_y + 4:n_y + 7]
    m_ref = refs[n_y + 7]

    def body(ys):
        u = u_ref[...]
        m = None
        for k in range(N_BRANCH):
            g = jax.nn.sigmoid(_dot(u, wg_refs[k][...]) + bg_ref[k:k + 1, :])
            part = g * _dot(ys[k][...], wbr_refs[k][...])
            m = part if m is None else m + part
        m_ref[...] = m.astype(BF16)

    if not has_ctx:
        body(y_refs)
        return
    is_lat = pl.program_id(0) < n_lat_tiles

    @pl.when(is_lat)
    def _():
        body(y_refs[0:3])

    @pl.when(jnp.logical_not(is_lat))
    def _():
        body(y_refs[3:6])


def _merge(u, y_lat, y_ctx, w_gate, b_gate, w_branch, *, n_tok, n_lat, tm):
    d = u.shape[1]
    nj = d // MERGE_TN
    nl = n_lat // tm
    has_ctx = y_ctx is not None
    y_specs = [pl.BlockSpec((tm, BRANCH_W), lambda i, j: (jnp.minimum(i, nl - 1), 0))] * 3
    y_args = list(y_lat)
    if has_ctx:
        y_specs += [pl.BlockSpec((tm, BRANCH_W), lambda i, j: (jnp.maximum(i - nl, 0), 0),
                                 pipeline_mode=pl.Buffered(1))] * 3
        y_args += list(y_ctx)
    wbr_spec = pl.BlockSpec((BRANCH_W, MERGE_TN), lambda i, j: (0, j))
    return pl.pallas_call(
        functools.partial(_merge_kernel, n_lat_tiles=nl, has_ctx=has_ctx),
        grid=(n_tok // tm, nj),
        in_specs=[pl.BlockSpec((tm, d), lambda i, j: (i, 0))] + y_specs + [
            pl.BlockSpec((d, MERGE_TN), lambda i, j: (0, j)),
            pl.BlockSpec((d, MERGE_TN), lambda i, j: (0, nj + j)),
            pl.BlockSpec((d, MERGE_TN), lambda i, j: (0, 2 * nj + j)),
            pl.BlockSpec((N_BRANCH, MERGE_TN), lambda i, j: (0, j)),
            wbr_spec, wbr_spec, wbr_spec,
        ],
        out_specs=pl.BlockSpec((tm, MERGE_TN), lambda i, j: (i, j)),
        out_shape=jax.ShapeDtypeStruct((n_tok, d), BF16),
        compiler_params=_cparams("parallel", "arbitrary"),
        name="branch_merge",
    )(u, *y_args, w_gate, w_gate, w_gate, b_gate, *w_branch)


OUT_RC = 256
ROUTE_E, ROUTE_P, ROUTE_RANK = 0, 2, 4
COUNT_ROWS = 8


def _outproj_kernel(m_ref, *refs, n_lat_tiles):
    mod_ref, g_ref, wo_ref, wr_ref, br_ref, h1_ref, vp_ref, route_ref, cnt_ref, v_scr = refs[-10:]
    h_refs = refs[:-10]
    tm, d = h1_ref.shape
    rc = min(OUT_RC, tm)
    for r in range(tm // rc):
        rows = slice(r * rc, (r + 1) * rc)
        h1 = (_read_stream(h_refs, rows, n_lat_tiles)
              + mod_ref[0, 2:3, :] * _dot(m_ref[rows, :], wo_ref[...]))
        h1_ref[rows, :] = h1
        ms = jnp.mean(h1 * h1, axis=-1, keepdims=True)
        y = h1 * lax.rsqrt(ms + EPS) * g_ref[...]
        v_scr[rows, :] = (y * (1.0 + mod_ref[0, 4:5, :]) + mod_ref[0, 3:4, :]).astype(BF16)
    v = v_scr[...]
    vp_ref[...] = v.astype(F32)

    logits = _dot(v, wr_ref[...]) + br_ref[...]
    lane = lax.broadcasted_iota(I32, logits.shape, 1).astype(F32)
    neg = jnp.float32(-jnp.inf)
    big = jnp.float32(1e9)

    def masked_argmax(mask):
        val = jnp.max(jnp.where(mask, logits, neg), axis=-1, keepdims=True)
        idx = jnp.min(jnp.where(jnp.logical_and(mask, logits == val), lane, big), axis=-1, keepdims=True)
        return val, idx

    gmask = lane < N_GROUPS
    gmax, gidx = masked_argmax(gmask)
    gsum = jnp.sum(jnp.where(gmask, jnp.exp(logits - gmax), 0.0), axis=-1, keepdims=True)
    g_w = 1.0 / gsum
    lo_lane = N_GROUPS + EXPERTS_PER_GROUP * gidx
    emask = jnp.logical_and(lane >= lo_lane, lane < lo_lane + EXPERTS_PER_GROUP)
    v1, i1 = masked_argmax(emask)
    v2, i2 = masked_argmax(jnp.logical_and(emask, lane != i1))
    e21 = jnp.exp(v2 - v1)
    p1 = g_w / (1.0 + e21)
    p2 = g_w * e21 / (1.0 + e21)
    e1 = i1 - N_GROUPS
    e2 = i2 - N_GROUPS
    oh1 = jnp.where(lane == e1, 1.0, 0.0)
    oh2 = jnp.where(lane == e2, 1.0, 0.0)
    cnt = oh1 + oh2
    row = lax.broadcasted_iota(I32, (tm, tm), 0)
    col = lax.broadcasted_iota(I32, (tm, tm), 1)
    before = jnp.where(row > col, 1.0, 0.0).astype(BF16)
    prefix = _dot(before, cnt.astype(BF16))
    rank1 = jnp.sum(prefix * oh1, axis=-1, keepdims=True)
    rank2 = jnp.sum(prefix * oh2, axis=-1, keepdims=True)
    route = jnp.where(lane == 0, e1, jnp.where(lane == 1, e2, jnp.where(lane == 2, p1, jnp.where(
        lane == 3, p2, jnp.where(lane == 4, rank1, jnp.where(lane == 5, rank2, 0.0))))))
    route_ref[...] = route
    cnt_ref[0] = jnp.broadcast_to(jnp.sum(cnt, axis=0, keepdims=True), (COUNT_ROWS, LANES))


def _out_projection(m, h_src, mod, g2, w_out, w_router, b_router, *, n_tok, tm, n_lat, seq, batch):
    d = m.shape[1]
    n_tiles = n_tok // tm

    def mod_idx(i):
        return (jnp.where(i * tm < n_lat, (i * tm) // seq, batch), 0, 0)

    h_args, h_specs = _stream_specs(h_src, tm, n_lat, n_tiles)
    return pl.pallas_call(
        functools.partial(_outproj_kernel, n_lat_tiles=n_lat // tm),
        grid=(n_tiles,),
        in_specs=[pl.BlockSpec((tm, d), lambda i: (i, 0))] + h_specs + [
            pl.BlockSpec((1, N_MOD, d), mod_idx),
            pl.BlockSpec((1, d), lambda i: (0, 0)),
            pl.BlockSpec((d, d), lambda i: (0, 0), pipeline_mode=pl.Buffered(1)),
            pl.BlockSpec((d, LANES), lambda i: (0, 0)),
            pl.BlockSpec((1, LANES), lambda i: (0, 0)),
        ],
        out_specs=[
            pl.BlockSpec((tm, d), lambda i: (i, 0)),
            pl.BlockSpec((tm, d), lambda i: (i, 0)),
            pl.BlockSpec((tm, LANES), lambda i: (i, 0)),
            pl.BlockSpec((1, COUNT_ROWS, LANES), lambda i: (i, 0, 0)),
        ],
        out_shape=[
            jax.ShapeDtypeStruct((n_tok, d), F32),
            jax.ShapeDtypeStruct((n_tok, d), F32),
            jax.ShapeDtypeStruct((n_tok, LANES), F32),
            jax.ShapeDtypeStruct((n_tiles, COUNT_ROWS, LANES), F32),
        ],
        scratch_shapes=[pltpu.VMEM((tm, d), BF16)],
        compiler_params=_cparams("parallel"),
        name="out_projection_router",
    )(m, *h_args, mod, g2, w_out, w_router, b_router)


MOE_TM = 256
CAST_ROWS = 64
ROW_TM = 512


def _dispatch(route, counts, n_tok, tm_route):
    n_tiles = (2 * n_tok) // MOE_TM + N_EXPERTS
    counts = counts[:, 0, :N_EXPERTS].astype(I32)
    tile_base = jnp.cumsum(counts, axis=0) - counts
    total = jnp.sum(counts, axis=0)
    padded = ((total + MOE_TM - 1) // MOE_TM) * MOE_TM
    seg_end = jnp.cumsum(padded)
    base = (seg_end - padded)[None, :] + tile_base
    r3 = route.reshape(n_tok // tm_route, tm_route, LANES)
    experts = jnp.arange(N_EXPERTS, dtype=I32)
    dest = []
    for k in range(2):
        hit = r3[:, :, ROUTE_E + k].astype(I32)[:, :, None] == experts
        picked = jnp.sum(jnp.where(hit, base[:, None, :], 0), axis=-1)
        dest.append((picked + r3[:, :, ROUTE_RANK + k].astype(I32)).reshape(n_tok // ROW_TM, 1, ROW_TM))
    tile_start = jnp.arange(n_tiles, dtype=I32) * MOE_TM
    n_used = seg_end[-1] // MOE_TM
    tile_valid = (tile_start < seg_end[-1]).astype(I32)
    tile_index = jnp.minimum(jnp.arange(n_tiles, dtype=I32), n_used - 1)
    tile_expert = jnp.sum((seg_end[None, :] <= (tile_index * MOE_TM)[:, None]).astype(I32), axis=1)
    prev = jnp.concatenate([jnp.full((1,), -1, I32), tile_expert[:-1]])
    tile_first = (tile_expert != prev).astype(I32)
    later_first = jnp.logical_and(tile_first == 1, tile_valid == 1)
    pos = jnp.arange(n_tiles, dtype=I32)
    first_pos = jnp.where(later_first, pos, n_tiles)
    next_pos = jnp.min(jnp.where(pos[None, :] > pos[:, None], first_pos[None, :], n_tiles), axis=1)
    tile_next = jnp.where(next_pos < n_tiles, tile_expert[jnp.minimum(next_pos, n_tiles - 1)], -1).astype(I32)
    tails = jnp.maximum(seg_end - MOE_TM, 0).astype(I32)
    return dest, (tile_expert, tile_first, tile_valid, tile_index, tile_next), tails


def _row_scatter_kernel(tail_ref, d0_ref, d1_ref, v_ref, x_hbm, zbuf, sem_z, sem):
    @pl.when(pl.program_id(0) == 0)
    def _():
        zbuf[...] = jnp.zeros_like(zbuf)
        def zero_tile(start):
            start = pl.multiple_of(start, MOE_TM)
            return pltpu.make_async_copy(zbuf, x_hbm.at[pl.ds(start, MOE_TM), :], sem_z)
        for e in range(N_EXPERTS):
            zero_tile(tail_ref[e]).start()
        for e in range(N_EXPERTS):
            zero_tile(tail_ref[e]).wait()
        first_unused = tail_ref[N_EXPERTS - 1] // MOE_TM + 1
        n_tiles = x_hbm.shape[0] // MOE_TM

        def start_unused(t, carry):
            zero_tile(t * MOE_TM).start()
            return carry

        def wait_unused(t, carry):
            zero_tile(t * MOE_TM).wait()
            return carry
        lax.fori_loop(first_unused, n_tiles, start_unused, 0)
        lax.fori_loop(first_unused, n_tiles, wait_unused, 0)

    def row_copies(g, k, d0, d1):
        src = v_ref.at[pl.ds(pl.multiple_of(g * 8, 8) + k, 1), :]
        return (pltpu.make_async_copy(src, x_hbm.at[pl.ds(d0, 1), :], sem),
                pltpu.make_async_copy(src, x_hbm.at[pl.ds(d1, 1), :], sem))

    def issue(g, carry):
        for k in range(8):
            for prio, copy in enumerate(row_copies(g, k, d0_ref[0, 0, g * 8 + k], d1_ref[0, 0, g * 8 + k])):
                copy.start(priority=prio)
        return carry
    lax.fori_loop(0, ROW_TM // 8, issue, 0)

    def drain(g, carry):
        for k in range(8):
            for copy in row_copies(g, k, 0, 0):
                copy.wait()
        return carry
    lax.fori_loop(0, ROW_TM // 8, drain, 0)


def _row_scatter(vp, dest, tails, n_rows):
    n_tok, width = vp.shape
    n_steps = n_tok // ROW_TM
    d0, d1 = dest
    grid_spec = pltpu.PrefetchScalarGridSpec(
        num_scalar_prefetch=1,
        grid=(n_steps,),
        in_specs=[
            pl.BlockSpec((1, 1, ROW_TM), lambda i, t: (i, 0, 0), memory_space=pltpu.SMEM),
            pl.BlockSpec((1, 1, ROW_TM), lambda i, t: (i, 0, 0), memory_space=pltpu.SMEM),
            pl.BlockSpec((ROW_TM, width), lambda i, t: (i, 0)),
        ],
        out_specs=pl.BlockSpec(memory_space=pl.ANY),
        scratch_shapes=[
            pltpu.VMEM((MOE_TM, width), vp.dtype),
            pltpu.SemaphoreType.DMA(()),
            pltpu.SemaphoreType.DMA(()),
        ],
    )
    return pl.pallas_call(
        _row_scatter_kernel,
        grid_spec=grid_spec,
        out_shape=jax.ShapeDtypeStruct((n_rows, width), vp.dtype),
        compiler_params=_cparams("arbitrary"),
        name="moe_row_scatter",
    )(tails, d0, d1, vp)


def _moe_kernel(te_ref, tf_ref, tv_ref, ti_ref, tn_ref, x_ref, w1_hbm, w3_hbm, w2_hbm, y_ref,
                s1, s3, s2, w1b, w3b, w2b, sems, *, layer):
    i = pl.program_id(0)
    valid = tv_ref[i] == 1

    def weight_copies(e):
        return (pltpu.make_async_copy(w1_hbm.at[layer, e], s1, sems.at[0]),
                pltpu.make_async_copy(w3_hbm.at[layer, e], s3, sems.at[1]),
                pltpu.make_async_copy(w2_hbm.at[layer, e], s2, sems.at[2]))

    @pl.when(i == 0)
    def _():
        for copy in weight_copies(te_ref[0]):
            copy.start()

    @pl.when(valid)
    def _():
        @pl.when(tf_ref[i] == 1)
        def _():
            for copy in weight_copies(te_ref[i]):
                copy.wait()
            for src, dst in ((s1, w1b), (s3, w3b), (s2, w2b)):
                rows_per = CAST_ROWS * 512 // src.shape[1]

                def cast_rows(c, carry, src=src, dst=dst, rows_per=rows_per):
                    rows = pl.ds(pl.multiple_of(c * rows_per, rows_per), rows_per)
                    dst[rows, :] = src[rows, :].astype(BF16)
                    return carry
                lax.fori_loop(0, src.shape[0] // rows_per, cast_rows, 0)

            @pl.when(tn_ref[i] >= 0)
            def _():
                for copy in weight_copies(tn_ref[i]):
                    copy.start()

        x = x_ref[...].astype(BF16)
        h1 = _dot(x, w1b[...])
        h3 = _dot(x, w3b[...])
        hh = (h1 * jax.nn.sigmoid(h1) * h3).astype(BF16)
        y_ref[...] = _dot(hh, w2b[...])

    @pl.when(jnp.logical_not(valid))
    def _():
        y_ref[...] = jnp.zeros_like(y_ref)


def _moe_experts(layer, x_sorted, tiles, w1, w3, w2):
    tile_expert, tile_first, tile_valid, tile_index, tile_next = tiles
    n_tiles = tile_expert.shape[0]
    d = x_sorted.shape[1]
    grid_spec = pltpu.PrefetchScalarGridSpec(
        num_scalar_prefetch=5,
        grid=(n_tiles,),
        in_specs=[
            pl.BlockSpec((MOE_TM, d), lambda i, te, tf, tv, ti, tn: (ti[i], 0)),
            pl.BlockSpec(memory_space=pl.ANY),
            pl.BlockSpec(memory_space=pl.ANY),
            pl.BlockSpec(memory_space=pl.ANY),
        ],
        out_specs=pl.BlockSpec((MOE_TM, d), lambda i, te, tf, tv, ti, tn: (i, 0)),
        scratch_shapes=[
            pltpu.VMEM((d, D_EXPERT), F32),
            pltpu.VMEM((d, D_EXPERT), F32),
            pltpu.VMEM((D_EXPERT, d), F32),
            pltpu.VMEM((d, D_EXPERT), BF16),
            pltpu.VMEM((d, D_EXPERT), BF16),
            pltpu.VMEM((D_EXPERT, d), BF16),
            pltpu.SemaphoreType.DMA((3,)),
        ],
    )
    return pl.pallas_call(
        functools.partial(_moe_kernel, layer=layer),
        grid_spec=grid_spec,
        out_shape=jax.ShapeDtypeStruct((n_tiles * MOE_TM, d), F32),
        compiler_params=_cparams("arbitrary"),
        name="moe_experts",
    )(tile_expert, tile_first, tile_valid, tile_index, tile_next, x_sorted, w1, w3, w2)


def _combine_kernel(d0_ref, d1_ref, n0_ref, n1_ref, h_ref, route_ref, mod_ref, *refs, with_next):
    if with_next:
        modn_ref, gn_ref, y_hbm, o_ref, u_ref, ybuf, sems = refs
    else:
        y_hbm, o_ref, ybuf, sems = refs
    i = pl.program_id(0)
    n_steps = pl.num_programs(0)

    def row_copies(slot, g, k, d0, d1):
        rows = pl.ds(pl.multiple_of(g * 8, 8) + k, 1)
        return (pltpu.make_async_copy(y_hbm.at[pl.ds(d0, 1), :], ybuf.at[slot, 0, rows, :], sems.at[slot]),
                pltpu.make_async_copy(y_hbm.at[pl.ds(d1, 1), :], ybuf.at[slot, 1, rows, :], sems.at[slot]))

    def request(slot, i0_ref, i1_ref):
        def body(g, carry):
            for k in range(8):
                for prio, copy in enumerate(row_copies(slot, g, k, i0_ref[0, 0, g * 8 + k], i1_ref[0, 0, g * 8 + k])):
                    copy.start(priority=prio)
            return carry
        lax.fori_loop(0, ROW_TM // 8, body, 0)

    def await_rows(slot):
        def body(g, carry):
            for k in range(8):
                for copy in row_copies(slot, g, k, 0, 0):
                    copy.wait()
            return carry
        lax.fori_loop(0, ROW_TM // 8, body, 0)

    slot = i % 2

    @pl.when(i == 0)
    def _():
        request(0, d0_ref, d1_ref)

    for s in range(2):
        @pl.when(jnp.logical_and(i + 1 < n_steps, slot == s))
        def _(s=s):
            request(1 - s, n0_ref, n1_ref)

    for s in range(2):
        @pl.when(slot == s)
        def _(s=s):
            await_rows(s)
    p0 = route_ref[:, ROUTE_P:ROUTE_P + 1]
    p1 = route_ref[:, ROUTE_P + 1:ROUTE_P + 2]
    o_ref[...] = h_ref[...] + mod_ref[0, 5:6, :] * (p0 * ybuf[slot, 0] + p1 * ybuf[slot, 1])
    if with_next:
        _norm_modulate_rows((o_ref,), 0, gn_ref, modn_ref, 0, 1, u_ref)


def _combine(h1, route, mod, y_sorted, dest, next_norm, *, n_tok, n_lat, seq, batch):
    d = h1.shape[1]
    tm = ROW_TM
    n_steps = n_tok // tm
    d0, d1 = dest
    with_next = next_norm is not None

    def mod_idx(i):
        return (jnp.where(i * tm < n_lat, (i * tm) // seq, batch), 0, 0)

    row_spec = pl.BlockSpec((tm, d), lambda i: (i, 0))
    mod_spec = pl.BlockSpec((1, N_MOD, d), mod_idx)
    next_specs = [mod_spec, pl.BlockSpec((1, d), lambda i: (0, 0))] if with_next else []
    next_args = list(next_norm) if with_next else []
    out = pl.pallas_call(
        functools.partial(_combine_kernel, with_next=with_next),
        grid=(n_steps,),
        in_specs=[
            pl.BlockSpec((1, 1, tm), lambda i: (i, 0, 0), memory_space=pltpu.SMEM),
            pl.BlockSpec((1, 1, tm), lambda i: (i, 0, 0), memory_space=pltpu.SMEM),
            pl.BlockSpec((1, 1, tm), lambda i: (jnp.minimum(i + 1, n_steps - 1), 0, 0), memory_space=pltpu.SMEM),
            pl.BlockSpec((1, 1, tm), lambda i: (jnp.minimum(i + 1, n_steps - 1), 0, 0), memory_space=pltpu.SMEM),
            row_spec,
            pl.BlockSpec((tm, LANES), lambda i: (i, 0)),
            mod_spec,
        ] + next_specs + [pl.BlockSpec(memory_space=pl.ANY)],
        out_specs=[row_spec, row_spec] if with_next else row_spec,
        out_shape=([jax.ShapeDtypeStruct((n_tok, d), F32), jax.ShapeDtypeStruct((n_tok, d), BF16)] if with_next
                   else jax.ShapeDtypeStruct((n_tok, d), F32)),
        scratch_shapes=[pltpu.VMEM((2, 2, tm, d), F32), pltpu.SemaphoreType.DMA((2,))],
        compiler_params=_cparams("arbitrary"),
        name="moe_combine",
    )(d0, d1, d0, d1, h1, route, mod, *next_args, y_sorted)
    return out if with_next else (out, None)


def _rope_tables(seq, tm):
    n_rows = seq // GRID_W
    row = jnp.repeat(jnp.arange(n_rows), GRID_W).astype(F32)
    col = jnp.tile(jnp.arange(GRID_W), n_rows).astype(F32)
    half = HEAD_DIM // 2
    inv = 1.0 / (ROPE_THETA ** (jnp.arange(0, half, 2, dtype=F32) / half))
    ang_r = row[:, None] * inv
    ang_c = col[:, None] * inv
    ang = jnp.concatenate([ang_r, ang_c, ang_r, ang_c], axis=-1)
    cos, sin = jnp.cos(ang), jnp.sin(ang)
    sin_signed = jnp.where(jnp.arange(HEAD_DIM) < half, -sin, sin)
    ident = jnp.zeros((tm, HEAD_DIM), F32)
    return jnp.concatenate([cos, ident + 1.0]), jnp.concatenate([sin_signed, ident])


def _permute_heads(a):
    lead = a.shape[:-1]
    quarter = HEAD_DIM // 4
    assert HEAD_PERM[quarter] == 2 * quarter and HEAD_PERM[2 * quarter] == quarter
    q = a.reshape(*lead, a.shape[-1] // HEAD_DIM, 4, quarter)
    q = jnp.stack([q[..., 0, :], q[..., 2, :], q[..., 1, :], q[..., 3, :]], axis=-2)
    return q.reshape(*lead, a.shape[-1])


def _score_bound(gq, gk):
    return (1.02 * HEAD_DIM * jnp.max(jnp.abs(gq)) * jnp.max(jnp.abs(gk))).reshape(1).astype(F32)


def kernel(x, c, ctx, c_ctx, w_ada, b_ada, norm1_g, norm2_g, w_in, b_gate, q_norm_a, k_norm_a, diff_lambda,
           sub_norm_a, q_norm_c, k_norm_c, conv_w, conv_b, lru_w_a, lru_b_a, lru_w_x, lru_b_x, lru_lambda,
           w_branch_a, w_branch_b, w_branch_c, w_out, w_group, b_group, w_route, b_route, w1, w3, w2):
    batch, seq, d = x.shape
    ctx_len = ctx.shape[1]
    depth = w_ada.shape[0]
    n_lat = batch * seq
    n_ctx = batch * ctx_len
    n_all = n_lat + n_ctx
    tm = min(1024, seq, n_ctx)
    assert seq % tm == 0 and n_ctx % tm == 0 and n_lat % ctx_len == 0 and batch < MOD_ROWS
    assert seq % GRID_W == 0 and seq % ROW_TM == 0 and n_ctx % ROW_TM == 0
    tq = min(512, seq)
    tm_out = min(512, tm)

    cc = jnp.zeros((MOD_ROWS, d), F32).at[:batch].set(c).at[batch].set(c_ctx)
    mod_all = _ada_modulation(cc, w_ada, b_ada).reshape(depth, MOD_ROWS, N_MOD, d)
    rope = _rope_tables(seq, tm)
    h_src = (x.reshape(n_lat, d), ctx.reshape(n_ctx, d))
    scale = HEAD_DIM ** -0.5

    for l in range(depth):
        last = l == depth - 1
        lambda_init = 0.8 - 0.6 * math.exp(-0.3 * l)
        mod = mod_all[l]
        t_av, t_cq, t_gate = 2048 // PREP_TN, 5120 // PREP_TN, 6656 // PREP_TN
        w_qk = _prepare_weight(w_in, l, QK_COLS, lambda j: jnp.where(j < t_av, j, j + t_cq - t_av),
                               QK_CV // PREP_TN, "prep_w_qk")
        w_vb = _prepare_weight(w_in, l, VB_COLS, lambda j: j + t_av, 0, "prep_w_vb")
        w_gate = _prepare_weight(w_in, l, N_BRANCH * d, lambda j: j + t_gate, 0, "prep_w_gate")
        gq_a, gq_c = q_norm_a[l] * scale, q_norm_c[l] * scale
        gcol = jnp.concatenate([
            _permute_heads(jnp.concatenate([
                jnp.tile(gq_a, 2 * DIFF_HEADS), jnp.tile(k_norm_a[l], 2 * DIFF_HEADS),
                jnp.tile(gq_c, GQA_HEADS), jnp.tile(k_norm_c[l], GQA_KV_HEADS)])),
            jnp.ones((QK_COLS - QK_CV,), F32)]).reshape(1, QK_COLS)
        bound_a = _score_bound(gq_a, k_norm_a[l])
        bound_c = _score_bound(gq_c, k_norm_c[l])

        if l == 0:
            u = _norm_modulate_stream(h_src, mod, norm1_g[l].reshape(1, d), n_all=n_all, tm=tm,
                                      n_lat_tiles=n_lat // tm, tiles_per_seq=seq // tm, batch=batch)
        qk, vb = _in_projection(u, w_qk, w_vb, gcol, rope, tm=tm, n_lat_tiles=n_lat // tm,
                                tiles_per_seq=seq // tm)
        dims = dict(batch=batch, seq=seq, ctx=ctx_len)
        extra_a = (diff_lambda[l], sub_norm_a[l].reshape(1, DIFF_V_DIM))
        ya = _attention("diff", qk, vb, bound_a, extra_a, tq=tq, latent=True, lambda_init=lambda_init, **dims)
        yc = _attention("gqa", qk, vb, bound_c, None, tq=tq, latent=True, **dims)
        yb, yb_c = _rglru(vb, conv_w[l], conv_b[l].reshape(1, LRU_WIDTH), 0.5 * lru_w_a[l], 0.5 * lru_b_a[l],
                          0.5 * lru_w_x[l], 0.5 * lru_b_x[l], lru_lambda[l], need_ctx=not last, **dims)
        if last:
            n_tok, y_ctx = n_lat, None
        else:
            n_tok = n_all
            ya_c = _attention("diff", qk, vb, bound_a, extra_a, tq=tq, latent=False, lambda_init=lambda_init,
                              **dims)
            yc_c = _attention("gqa", qk, vb, bound_c, None, tq=tq, latent=False, **dims)
            y_ctx = (ya_c, yb_c, yc_c)
        w_branch = (w_branch_a[l].astype(BF16), w_branch_b[l].astype(BF16), w_branch_c[l].astype(BF16))
        m = _merge(u, (ya, yb, yc), y_ctx, w_gate, b_gate[l], w_branch, n_tok=n_tok, n_lat=n_lat, tm=tm)
        w_router = jnp.zeros((d, LANES), F32).at[:, :N_GROUPS].set(w_group[l])
        w_router = w_router.at[:, N_GROUPS:N_GROUPS + N_EXPERTS].set(w_route[l]).astype(BF16)
        b_router = jnp.zeros((1, LANES), F32).at[0, :N_GROUPS].set(b_group[l])
        b_router = b_router.at[0, N_GROUPS:N_GROUPS + N_EXPERTS].set(b_route[l])
        h1, vp, route, counts = _out_projection(m, h_src, mod, norm2_g[l].reshape(1, d),
                                                w_out[l].astype(BF16), w_router, b_router, n_tok=n_tok,
                                                tm=tm_out, n_lat=n_lat, seq=seq, batch=batch)
        dest, tiles, tails = _dispatch(route, counts, n_tok, tm_out)
        x_sorted = _row_scatter(vp, dest, tails, tiles[0].shape[0] * MOE_TM)
        y_sorted = _moe_experts(l, x_sorted, tiles, w1, w3, w2)
        next_norm = None if last else (mod_all[l + 1], norm1_g[l + 1].reshape(1, d))
        h_src, u = _combine(h1, route, mod, y_sorted, dest, next_norm, n_tok=n_tok, n_lat=n_lat, seq=seq,
                            batch=batch)
    return h_src.reshape(batch, seq, d)
```
